```python
import jax, jax.numpy as jnp
from jax import lax
import numpy as np

D_MODEL = 1024
BATCH = 8
SEQ = 2048
DEPTH = 1
DEC_BATCH = 128
DEC_SEQ = 4
PAST_LEN = 16384
PAGE_SIZE = 128

RW_HEADS = 8
RW_HEAD_DIM = 64
RW_WIDTH = RW_HEADS * RW_HEAD_DIM
DECAY_LORA = 64
ICLR_LORA = 64
GATE_LORA = 128
RW_PROJ = 3 * RW_WIDTH + DECAY_LORA + ICLR_LORA + GATE_LORA
GN_EPS = 64e-5
CONV_WIDTH = D_MODEL - RW_WIDTH
CONV_GROUPS = 8
CONV_K = 31
LN_EPS = 1e-5
MIX_WIDTH = RW_WIDTH + CONV_WIDTH
IN_PROJ = RW_PROJ + 2 * CONV_WIDTH
N_MEM = 256
X_HEADS = 4
X_HEAD_DIM = D_MODEL // X_HEADS
ATTN_SCALE = X_HEAD_DIM ** -0.5
N_EXPERTS = 32
TOP_K = 4
D_EXPERT = D_MODEL
SWIGLU_LIMIT = 7.0
SWIGLU_ALPHA = 1.702
NORM_EPS = 1e-5

kernel_name = "hybrid_rwkv7_conformer_memxattn_moe_step"


def rms_norm(x, g):
    xf = x.astype(jnp.float32)
    y = xf * lax.rsqrt(jnp.mean(xf * xf, axis=-1, keepdims=True) + NORM_EPS)
    return (y * g.astype(jnp.float32)).astype(x.dtype)


def _wkv7_step(S, inp):
    r_t, d_t, k_t, v_t, a_t, b_t = inp
    sa = jnp.einsum('bhij,bhj->bhi', S, a_t)
    S = S * d_t[:, :, None, :] + sa[..., None] * b_t[:, :, None, :] + v_t[..., None] * k_t[:, :, None, :]
    y = jnp.einsum('bhij,bhj->bhi', S, r_t)
    return S, y


def rwkv7_group(p, prev, s0, mu, w0, w_dup, a0, a_up, g_up, k_k, k_a, r_k, gn_g, gn_b):
    B, T, _ = p.shape
    H, N = RW_HEADS, RW_HEAD_DIM
    f32 = jnp.float32
    ps = jnp.concatenate([prev[:, None, :], p[:, :-1, :]], axis=1)
    z = p + mu * (ps - p)
    c1, c2, c3 = RW_WIDTH, 2 * RW_WIDTH, 3 * RW_WIDTH
    c4 = c3 + DECAY_LORA
    c5 = c4 + ICLR_LORA
    r, k, v = z[..., :c1], z[..., c1:c2], z[..., c2:c3]
    wd, ad, gd = z[..., c3:c4], z[..., c4:c5], z[..., c5:]
    w = -jax.nn.softplus(-(w0 + jnp.tanh(wd) @ w_dup)) - 0.5
    a = jax.nn.sigmoid(a0 + ad @ a_up)
    g = jax.nn.sigmoid(gd) @ g_up
    heads = lambda t: t.reshape(B, T, H, N).astype(f32)
    r, w, k, v, a = heads(r), heads(w), heads(k), heads(v), heads(a)
    kk = k * k_k.reshape(H, N).astype(f32)
    kk = kk / jnp.maximum(jnp.sqrt(jnp.sum(kk * kk, axis=-1, keepdims=True)), 1e-12)
    k = k * (1.0 + (a - 1.0) * k_a.reshape(H, N).astype(f32))
    decay = jnp.exp(-jnp.exp(w))
    xs = tuple(jnp.moveaxis(t, 1, 0) for t in (r, decay, k, v, -kk, kk * a))
    s_fin, y = lax.scan(_wkv7_step, s0.astype(f32), xs)
    y = jnp.moveaxis(y, 0, 1)
    mean = jnp.mean(y, axis=-1, keepdims=True)
    var = jnp.mean(jnp.square(y - mean), axis=-1, keepdims=True)
    yn = (y - mean) * lax.rsqrt(var + GN_EPS) * gn_g.reshape(H, N).astype(f32) + gn_b.reshape(H, N).astype(f32)
    bonus = jnp.sum(r * k * r_k.astype(f32), axis=-1, keepdims=True) * v
    out = ((yn + bonus).reshape(B, T, RW_WIDTH) * g.astype(f32)).astype(p.dtype)
    return out, p[:, -1, :], s_fin.astype(s0.dtype)


def conv_group(p, buf, dw_w, dw_b, ln_g, ln_b):
    u = p[..., :CONV_WIDTH] * jax.nn.sigmoid(p[..., CONV_WIDTH:])
    ext = jnp.concatenate([buf.astype(u.dtype), u], axis=1)
    c = lax.conv_general_dilated(ext, dw_w[:, None, :].astype(u.dtype), window_strides=(1,), padding='VALID',
                                 dimension_numbers=('NWC', 'WIO', 'NWC'),
                                 feature_group_count=CONV_WIDTH) + dw_b
    cf = c.astype(jnp.float32)
    mean = jnp.mean(cf, axis=-1, keepdims=True)
    var = jnp.mean(jnp.square(cf - mean), axis=-1, keepdims=True)
    cn = (cf - mean) * lax.rsqrt(var + LN_EPS) * ln_g.astype(jnp.float32) + ln_b.astype(jnp.float32)
    return jax.nn.silu(cn).astype(p.dtype), ext[:, -(CONV_K - 1):, :]


def memory_kv(mem, g_mem, w_mk, w_mv):
    B = mem.shape[0]
    m = rms_norm(mem, g_mem)
    k = (m @ w_mk).reshape(B, N_MEM, X_HEADS, X_HEAD_DIM)
    v = (m @ w_mv).reshape(B, N_MEM, X_HEADS, X_HEAD_DIM)
    return k, v


def cross_attn(h, mk, mv, w_q, w_o):
    B, T, _ = h.shape
    f32 = jnp.float32
    q = (h @ w_q).reshape(B, T, X_HEADS, X_HEAD_DIM)
    s = jnp.einsum('bthd,bmhd->bhtm', q.astype(f32), mk.astype(f32)) * ATTN_SCALE
    pr = jax.nn.softmax(s, axis=-1)
    o = jnp.einsum('bhtm,bmhd->bthd', pr, mv.astype(f32)).reshape(B, T, D_MODEL).astype(h.dtype)
    return o @ w_o


def moe(h, w_router, b_router, w_gate, b_gate, w_up, b_up, w_down, b_down):
    B, T, D = h.shape
    t = h.reshape(B * T, D)
    logits = (t @ w_router + b_router).astype(jnp.float32)
    top_v, top_i = lax.top_k(logits, TOP_K)
    probs = jax.nn.softmax(top_v, axis=-1)
    comb = jnp.sum(jax.nn.one_hot(top_i, N_EXPERTS, dtype=jnp.float32) * probs[..., None], axis=1)
    y = jnp.zeros((B * T, D), jnp.float32)
    for e in range(N_EXPERTS):
        gate = jnp.minimum(t @ w_gate[e] + b_gate[e], SWIGLU_LIMIT)
        up = jnp.clip(t @ w_up[e] + b_up[e], -SWIGLU_LIMIT, SWIGLU_LIMIT)
        hid = (up + 1.0) * gate * jax.nn.sigmoid(gate * SWIGLU_ALPHA)
        y = y + comb[:, e:e + 1] * (hid @ w_down[e] + b_down[e]).astype(jnp.float32)
    return y.astype(h.dtype).reshape(B, T, D)


def layer(x, mk, mv, prev, s0, buf, lw):
    h = rms_norm(x, lw['g_mix'])
    p = h @ lw['w_in']
    rw, new_prev, new_s = rwkv7_group(p[..., :RW_PROJ], prev, s0, lw['mu_shift'], lw['w0'], lw['w_decay_up'],
                                      lw['a0'], lw['w_iclr_up'], lw['w_glora_up'], lw['k_k'], lw['k_a'],
                                      lw['r_k'], lw['gn_g'], lw['gn_b'])
    cv, new_buf = conv_group(p[..., RW_PROJ:], buf, lw['dw_w'], lw['dw_b'], lw['cln_g'], lw['cln_b'])
    x = x + jnp.concatenate([rw, cv], axis=-1) @ lw['w_out']
    x = x + cross_attn(rms_norm(x, lw['g_xattn']), mk, mv, lw['w_q'], lw['w_o'])
    x = x + moe(rms_norm(x, lw['g_moe']), lw['w_router'], lw['b_router'], lw['w_moe_gate'], lw['b_moe_gate'],
                lw['w_moe_up'], lw['b_moe_up'], lw['w_moe_down'], lw['b_moe_down'])
    return x, new_prev, new_s, new_buf


def setup_inputs(seed: int = 0) -> dict:
    key = jax.random.key(seed)
    ks = iter(jax.random.split(key, 64))
    nrm = lambda shape, s: jax.random.normal(next(ks), shape, jnp.float32) * s
    L, D, E, F = DEPTH, D_MODEL, N_EXPERTS, D_EXPERT
    return {
        'x_prompt': nrm((BATCH, SEQ, D), 1.0),
        'x_sample': nrm((DEC_BATCH, DEC_SEQ, D), 1.0),
        'mem_prompt': nrm((BATCH, N_MEM, D), 1.0),
        'state_shift': nrm((L, DEC_BATCH, RW_PROJ), 1.0),
        'state_wkv': nrm((L, DEC_BATCH, RW_HEADS, RW_HEAD_DIM, RW_HEAD_DIM), 1.0),
        'state_conv': nrm((L, DEC_BATCH, CONV_K - 1, CONV_WIDTH), 0.5),
        'cache_mem_k': nrm((L, DEC_BATCH, N_MEM, X_HEADS, X_HEAD_DIM), 1.0),
        'cache_mem_v': nrm((L, DEC_BATCH, N_MEM, X_HEADS, X_HEAD_DIM), 1.0),
        'g_mix': 1.0 + nrm((L, D), 0.02),
        'w_in': nrm((L, D, IN_PROJ), D ** -0.5),
        'mu_shift': jax.random.uniform(next(ks), (L, RW_PROJ), jnp.float32, 0.0, 1.0),
        'w0': jax.random.uniform(next(ks), (L, RW_WIDTH), jnp.float32, -6.0, -0.5),
        'w_decay_up': nrm((L, DECAY_LORA, RW_WIDTH), 0.1),
        'a0': nrm((L, RW_WIDTH), 0.1),
        'w_iclr_up': nrm((L, ICLR_LORA, RW_WIDTH), 0.5 * ICLR_LORA ** -0.5),
        'w_glora_up': nrm((L, GATE_LORA, RW_WIDTH), GATE_LORA ** -0.5),
        'k_k': 0.85 + nrm((L, RW_WIDTH), 0.02),
        'k_a': 1.0 + nrm((L, RW_WIDTH), 0.02),
        'r_k': nrm((L, RW_HEADS, RW_HEAD_DIM), 0.1),
        'gn_g': 1.0 + nrm((L, RW_WIDTH), 0.02),
        'gn_b': nrm((L, RW_WIDTH), 0.01),
        'dw_w': nrm((L, CONV_K, CONV_WIDTH), CONV_K ** -0.5),
        'dw_b': nrm((L, CONV_WIDTH), 0.01),
        'cln_g': 1.0 + nrm((L, CONV_WIDTH), 0.02),
        'cln_b': nrm((L, CONV_WIDTH), 0.01),
        'w_out': nrm((L, MIX_WIDTH, D), MIX_WIDTH ** -0.5),
        'g_xattn': 1.0 + nrm((L, D), 0.02),
        'g_mem': 1.0 + nrm((L, D), 0.02),
        'w_q': nrm((L, D, D), D ** -0.5),
        'w_mk': nrm((L, D, D), D ** -0.5),
        'w_mv': nrm((L, D, D), D ** -0.5),
        'w_o': nrm((L, D, D), D ** -0.5),
        'g_moe': 1.0 + nrm((L, D), 0.02),
        'w_router': nrm((L, D, E), D ** -0.5),
        'b_router': nrm((L, E), 0.01),
        'w_moe_gate': nrm((L, E, D, F), D ** -0.5),
        'b_moe_gate': nrm((L, E, F), 0.01),
        'w_moe_up': nrm((L, E, D, F), D ** -0.5),
        'b_moe_up': nrm((L, E, F), 0.01),
        'w_moe_down': nrm((L, E, F, D), F ** -0.5),
        'b_moe_down': nrm((L, E, D), 0.01),
        'g_final': 1.0 + nrm((D,), 0.02),
    }


def reference(x_prompt, x_sample, mem_prompt, state_shift, state_wkv, state_conv, cache_mem_k, cache_mem_v,
              g_mix, w_in, mu_shift, w0, w_decay_up, a0, w_iclr_up, w_glora_up, k_k, k_a, r_k, gn_g, gn_b,
              dw_w, dw_b, cln_g, cln_b, w_out, g_xattn, g_mem, w_q, w_mk, w_mv, w_o,
              g_moe, w_router, b_router, w_moe_gate, b_moe_gate, w_moe_up, b_moe_up, w_moe_down, b_moe_down,
              g_final):
    bp = x_prompt.shape[0]
    dt = x_prompt.dtype
    xp, xs = x_prompt, x_sample
    sh_p_l, wkv_p_l, cv_p_l, mk_p_l, mv_p_l = [], [], [], [], []
    sh_s_l, wkv_s_l, cv_s_l = [], [], []
    for l in range(DEPTH):
        lw = dict(g_mix=g_mix[l], w_in=w_in[l], mu_shift=mu_shift[l], w0=w0[l], w_decay_up=w_decay_up[l],
                  a0=a0[l], w_iclr_up=w_iclr_up[l], w_glora_up=w_glora_up[l], k_k=k_k[l], k_a=k_a[l],
                  r_k=r_k[l], gn_g=gn_g[l], gn_b=gn_b[l], dw_w=dw_w[l], dw_b=dw_b[l], cln_g=cln_g[l],
                  cln_b=cln_b[l], w_out=w_out[l], g_xattn=g_xattn[l], w_q=w_q[l], w_o=w_o[l],
                  g_moe=g_moe[l], w_router=w_router[l], b_router=b_router[l], w_moe_gate=w_moe_gate[l],
                  b_moe_gate=b_moe_gate[l], w_moe_up=w_moe_up[l], b_moe_up=b_moe_up[l],
                  w_moe_down=w_moe_down[l], b_moe_down=b_moe_down[l])
        mk_p, mv_p = memory_kv(mem_prompt, g_mem[l], w_mk[l], w_mv[l])
        prev0 = jnp.zeros((bp, RW_PROJ), dt)
        s00 = jnp.zeros((bp, RW_HEADS, RW_HEAD_DIM, RW_HEAD_DIM), dt)
        buf0 = jnp.zeros((bp, CONV_K - 1, CONV_WIDTH), dt)
        xp, sh_p, wkv_p, cv_p = layer(xp, mk_p, mv_p, prev0, s00, buf0, lw)
        xs, sh_s, wkv_s, cv_s = layer(xs, cache_mem_k[l], cache_mem_v[l], state_shift[l], state_wkv[l],
                                      state_conv[l], lw)
        sh_p_l.append(sh_p); wkv_p_l.append(wkv_p); cv_p_l.append(cv_p)
        mk_p_l.append(mk_p); mv_p_l.append(mv_p)
        sh_s_l.append(sh_s); wkv_s_l.append(wkv_s); cv_s_l.append(cv_s)
    y_prompt = rms_norm(xp, g_final)
    y_sample = rms_norm(xs, g_final)
    return (y_prompt, y_sample,
            jnp.stack(sh_p_l), jnp.stack(wkv_p_l), jnp.stack(cv_p_l), jnp.stack(mk_p_l), jnp.stack(mv_p_l),
            jnp.stack(sh_s_l), jnp.stack(wkv_s_l), jnp.stack(cv_s_l))
```

```python
import functools

import jax
import jax.numpy as jnp
from jax import lax
from jax.experimental import pallas as pl
from jax.experimental.pallas import tpu as pltpu

F32 = jnp.float32
BF16 = jnp.bfloat16
I32 = jnp.int32

D_MODEL = 1024
RW_HEADS = 8
RW_HEAD_DIM = 64
RW_WIDTH = RW_HEADS * RW_HEAD_DIM
DECAY_LORA = 64
ICLR_LORA = 64
GATE_LORA = 128
RW_PROJ = 3 * RW_WIDTH + DECAY_LORA + ICLR_LORA + GATE_LORA
GN_EPS = 64e-5
CONV_WIDTH = D_MODEL - RW_WIDTH
CONV_K = 31
LN_EPS = 1e-5
IN_PROJ = RW_PROJ + 2 * CONV_WIDTH
N_MEM = 256
X_HEADS = 4
X_HEAD_DIM = D_MODEL // X_HEADS
ATTN_SCALE = X_HEAD_DIM ** -0.5
N_EXPERTS = 32
TOP_K = 4
SWIGLU_LIMIT = 7.0
SWIGLU_ALPHA = 1.702
NORM_EPS = 1e-5

TOK_TILE = 512
WKV_BLOCK = 256
WKV_CHUNK = 64
SAMPLE_PAD = 8
FFN_TILE = 256
COMB_TILE = 128
CONV_PAD = 32
VMEM_LIMIT = 48 * 1024 * 1024


def _cparams(sem):
    return pltpu.CompilerParams(dimension_semantics=sem, vmem_limit_bytes=VMEM_LIMIT)


def _dot(a, b):
    return jnp.dot(a.astype(BF16), b.astype(BF16), preferred_element_type=F32)


def _dot_nt(a, b):
    return lax.dot_general(a.astype(BF16), b.astype(BF16), (((1,), (1,)), ((), ())),
                           preferred_element_type=F32)


def _dot_tn(a, b):
    return lax.dot_general(a.astype(BF16), b.astype(BF16), (((0,), (0,)), ((), ())),
                           preferred_element_type=F32)


def _split2(x):
    hi = x.astype(BF16)
    lo = (x - hi.astype(F32)).astype(BF16)
    return hi, lo


def _split3(x):
    hi = x.astype(BF16)
    r1 = x - hi.astype(F32)
    mid = r1.astype(BF16)
    lo = (r1 - mid.astype(F32)).astype(BF16)
    return hi, mid, lo


def _segsum(x, eseg):
    hi, lo = _split2(x)
    return (jnp.dot(hi, eseg, preferred_element_type=F32)
            + jnp.dot(lo, eseg, preferred_element_type=F32))


def _rms(x, g):
    return x * lax.rsqrt(jnp.mean(x * x, axis=-1, keepdims=True) + NORM_EPS) * g


def _sigmoid(x):
    return 1.0 / (1.0 + jnp.exp(-x))


def _inproj_kernel(x_ref, g_ref, w_ref, prw_ref, u_ref):
    h = _rms(x_ref[...], g_ref[...])
    p = _dot(h, w_ref[...])
    prw_ref[...] = p[:, :RW_PROJ]
    u_ref[...] = p[:, RW_PROJ:RW_PROJ + CONV_WIDTH] * _sigmoid(p[:, RW_PROJ + CONV_WIDTH:])


def _inproj(x, g_mix, w_in_bf):
    n = x.shape[0]
    return pl.pallas_call(
        _inproj_kernel,
        grid=(n // TOK_TILE,),
        in_specs=[pl.BlockSpec((TOK_TILE, D_MODEL), lambda i: (i, 0)),
                  pl.BlockSpec((1, D_MODEL), lambda i: (0, 0)),
                  pl.BlockSpec((D_MODEL, IN_PROJ), lambda i: (0, 0))],
        out_specs=[pl.BlockSpec((TOK_TILE, RW_PROJ), lambda i: (i, 0)),
                   pl.BlockSpec((TOK_TILE, CONV_WIDTH), lambda i: (i, 0))],
        out_shape=[jax.ShapeDtypeStruct((n, RW_PROJ), F32),
                   jax.ShapeDtypeStruct((n, CONV_WIDTH), F32)],
        compiler_params=_cparams(("parallel",)),
        name="inproj",
    )(x, g_mix, w_in_bf)


def _wkv_kernel(p_ref, prev_ref, s0_ref, mu_ref, w0_ref, wdup_ref, a0_ref, aup_ref, gup_ref,
                kk_ref, ka_ref, rk_ref, gng_ref, gnb_ref, eseg_ref,
                out_ref, sfin_ref,
                s_ref, last_ref, rt_ref, at_ref, bt_ref, kt_ref, v_ref, pc_ref, y_ref,
                r_ref, km_ref, g_ref, *, tt, chunk, t_valid, n_t):
    t = pl.program_id(1)
    n_chunk = tt // chunk
    n_sq = chunk.bit_length() - 2
    c1, c2, c3 = RW_WIDTH, 2 * RW_WIDTH, 3 * RW_WIDTH
    c4 = c3 + DECAY_LORA
    c5 = c4 + ICLR_LORA

    @pl.when(t == 0)
    def _():
        s_ref[...] = s0_ref[0]
        last_ref[0:1, :] = prev_ref[0]

    p = p_ref[...]
    row = lax.broadcasted_iota(I32, (tt, 1), 0)
    ps = jnp.where(row == 0, last_ref[0:1, :], pltpu.roll(p, shift=1, axis=0))
    last_ref[0:1, :] = p[tt - 1:tt, :]
    z = p + mu_ref[...] * (ps - p)
    r, k, v = z[:, :c1], z[:, c1:c2], z[:, c2:c3]
    wd, ad, gd = z[:, c3:c4], z[:, c4:c5], z[:, c5:]
    wpre = w0_ref[...] + _dot(jnp.tanh(wd), wdup_ref[...])
    neg = -wpre
    w = -(jnp.maximum(neg, 0.0) + jnp.log(1.0 + jnp.exp(-jnp.abs(neg)))) - 0.5
    a = _sigmoid(a0_ref[...] + _dot(ad, aup_ref[...]))
    g_ref[...] = _dot(_sigmoid(gd), gup_ref[...])
    eseg = eseg_ref[...]
    kk = k * kk_ref[...]
    kk = kk / jnp.maximum(jnp.sqrt(_segsum(kk * kk, eseg)), 1e-12)
    kmod = k * (1.0 + (a - 1.0) * ka_ref[...])
    logd = -jnp.exp(w)
    if t_valid < tt * n_t:
        valid = (t * tt + row) < t_valid
        logd = jnp.where(valid, logd, 0.0)
        kk = jnp.where(valid, kk, 0.0)
        kmod = jnp.where(valid, kmod, 0.0)
        v = jnp.where(valid, v, 0.0)
    ri = lax.broadcasted_iota(I32, (tt, tt), 0)
    ci = lax.broadcasted_iota(I32, (tt, tt), 1)
    shift = chunk.bit_length() - 1
    ltri = jnp.where(ci <= ri, jnp.where((ri >> shift) == (ci >> shift), 1.0, 0.0), 0.0).astype(BF16)
    hi, mid, lo = _split3(logd)
    logp = (jnp.dot(ltri, hi, preferred_element_type=F32)
            + jnp.dot(ltri, mid, preferred_element_type=F32)
            + jnp.dot(ltri, lo, preferred_element_type=F32))
    pcum = jnp.exp(logp)
    pinv = jnp.exp(-logp)
    r_ref[...] = r
    km_ref[...] = kmod
    v_ref[...] = v
    pc_ref[...] = pcum
    rt_ref[...] = r * pcum
    at_ref[...] = -kk * jnp.exp(logp - logd)
    bt_ref[...] = kk * a * pinv
    kt_ref[...] = kmod * pinv

    ti = lax.broadcasted_iota(I32, (chunk, chunk), 0)
    si = lax.broadcasted_iota(I32, (chunk, chunk), 1)
    strict = si < ti
    incl = si <= ti
    eye_c = jnp.where(si == ti, 1.0, 0.0)
    ji = lax.broadcasted_iota(I32, (RW_HEAD_DIM, RW_HEAD_DIM), 0)
    jj = lax.broadcasted_iota(I32, (RW_HEAD_DIM, RW_HEAD_DIM), 1)
    eye_n = jnp.where(ji == jj, 1.0, 0.0)

    def chunk_body(c, carry):
        r0 = c * chunk if n_chunk == 1 else pl.multiple_of(c * chunk, chunk)
        rows = pl.ds(r0, chunk)
        for h in range(RW_HEADS):
            hs = slice(h * RW_HEAD_DIM, (h + 1) * RW_HEAD_DIM)
            am, rm, bm, km, vm = at_ref[rows, hs], rt_ref[rows, hs], bt_ref[rows, hs], kt_ref[rows, hs], v_ref[rows, hs]
            a_ab = jnp.where(strict, _dot_nt(am, bm), 0.0)
            a_ak = jnp.where(strict, _dot_nt(am, km), 0.0)
            a_rb = jnp.where(incl, _dot_nt(rm, bm), 0.0)
            a_rk = jnp.where(incl, _dot_nt(rm, km), 0.0)
            tinv = eye_c + a_ab
            pw = a_ab
            for _ in range(n_sq):
                pw = _dot(pw, pw)
                tinv = tinv + _dot(tinv, pw)
            ap = _dot(tinv, am)
            uv = _dot(tinv, _dot(a_ak, vm))
            rp = rm + _dot(a_rb, ap)
            yv = _dot(a_rb, uv) + _dot(a_rk, vm)
            pc = pc_ref[pl.ds(r0 + chunk - 8, 8), hs][7:8, :]
            mm = (eye_n + _dot_tn(ap, bm)) * pc
            gm = (_dot_tn(uv, bm) + _dot_tn(vm, km)) * pc
            s0 = s_ref[h]
            y_ref[rows, hs] = _dot_nt(rp, s0) + yv
            s_ref[h] = _dot(s0, mm) + gm
        return carry

    if n_chunk == 1:
        chunk_body(0, 0)
    else:
        lax.fori_loop(0, n_chunk, chunk_body, 0)

    y = y_ref[...]
    inv_n = 1.0 / RW_HEAD_DIM
    mean = _segsum(y, eseg) * inv_n
    yc = y - mean
    var = _segsum(yc * yc, eseg) * inv_n
    yn = yc * lax.rsqrt(var + GN_EPS) * gng_ref[...] + gnb_ref[...]
    bonus = _segsum(r_ref[...] * km_ref[...] * rk_ref[...], eseg) * v_ref[...]
    out_ref[...] = (yn + bonus) * g_ref[...]

    @pl.when(t == n_t - 1)
    def _():
        sfin_ref[0] = s_ref[...]


def _wkv(p_rows, row_block0, prev, s0, wts, *, batch, tt, chunk, t_valid, n_t, out_rows):
    (mu, w0, wdup, a0, aup, gup, k_k, k_a, r_k, gn_g, gn_b, eseg) = wts
    kern = functools.partial(_wkv_kernel, tt=tt, chunk=chunk, t_valid=t_valid, n_t=n_t)
    const = lambda shape: pl.BlockSpec(shape, lambda b, t: tuple(0 for _ in shape))
    blk = pltpu.VMEM((tt, RW_WIDTH), F32)
    return pl.pallas_call(
        kern,
        grid=(batch, n_t),
        in_specs=[pl.BlockSpec((tt, RW_PROJ), lambda b, t: (row_block0 + b * n_t + t, 0)),
                  pl.BlockSpec((1, 1, RW_PROJ), lambda b, t: (b, 0, 0)),
                  pl.BlockSpec((1, RW_HEADS, RW_HEAD_DIM, RW_HEAD_DIM), lambda b, t: (b, 0, 0, 0)),
                  const((1, RW_PROJ)), const((1, RW_WIDTH)), const((DECAY_LORA, RW_WIDTH)),
                  const((1, RW_WIDTH)), const((ICLR_LORA, RW_WIDTH)), const((GATE_LORA, RW_WIDTH)),
                  const((1, RW_WIDTH)), const((1, RW_WIDTH)), const((1, RW_WIDTH)),
                  const((1, RW_WIDTH)), const((1, RW_WIDTH)), const((RW_WIDTH, RW_WIDTH))],
        out_specs=[pl.BlockSpec((tt, RW_WIDTH), lambda b, t: (b * n_t + t, 0)),
                   pl.BlockSpec((1, RW_HEADS, RW_HEAD_DIM, RW_HEAD_DIM), lambda b, t: (b, 0, 0, 0))],
        out_shape=[jax.ShapeDtypeStruct((out_rows, RW_WIDTH), F32),
                   jax.ShapeDtypeStruct((batch, RW_HEADS, RW_HEAD_DIM, RW_HEAD_DIM), F32)],
        scratch_shapes=[pltpu.VMEM((RW_HEADS, RW_HEAD_DIM, RW_HEAD_DIM), F32),
                        pltpu.VMEM((8, RW_PROJ), F32),
                        blk, blk, blk, blk, blk, blk, blk, blk, blk, blk],
        compiler_params=_cparams(("parallel", "arbitrary")),
        name="wkv",
    )(p_rows, prev, s0, mu, w0, wdup, a0, aup, gup, k_k, k_a, r_k, gn_g, gn_b, eseg)


def _conv_kernel(u_ref, buf_ref, w_ref, b_ref, g_ref, beta_ref, o_ref, ext_ref, *, tt, n_t):
    t = pl.program_id(1)

    @pl.when(t == 0)
    def _():
        ext_ref[0:CONV_PAD, :] = buf_ref[0]

    ext_ref[CONV_PAD:CONV_PAD + tt, :] = u_ref[...]
    acc = jnp.zeros((tt, CONV_WIDTH), F32) + b_ref[...]
    first = CONV_PAD - (CONV_K - 1)
    for j in range(CONV_K):
        acc = acc + ext_ref[pl.ds(first + j, tt), :] * w_ref[j:j + 1, :]
    mean = jnp.mean(acc, axis=-1, keepdims=True)
    cen = acc - mean
    var = jnp.mean(cen * cen, axis=-1, keepdims=True)
    cn = cen * lax.rsqrt(var + LN_EPS) * g_ref[...] + beta_ref[...]
    o_ref[...] = cn * _sigmoid(cn)
    if n_t > 1:
        ext_ref[0:CONV_PAD, :] = ext_ref[tt:tt + CONV_PAD, :]


def _conv(u_rows, row_block0, buf, dw_w, dw_b, ln_g, ln_b, *, batch, tt, n_t, out_rows):
    kern = functools.partial(_conv_kernel, tt=tt, n_t=n_t)
    const = lambda shape: pl.BlockSpec(shape, lambda b, t: tuple(0 for _ in shape))
    return pl.pallas_call(
        kern,
        grid=(batch, n_t),
        in_specs=[pl.BlockSpec((tt, CONV_WIDTH), lambda b, t: (row_block0 + b * n_t + t, 0)),
                  pl.BlockSpec((1, CONV_PAD, CONV_WIDTH), lambda b, t: (b, 0, 0)),
                  const((CONV_K, CONV_WIDTH)), const((1, CONV_WIDTH)),
                  const((1, CONV_WIDTH)), const((1, CONV_WIDTH))],
        out_specs=pl.BlockSpec((tt, CONV_WIDTH), lambda b, t: (b * n_t + t, 0)),
        out_shape=jax.ShapeDtypeStruct((out_rows, CONV_WIDTH), F32),
        scratch_shapes=[pltpu.VMEM((CONV_PAD + tt, CONV_WIDTH), F32)],
        compiler_params=_cparams(("parallel", "arbitrary")),
        name="conv",
    )(u_rows, buf, dw_w, dw_b, ln_g, ln_b)


def _mid1_kernel(x_ref, rw_ref, cv_ref, wo_ref, g_ref, wq_ref, x1_ref, q_ref):
    mix = _dot(rw_ref[...], wo_ref[:RW_WIDTH, :]) + _dot(cv_ref[...], wo_ref[RW_WIDTH:, :])
    x1 = x_ref[...] + mix
    x1_ref[...] = x1
    q = _dot(_rms(x1, g_ref[...]), wq_ref[...])
    q_ref[...] = (q * ATTN_SCALE).astype(BF16)


def _mid1(x, rw, cv, w_out_bf, g_x, w_q_bf):
    n = x.shape[0]
    row = lambda w: pl.BlockSpec((TOK_TILE, w), lambda i: (i, 0))
    full = lambda a, b: pl.BlockSpec((a, b), lambda i: (0, 0))
    return pl.pallas_call(
        _mid1_kernel,
        grid=(n // TOK_TILE,),
        in_specs=[row(D_MODEL), row(RW_WIDTH), row(CONV_WIDTH), full(D_MODEL, D_MODEL),
                  full(1, D_MODEL), full(D_MODEL, D_MODEL)],
        out_specs=[row(D_MODEL), row(D_MODEL)],
        out_shape=[jax.ShapeDtypeStruct((n, D_MODEL), F32), jax.ShapeDtypeStruct((n, D_MODEL), BF16)],
        compiler_params=_cparams(("parallel",)),
        name="mid1",
    )(x, rw, cv, w_out_bf, g_x, w_q_bf)


def _memkv_kernel(m_ref, g_ref, wk_ref, wv_ref, k_ref, v_ref):
    m = _rms(m_ref[...], g_ref[...])
    k_ref[...] = _dot(m, wk_ref[...])
    v_ref[...] = _dot(m, wv_ref[...])


def _memkv(mem, g_mem, w_mk_bf, w_mv_bf):
    n = mem.shape[0]
    tile = N_MEM
    row = pl.BlockSpec((tile, D_MODEL), lambda i: (i, 0))
    full = lambda a, b: pl.BlockSpec((a, b), lambda i: (0, 0))
    return pl.pallas_call(
        _memkv_kernel,
        grid=(n // tile,),
        in_specs=[row, full(1, D_MODEL), full(D_MODEL, D_MODEL), full(D_MODEL, D_MODEL)],
        out_specs=[row, row],
        out_shape=[jax.ShapeDtypeStruct((n, D_MODEL), F32), jax.ShapeDtypeStruct((n, D_MODEL), F32)],
        compiler_params=_cparams(("parallel",)),
        name="memkv",
    )(mem, g_mem, w_mk_bf, w_mv_bf)


def _attn_kernel(q_ref, k_ref, v_ref, o_ref):
    q = q_ref[0]
    for h in range(X_HEADS):
        hs = slice(h * X_HEAD_DIM, (h + 1) * X_HEAD_DIM)
        s = _dot_nt(q[:, hs], k_ref[0, :, hs])
        e = jnp.exp(s - jnp.max(s, axis=-1, keepdims=True))
        pr = e / jnp.sum(e, axis=-1, keepdims=True)
        o_ref[0, :, hs] = _dot(pr, v_ref[0, :, hs]).astype(BF16)


def _attn(q3, mk, mv, rows):
    b, t, _ = q3.shape
    qspec = pl.BlockSpec((1, rows, D_MODEL), lambda i, j: (i, j, 0))
    kvspec = pl.BlockSpec((1, N_MEM, D_MODEL), lambda i, j: (i, 0, 0))
    return pl.pallas_call(
        _attn_kernel,
        grid=(b, t // rows),
        in_specs=[qspec, kvspec, kvspec],
        out_specs=qspec,
        out_shape=jax.ShapeDtypeStruct((b, t, D_MODEL), BF16),
        compiler_params=_cparams(("parallel", "parallel")),
        name="attn",
    )(q3, mk, mv)


def _mid2_kernel(x1_ref, o_ref, wo_ref, g_ref, wr_ref, br_ref, tri_ref,
                 x2_ref, h3_ref, ids_ref, prob_ref, rank_ref, cnt_ref, carry_ref):
    i = pl.program_id(0)

    @pl.when(i == 0)
    def _():
        carry_ref[...] = jnp.zeros_like(carry_ref)

    x2 = x1_ref[...] + _dot(o_ref[...], wo_ref[...])
    x2_ref[...] = x2
    h3 = _rms(x2, g_ref[...])
    h3_ref[...] = h3
    logits = jnp.dot(h3, wr_ref[...], preferred_element_type=F32,
                     precision=lax.Precision.HIGHEST) + br_ref[...]
    n = logits.shape[0]
    lane = lax.broadcasted_iota(I32, (n, N_EXPERTS), 1)
    work = logits
    vals, ids = [], []
    for _ in range(TOP_K):
        m = jnp.max(work, axis=-1, keepdims=True)
        idx = jnp.min(jnp.where(work == m, lane, N_EXPERTS), axis=-1, keepdims=True)
        vals.append(m)
        ids.append(idx)
        work = jnp.where(lane == idx, -jnp.inf, work)
    exps = [jnp.exp(vk - vals[0]) for vk in vals]
    den = exps[0] + exps[1] + exps[2] + exps[3]
    mask = jnp.zeros((n, N_EXPERTS), F32)
    for idx in ids:
        mask = mask + jnp.where(lane == idx, 1.0, 0.0)
    rank = jnp.dot(tri_ref[...], mask.astype(BF16), preferred_element_type=F32) + carry_ref[0:1, :]
    carry_ref[0:1, :] = carry_ref[0:1, :] + jnp.sum(mask, axis=0, keepdims=True)
    cnt_ref[...] = carry_ref[0:1, :]
    for kq in range(TOP_K):
        ids_ref[:, kq:kq + 1] = ids[kq]
        prob_ref[:, kq:kq + 1] = exps[kq] / den
        rank_ref[:, kq:kq + 1] = jnp.sum(jnp.where(lane == ids[kq], rank, 0.0), axis=-1, keepdims=True)


def _mid2(x1, o, w_o_bf, g_moe, w_router, b_router, tri):
    n = x1.shape[0]
    row = lambda w: pl.BlockSpec((TOK_TILE, w), lambda i: (i, 0))
    full = lambda a, b: pl.BlockSpec((a, b), lambda i: (0, 0))
    return pl.pallas_call(
        _mid2_kernel,
        grid=(n // TOK_TILE,),
        in_specs=[row(D_MODEL), row(D_MODEL), full(D_MODEL, D_MODEL), full(1, D_MODEL),
                  full(D_MODEL, N_EXPERTS), full(1, N_EXPERTS), full(TOK_TILE, TOK_TILE)],
        out_specs=[row(D_MODEL), row(D_MODEL), row(TOP_K), row(TOP_K), row(TOP_K), full(1, N_EXPERTS)],
        out_shape=[jax.ShapeDtypeStruct((n, D_MODEL), F32), jax.ShapeDtypeStruct((n, D_MODEL), F32),
                   jax.ShapeDtypeStruct((n, TOP_K), I32), jax.ShapeDtypeStruct((n, TOP_K), F32),
                   jax.ShapeDtypeStruct((n, TOP_K), F32), jax.ShapeDtypeStruct((1, N_EXPERTS), F32)],
        scratch_shapes=[pltpu.VMEM((8, N_EXPERTS), F32)],
        compiler_params=_cparams(("arbitrary",)),
        name="mid2",
    )(x1, o, w_o_bf, g_moe, w_router, b_router, tri)


def _ffn_kernel(te_ref, nt_ref, src_cur_ref, src_nxt_ref, h3_hbm, wg_ref, bg_ref, wu_ref, bu_ref,
                wd_ref, bd_ref, y_ref, xbuf, sem):
    i = pl.program_id(0)
    n_used = nt_ref[0]
    slot = lax.rem(i, 2)

    def gather(src_ref, dst_slot):
        for r in range(FFN_TILE):
            pltpu.make_async_copy(h3_hbm.at[pl.ds(src_ref[0, 0, r], 1), :],
                                  xbuf.at[dst_slot, pl.ds(r, 1), :], sem.at[dst_slot]).start()

    @pl.when(i == 0)
    def _():
        gather(src_cur_ref, 0)

    @pl.when(i + 1 < n_used)
    def _():
        gather(src_nxt_ref, 1 - slot)

    @pl.when(i < n_used)
    def _():
        pltpu.make_async_copy(h3_hbm.at[pl.ds(0, FFN_TILE), :], xbuf.at[slot], sem.at[slot]).wait()
        x = xbuf[slot].astype(BF16)
        gate = jnp.minimum(jnp.dot(x, wg_ref[0], preferred_element_type=F32) + bg_ref[0], SWIGLU_LIMIT)
        up = jnp.clip(jnp.dot(x, wu_ref[0], preferred_element_type=F32) + bu_ref[0],
                      -SWIGLU_LIMIT, SWIGLU_LIMIT)
        hid = (up + 1.0) * gate * _sigmoid(gate * SWIGLU_ALPHA)
        y_ref[...] = jnp.dot(hid.astype(BF16), wd_ref[0], preferred_element_type=F32) + bd_ref[0]

    @pl.when(i >= n_used)
    def _():
        y_ref[...] = jnp.zeros_like(y_ref)


def _ffn(tile_expert, n_used, src3, h3, wg, bg, wu, bu, wd, bd):
    n_tiles = src3.shape[0]
    last = n_tiles - 1
    wspec = pl.BlockSpec((1, D_MODEL, D_MODEL), lambda i, te, nt: (te[i], 0, 0))
    bspec = pl.BlockSpec((1, 1, D_MODEL), lambda i, te, nt: (te[i], 0, 0))
    grid_spec = pltpu.PrefetchScalarGridSpec(
        num_scalar_prefetch=2,
        grid=(n_tiles,),
        in_specs=[pl.BlockSpec((1, 1, FFN_TILE), lambda i, te, nt: (i, 0, 0), memory_space=pltpu.SMEM),
                  pl.BlockSpec((1, 1, FFN_TILE), lambda i, te, nt: (jnp.minimum(i + 1, last), 0, 0),
                               memory_space=pltpu.SMEM),
                  pl.BlockSpec(memory_space=pl.ANY),
                  wspec, bspec, wspec, bspec, wspec, bspec],
        out_specs=pl.BlockSpec((FFN_TILE, D_MODEL), lambda i, te, nt: (i, 0)),
        scratch_shapes=[pltpu.VMEM((2, FFN_TILE, D_MODEL), F32), pltpu.SemaphoreType.DMA((2,))],
    )
    return pl.pallas_call(
        _ffn_kernel,
        grid_spec=grid_spec,
        out_shape=jax.ShapeDtypeStruct((n_tiles * FFN_TILE, D_MODEL), F32),
        compiler_params=_cparams(("arbitrary",)),
        name="ffn",
    )(tile_expert, n_used, src3, src3, h3, wg, bg, wu, bu, wd, bd)


def _combine_kernel(pos_cur_ref, pos_nxt_ref, ys_hbm, x2_ref, prob_ref, g_ref, o_ref, ybuf, sem, *, n_t):
    i = pl.program_id(0)
    slot = lax.rem(i, 2)

    def gather(pos_ref, dst_slot):
        for kq in range(TOP_K):
            for r in range(COMB_TILE):
                pltpu.make_async_copy(ys_hbm.at[pl.ds(pos_ref[0, 0, kq * COMB_TILE + r], 1), :],
                                      ybuf.at[dst_slot, kq, pl.ds(r, 1), :], sem.at[dst_slot]).start()

    @pl.when(i == 0)
    def _():
        gather(pos_cur_ref, 0)

    @pl.when(i + 1 < n_t)
    def _():
        gather(pos_nxt_ref, 1 - slot)

    for kq in range(TOP_K):
        pltpu.make_async_copy(ys_hbm.at[pl.ds(0, COMB_TILE), :], ybuf.at[slot, kq], sem.at[slot]).wait()
    acc = x2_ref[...]
    for kq in range(TOP_K):
        acc = acc + prob_ref[:, kq:kq + 1] * ybuf[slot, kq]
    o_ref[...] = _rms(acc, g_ref[...])


def _combine(pos3, ys, x2, prob, g_final, tile0, n_t):
    last = n_t - 1
    kern = functools.partial(_combine_kernel, n_t=n_t)
    return pl.pallas_call(
        kern,
        grid=(n_t,),
        in_specs=[pl.BlockSpec((1, 1, TOP_K * COMB_TILE), lambda i: (i, 0, 0), memory_space=pltpu.SMEM),
                  pl.BlockSpec((1, 1, TOP_K * COMB_TILE), lambda i: (jnp.minimum(i + 1, last), 0, 0),
                               memory_space=pltpu.SMEM),
                  pl.BlockSpec(memory_space=pl.ANY),
                  pl.BlockSpec((COMB_TILE, D_MODEL), lambda i: (tile0 + i, 0)),
                  pl.BlockSpec((COMB_TILE, TOP_K), lambda i: (tile0 + i, 0)),
                  pl.BlockSpec((1, D_MODEL), lambda i: (0, 0))],
        out_specs=pl.BlockSpec((COMB_TILE, D_MODEL), lambda i: (i, 0)),
        out_shape=jax.ShapeDtypeStruct((n_t * COMB_TILE, D_MODEL), F32),
        scratch_shapes=[pltpu.VMEM((2, TOP_K, COMB_TILE, D_MODEL), F32), pltpu.SemaphoreType.DMA((2,))],
        compiler_params=_cparams(("arbitrary",)),
        name="combine",
    )(pos3, pos3, ys, x2, prob, g_final)


def kernel(x_prompt, x_sample, mem_prompt, state_shift, state_wkv, state_conv, cache_mem_k, cache_mem_v,
           g_mix, w_in, mu_shift, w0, w_decay_up, a0, w_iclr_up, w_glora_up, k_k, k_a, r_k, gn_g, gn_b,
           dw_w, dw_b, cln_g, cln_b, w_out, g_xattn, g_mem, w_q, w_mk, w_mv, w_o,
           g_moe, w_router, b_router, w_moe_gate, b_moe_gate, w_moe_up, b_moe_up, w_moe_down, b_moe_down,
           g_final):
    bp, tp, _ = x_prompt.shape
    bs, ts, _ = x_sample.shape
    n_p, n_s = bp * tp, bs * ts
    n_all = n_p + n_s
    assert tp % WKV_BLOCK == 0 and n_p % TOK_TILE == 0 and n_s % TOK_TILE == 0 and ts <= SAMPLE_PAD
    assert n_p % COMB_TILE == 0 and n_s % COMB_TILE == 0
    row2 = lambda a: a.reshape(1, -1)

    x_all = jnp.concatenate([x_prompt.reshape(n_p, D_MODEL), x_sample.reshape(n_s, D_MODEL)], axis=0)
    p_rw, u = _inproj(x_all, row2(g_mix[0]), w_in[0].astype(BF16))

    head_idx = jnp.arange(RW_WIDTH) // RW_HEAD_DIM
    eseg = (head_idx[:, None] == head_idx[None, :]).astype(BF16)
    wts = (row2(mu_shift[0]), row2(w0[0]), w_decay_up[0].astype(BF16), row2(a0[0]),
           w_iclr_up[0].astype(BF16), w_glora_up[0].astype(BF16), row2(k_k[0]), row2(k_a[0]),
           row2(r_k[0]), row2(gn_g[0]), row2(gn_b[0]), eseg)
    n_tp = tp // WKV_BLOCK
    rw_p, wkv_p = _wkv(p_rw, 0, jnp.zeros((bp, 1, RW_PROJ), F32),
                       jnp.zeros((bp, RW_HEADS, RW_HEAD_DIM, RW_HEAD_DIM), F32), wts,
                       batch=bp, tt=WKV_BLOCK, chunk=WKV_CHUNK, t_valid=tp, n_t=n_tp, out_rows=n_p)
    p_rw_s = p_rw[n_p:].reshape(bs, ts, RW_PROJ)
    p_rw_s_pad = jnp.pad(p_rw_s, ((0, 0), (0, SAMPLE_PAD - ts), (0, 0))).reshape(bs * SAMPLE_PAD, RW_PROJ)
    rw_s_pad, wkv_s = _wkv(p_rw_s_pad, 0, state_shift[0].reshape(bs, 1, RW_PROJ), state_wkv[0], wts,
                           batch=bs, tt=SAMPLE_PAD, chunk=SAMPLE_PAD, t_valid=ts, n_t=1,
                           out_rows=bs * SAMPLE_PAD)
    rw_s = rw_s_pad.reshape(bs, SAMPLE_PAD, RW_WIDTH)[:, :ts].reshape(n_s, RW_WIDTH)

    conv_w = (dw_w[0], row2(dw_b[0]), row2(cln_g[0]), row2(cln_b[0]))
    cv_p = _conv(u, 0, jnp.zeros((bp, CONV_PAD, CONV_WIDTH), F32), *conv_w,
                 batch=bp, tt=WKV_BLOCK, n_t=n_tp, out_rows=n_p)
    u_s = u[n_p:]
    u_s_pad = jnp.pad(u_s.reshape(bs, ts, CONV_WIDTH), ((0, 0), (0, SAMPLE_PAD - ts), (0, 0)))
    buf_s = jnp.pad(state_conv[0], ((0, 0), (CONV_PAD - (CONV_K - 1), 0), (0, 0)))
    cv_s_pad = _conv(u_s_pad.reshape(bs * SAMPLE_PAD, CONV_WIDTH), 0, buf_s, *conv_w,
                     batch=bs, tt=SAMPLE_PAD, n_t=1, out_rows=bs * SAMPLE_PAD)
    cv_s = cv_s_pad.reshape(bs, SAMPLE_PAD, CONV_WIDTH)[:, :ts].reshape(n_s, CONV_WIDTH)

    rw = jnp.concatenate([rw_p, rw_s], axis=0)
    cv = jnp.concatenate([cv_p, cv_s], axis=0)
    x1, q = _mid1(x_all, rw, cv, w_out[0].astype(BF16), row2(g_xattn[0]), w_q[0].astype(BF16))

    mk_p, mv_p = _memkv(mem_prompt.reshape(bp * N_MEM, D_MODEL), row2(g_mem[0]),
                        w_mk[0].astype(BF16), w_mv[0].astype(BF16))
    mk_p = mk_p.reshape(bp, N_MEM, D_MODEL)
    mv_p = mv_p.reshape(bp, N_MEM, D_MODEL)
    o_p = _attn(q[:n_p].reshape(bp, tp, D_MODEL), mk_p, mv_p, TOK_TILE)
    o_s = _attn(q[n_p:].reshape(bs, ts, D_MODEL), cache_mem_k[0].reshape(bs, N_MEM, D_MODEL),
                cache_mem_v[0].reshape(bs, N_MEM, D_MODEL), ts)
    o_all = jnp.concatenate([o_p.reshape(n_p, D_MODEL), o_s.reshape(n_s, D_MODEL)], axis=0)

    tri = (jnp.arange(TOK_TILE)[None, :] < jnp.arange(TOK_TILE)[:, None]).astype(BF16)
    x2, h3, ids, prob, rank, cnt = _mid2(x1, o_all, w_o[0].astype(BF16), row2(g_moe[0]),
                                         w_router[0], row2(b_router[0]), tri)

    counts = cnt[0].astype(I32)
    padded = ((counts + FFN_TILE - 1) // FFN_TILE) * FFN_TILE
    ends = jnp.cumsum(padded)
    offs = ends - padded
    pos = offs[ids] + rank.astype(I32)
    n_tiles = (n_all * TOP_K) // FFN_TILE + N_EXPERTS
    n_used = ends[-1] // FFN_TILE
    tile_ids = jnp.minimum(jnp.arange(n_tiles, dtype=I32), n_used - 1)
    tile_expert = jnp.minimum(jnp.searchsorted(ends // FFN_TILE, tile_ids, side="right"),
                              N_EXPERTS - 1).astype(I32)
    tok = jnp.broadcast_to(jnp.arange(n_all, dtype=I32)[:, None], (n_all, TOP_K))
    src = jnp.zeros((n_tiles * FFN_TILE,), I32).at[pos.reshape(-1)].set(tok.reshape(-1))

    ys = _ffn(tile_expert, n_used.reshape(1).astype(I32), src.reshape(n_tiles, 1, FFN_TILE), h3,
              w_moe_gate[0].astype(BF16), b_moe_gate[0].reshape(N_EXPERTS, 1, D_MODEL),
              w_moe_up[0].astype(BF16), b_moe_up[0].reshape(N_EXPERTS, 1, D_MODEL),
              w_moe_down[0].astype(BF16), b_moe_down[0].reshape(N_EXPERTS, 1, D_MODEL))

    def pos_tiles(p2):
        nt = p2.shape[0] // COMB_TILE
        return p2.reshape(nt, COMB_TILE, TOP_K).transpose(0, 2, 1).reshape(nt, 1, TOP_K * COMB_TILE)

    gfin = row2(g_final)
    y_p = _combine(pos_tiles(pos[:n_p]), ys, x2, prob, gfin, 0, n_p // COMB_TILE)
    y_s = _combine(pos_tiles(pos[n_p:]), ys, x2, prob, gfin, n_p // COMB_TILE, n_s // COMB_TILE)

    p_rw_p3 = p_rw[:n_p].reshape(bp, tp, RW_PROJ)
    u_p3 = u[:n_p].reshape(bp, tp, CONV_WIDTH)
    new_conv_s = jnp.concatenate([state_conv[0], u_s.reshape(bs, ts, CONV_WIDTH)], axis=1)[:, -(CONV_K - 1):]
    kv_shape = (1, bp, N_MEM, X_HEADS, X_HEAD_DIM)
    return (y_p.reshape(bp, tp, D_MODEL), y_s.reshape(bs, ts, D_MODEL),
            p_rw_p3[:, -1][None], wkv_p[None], u_p3[:, -(CONV_K - 1):][None],
            mk_p.reshape(kv_shape), mv_p.reshape(kv_shape),
            p_rw_s[:, -1][None], wkv_s[None], new_conv_s[None])
```

```python
import functools

import jax
import jax.numpy as jnp
from jax import lax
from jax.experimental import pallas as pl
from jax.experimental.pallas import tpu as pltpu

F32 = jnp.float32
BF16 = jnp.bfloat16
I32 = jnp.int32

D_MODEL = 1024
RW_HEADS = 8
RW_HEAD_DIM = 64
RW_WIDTH = RW_HEADS * RW_HEAD_DIM
DECAY_LORA = 64
ICLR_LORA = 64
GATE_LORA = 128
RW_PROJ = 3 * RW_WIDTH + DECAY_LORA + ICLR_LORA + GATE_LORA
GN_EPS = 64e-5
CONV_WIDTH = D_MODEL - RW_WIDTH
CONV_K = 31
LN_EPS = 1e-5
IN_PROJ = RW_PROJ + 2 * CONV_WIDTH
N_MEM = 256
X_HEADS = 4
X_HEAD_DIM = D_MODEL // X_HEADS
ATTN_SCALE = X_HEAD_DIM ** -0.5
N_EXPERTS = 32
TOP_K = 4
SWIGLU_LIMIT = 7.0
SWIGLU_ALPHA = 1.702
NORM_EPS = 1e-5

TOK_TILE = 512
WKV_BLOCK = 256
WKV_CHUNK = 64
SAMPLE_PAD = 8
SAMPLE_SEQS = 4
GROUP = 4
GW = GROUP * RW_HEAD_DIM
N_GROUPS = RW_HEADS // GROUP
FFN_TILE = 512
COMB_TILE = 128
CONV_PAD = 32
VMEM_LIMIT = 56 * 1024 * 1024


def _cparams(sem):
    return pltpu.CompilerParams(dimension_semantics=sem, vmem_limit_bytes=VMEM_LIMIT)


def _dot(a, b):
    return jnp.dot(a.astype(BF16), b.astype(BF16), preferred_element_type=F32)


def _dot_nt(a, b):
    return lax.dot_general(a.astype(BF16), b.astype(BF16), (((1,), (1,)), ((), ())),
                           preferred_element_type=F32)


def _dot_tn(a, b):
    return lax.dot_general(a.astype(BF16), b.astype(BF16), (((0,), (0,)), ((), ())),
                           preferred_element_type=F32)


def _split2(x):
    hi = x.astype(BF16)
    lo = (x - hi.astype(F32)).astype(BF16)
    return hi, lo


def _split3(x):
    hi = x.astype(BF16)
    r1 = x - hi.astype(F32)
    mid = r1.astype(BF16)
    lo = (r1 - mid.astype(F32)).astype(BF16)
    return hi, mid, lo


def _segsum(x, eseg):
    hi, lo = _split2(x)
    return (jnp.dot(hi, eseg, preferred_element_type=F32)
            + jnp.dot(lo, eseg, preferred_element_type=F32))


def _rms(x, g):
    return x * lax.rsqrt(jnp.mean(x * x, axis=-1, keepdims=True) + NORM_EPS) * g


def _sigmoid(x):
    return 1.0 / (1.0 + jnp.exp(-x))


def _pick(first, a_ref, b_ref):
    return jnp.where(first, a_ref[...], b_ref[...])


def _two_source_specs(width, n_pt):
    return (pl.BlockSpec((TOK_TILE, width), lambda i: (jnp.minimum(i, n_pt - 1), 0)),
            pl.BlockSpec((TOK_TILE, width), lambda i: (jnp.maximum(i - n_pt, 0), 0)))


def _inproj_kernel(xp_ref, xs_ref, g_ref, w_ref, prw_ref, u_ref, *, n_pt):
    h = _rms(_pick(pl.program_id(0) < n_pt, xp_ref, xs_ref), g_ref[...])
    p = _dot(h, w_ref[...])
    prw_ref[...] = p[:, :RW_PROJ]
    u_ref[...] = p[:, RW_PROJ:RW_PROJ + CONV_WIDTH] * _sigmoid(p[:, RW_PROJ + CONV_WIDTH:])


def _inproj(x_p, x_s, g_mix, w_in_bf):
    n = x_p.shape[0] + x_s.shape[0]
    n_pt = x_p.shape[0] // TOK_TILE
    return pl.pallas_call(
        functools.partial(_inproj_kernel, n_pt=n_pt),
        grid=(n // TOK_TILE,),
        in_specs=[*_two_source_specs(D_MODEL, n_pt),
                  pl.BlockSpec((1, D_MODEL), lambda i: (0, 0)),
                  pl.BlockSpec((D_MODEL, IN_PROJ), lambda i: (0, 0))],
        out_specs=[pl.BlockSpec((TOK_TILE, RW_PROJ), lambda i: (i, 0)),
                   pl.BlockSpec((TOK_TILE, CONV_WIDTH), lambda i: (i, 0))],
        out_shape=[jax.ShapeDtypeStruct((n, RW_PROJ), F32),
                   jax.ShapeDtypeStruct((n, CONV_WIDTH), F32)],
        compiler_params=_cparams(("parallel",)),
        name="inproj",
    )(x_p, x_s, g_mix, w_in_bf)


def _bd(x, w):
    r, lanes = x.shape
    head = lax.broadcasted_iota(I32, (r, lanes), 1) >> (w.bit_length() - 1)
    return jnp.concatenate([jnp.where(head == h, x, 0.0) for h in range(GROUP)], axis=0).astype(BF16)


def _collapse(x):
    head = lax.broadcasted_iota(I32, (RW_HEAD_DIM, GW), 1) >> (RW_HEAD_DIM.bit_length() - 1)
    out = jnp.zeros((RW_HEAD_DIM, GW), F32)
    for h in range(GROUP):
        out = out + jnp.where(head == h, x[h * RW_HEAD_DIM:(h + 1) * RW_HEAD_DIM, :], 0.0)
    return out


def _wkv_kernel(p_ref, prev_ref, s0_ref, mu_ref, w0_ref, wdup_ref, a0_ref, aup_ref, gup_ref,
                kk_ref, ka_ref, rk_ref, gng_ref, gnb_ref, eseg_ref,
                out_ref, sfin_ref,
                s_ref, last_ref, rt_ref, at_ref, bt_ref, kt_ref, v_ref, pc_ref, y_ref,
                r_ref, km_ref, g_ref, *, n_seq, tt_seq, chunk, t_valid, n_t):
    t = pl.program_id(1)
    tt = n_seq * tt_seq
    n_chunk = tt_seq // chunk
    n_fused = chunk.bit_length() - 2
    nd = RW_HEAD_DIM
    c1, c2, c3 = RW_WIDTH, 2 * RW_WIDTH, 3 * RW_WIDTH
    c4 = c3 + DECAY_LORA
    c5 = c4 + ICLR_LORA

    @pl.when(t == 0)
    def _():
        s_ref[...] = s0_ref[...]
        if n_seq == 1:
            last_ref[0:1, :] = prev_ref[0]

    p = p_ref[...]
    row = lax.broadcasted_iota(I32, (tt, 1), 0)
    seq_row = row & (tt_seq - 1)
    rolled = pltpu.roll(p, shift=1, axis=0)
    if n_seq == 1:
        ps = jnp.where(seq_row == 0, last_ref[0:1, :], rolled)
        last_ref[0:1, :] = p[tt - 1:tt, :]
    else:
        prev_rows = jnp.concatenate(
            [jnp.broadcast_to(prev_ref[q], (tt_seq, RW_PROJ)) for q in range(n_seq)], axis=0)
        ps = jnp.where(seq_row == 0, prev_rows, rolled)
    z = p + mu_ref[...] * (ps - p)
    r, k, v = z[:, :c1], z[:, c1:c2], z[:, c2:c3]
    wd, ad, gd = z[:, c3:c4], z[:, c4:c5], z[:, c5:]
    wpre = w0_ref[...] + _dot(jnp.tanh(wd), wdup_ref[...])
    neg = -wpre
    w = -(jnp.maximum(neg, 0.0) + jnp.log(1.0 + jnp.exp(-jnp.abs(neg)))) - 0.5
    a = _sigmoid(a0_ref[...] + _dot(ad, aup_ref[...]))
    g_ref[...] = _dot(_sigmoid(gd), gup_ref[...])
    eseg = eseg_ref[...]
    kk = k * kk_ref[...]
    kk = kk / jnp.maximum(jnp.sqrt(_segsum(kk * kk, eseg)), 1e-12)
    kmod = k * (1.0 + (a - 1.0) * ka_ref[...])
    logd = -jnp.exp(w)
    if t_valid < tt_seq * n_t:
        valid = (t * tt_seq + seq_row) < t_valid
        logd = jnp.where(valid, logd, 0.0)
        kk = jnp.where(valid, kk, 0.0)
        kmod = jnp.where(valid, kmod, 0.0)
        v = jnp.where(valid, v, 0.0)
    ri = lax.broadcasted_iota(I32, (tt, tt), 0)
    ci = lax.broadcasted_iota(I32, (tt, tt), 1)
    shift = chunk.bit_length() - 1
    ltri = jnp.where(ci <= ri, jnp.where((ri >> shift) == (ci >> shift), 1.0, 0.0), 0.0).astype(BF16)
    hi, mid, lo = _split3(logd)
    logp = (jnp.dot(ltri, hi, preferred_element_type=F32)
            + jnp.dot(ltri, mid, preferred_element_type=F32)
            + jnp.dot(ltri, lo, preferred_element_type=F32))
    pcum = jnp.exp(logp)
    pinv = jnp.exp(-logp)
    r_ref[...] = r
    km_ref[...] = kmod
    v_ref[...] = v
    pc_ref[...] = pcum
    rt_ref[...] = r * pcum
    at_ref[...] = -kk * jnp.exp(logp - logd)
    bt_ref[...] = kk * a * pinv
    kt_ref[...] = kmod * pinv

    cw = GROUP * chunk
    ti = lax.broadcasted_iota(I32, (chunk, cw), 0)
    si = lax.broadcasted_iota(I32, (chunk, cw), 1) & (chunk - 1)
    strict = si < ti
    incl = si <= ti
    eye_c = jnp.where(si == ti, 1.0, 0.0)
    ji = lax.broadcasted_iota(I32, (nd, GW), 0)
    jj = lax.broadcasted_iota(I32, (nd, GW), 1) & (nd - 1)
    eye_n = jnp.where(ji == jj, 1.0, 0.0)

    chains = [(s, c, g) for s in range(n_seq) for c in range(n_chunk) for g in range(N_GROUPS)]

    def blk(ref, s, c, g):
        r0 = s * tt_seq + c * chunk
        return ref[r0:r0 + chunk, g * GW:(g + 1) * GW]

    am = {q: blk(at_ref, *q) for q in chains}
    rm = {q: blk(rt_ref, *q) for q in chains}
    bm = {q: blk(bt_ref, *q) for q in chains}
    km = {q: blk(kt_ref, *q) for q in chains}
    vm = {q: blk(v_ref, *q) for q in chains}
    a_ab, a_ak, a_rb, a_rk = {}, {}, {}, {}
    for q in chains:
        ar = jnp.concatenate([am[q], rm[q]], axis=0)
        s1 = _dot_nt(ar, _bd(bm[q], nd))
        s2 = _dot_nt(ar, _bd(km[q], nd))
        a_ab[q] = jnp.where(strict, s1[:chunk], 0.0)
        a_rb[q] = jnp.where(incl, s1[chunk:], 0.0)
        a_ak[q] = jnp.where(strict, s2[:chunk], 0.0)
        a_rk[q] = jnp.where(incl, s2[chunk:], 0.0)
    tinv = {q: eye_c + a_ab[q] for q in chains}
    pw = {q: _dot(a_ab[q], _bd(a_ab[q], chunk)) for q in chains}
    for lvl in range(n_fused):
        last = lvl == n_fused - 1
        for q in chains:
            lhs = tinv[q] if last else jnp.concatenate([tinv[q], pw[q]], axis=0)
            zz = _dot(lhs, _bd(pw[q], chunk))
            tinv[q] = tinv[q] + zz[:chunk]
            if not last:
                pw[q] = zz[chunk:]
    akv, rkv = {}, {}
    for q in chains:
        zv = _dot(jnp.concatenate([a_ak[q], a_rk[q]], axis=0), _bd(vm[q], nd))
        akv[q], rkv[q] = zv[:chunk], zv[chunk:]
    ap = {q: _dot(tinv[q], _bd(am[q], nd)) for q in chains}
    uv = {q: _dot(tinv[q], _bd(akv[q], nd)) for q in chains}
    rp = {q: rm[q] + _dot(a_rb[q], _bd(ap[q], nd)) for q in chains}
    yv = {q: _dot(a_rb[q], _bd(uv[q], nd)) + rkv[q] for q in chains}
    mm, gm = {}, {}
    for q in chains:
        s, c, g = q
        r_end = s * tt_seq + (c + 1) * chunk
        pc = pc_ref[r_end - 8:r_end, g * GW:(g + 1) * GW][7:8, :]
        cross = _dot_tn(jnp.concatenate([ap[q], uv[q]], axis=1), bm[q])
        vk = _dot_tn(vm[q], km[q])
        mm[q] = (eye_n + _collapse(cross[:GW])) * pc
        gm[q] = (_collapse(cross[GW:]) + _collapse(vk)) * pc

    for s in range(n_seq):
        for g in range(N_GROUPS):
            st = s_ref[s, g]
            for c in range(n_chunk):
                q = (s, c, g)
                r0 = s * tt_seq + c * chunk
                y_ref[r0:r0 + chunk, g * GW:(g + 1) * GW] = _dot_nt(rp[q], _bd(st, nd)) + yv[q]
                st = _dot(st, _bd(mm[q], nd)) + gm[q]
            s_ref[s, g] = st

    y = y_ref[...]
    inv_n = 1.0 / RW_HEAD_DIM
    mean = _segsum(y, eseg) * inv_n
    yc = y - mean
    var = _segsum(yc * yc, eseg) * inv_n
    yn = yc * lax.rsqrt(var + GN_EPS) * gng_ref[...] + gnb_ref[...]
    bonus = _segsum(r_ref[...] * km_ref[...] * rk_ref[...], eseg) * v_ref[...]
    out_ref[...] = (yn + bonus) * g_ref[...]

    @pl.when(t == n_t - 1)
    def _():
        sfin_ref[...] = s_ref[...]


def _wkv(p_rows, row_block0, prev, s0, wts, *, batch, n_seq, tt_seq, chunk, t_valid, n_t, out_rows):
    (mu, w0, wdup, a0, aup, gup, k_k, k_a, r_k, gn_g, gn_b, eseg) = wts
    tt = n_seq * tt_seq
    kern = functools.partial(_wkv_kernel, n_seq=n_seq, tt_seq=tt_seq, chunk=chunk, t_valid=t_valid, n_t=n_t)
    const = lambda shape: pl.BlockSpec(shape, lambda b, t: tuple(0 for _ in shape))
    blk = pltpu.VMEM((tt, RW_WIDTH), F32)
    st_spec = pl.BlockSpec((n_seq, N_GROUPS, RW_HEAD_DIM, GW), lambda b, t: (b, 0, 0, 0))
    return pl.pallas_call(
        kern,
        grid=(batch // n_seq, n_t),
        in_specs=[pl.BlockSpec((tt, RW_PROJ), lambda b, t: (row_block0 + b * n_t + t, 0)),
                  pl.BlockSpec((n_seq, 1, RW_PROJ), lambda b, t: (b, 0, 0)),
                  st_spec,
                  const((1, RW_PROJ)), const((1, RW_WIDTH)), const((DECAY_LORA, RW_WIDTH)),
                  const((1, RW_WIDTH)), const((ICLR_LORA, RW_WIDTH)), const((GATE_LORA, RW_WIDTH)),
                  const((1, RW_WIDTH)), const((1, RW_WIDTH)), const((1, RW_WIDTH)),
                  const((1, RW_WIDTH)), const((1, RW_WIDTH)), const((RW_WIDTH, RW_WIDTH))],
        out_specs=[pl.BlockSpec((tt, RW_WIDTH), lambda b, t: (b * n_t + t, 0)), st_spec],
        out_shape=[jax.ShapeDtypeStruct((out_rows, RW_WIDTH), F32),
                   jax.ShapeDtypeStruct((batch, N_GROUPS, RW_HEAD_DIM, GW), F32)],
        scratch_shapes=[pltpu.VMEM((n_seq, N_GROUPS, RW_HEAD_DIM, GW), F32),
                        pltpu.VMEM((8, RW_PROJ), F32),
                        blk, blk, blk, blk, blk, blk, blk, blk, blk, blk],
        compiler_params=_cparams(("parallel", "arbitrary")),
        name="wkv",
    )(p_rows, prev, s0, mu, w0, wdup, a0, aup, gup, k_k, k_a, r_k, gn_g, gn_b, eseg)


def _to_grouped(state):
    b = state.shape[0]
    return (state.reshape(b, N_GROUPS, GROUP, RW_HEAD_DIM, RW_HEAD_DIM).transpose(0, 1, 3, 2, 4)
            .reshape(b, N_GROUPS, RW_HEAD_DIM, GW))


def _from_grouped(state):
    b = state.shape[0]
    return (state.reshape(b, N_GROUPS, RW_HEAD_DIM, GROUP, RW_HEAD_DIM).transpose(0, 1, 3, 2, 4)
            .reshape(b, RW_HEADS, RW_HEAD_DIM, RW_HEAD_DIM))


def _conv_kernel(u_ref, buf_ref, w_ref, b_ref, g_ref, beta_ref, o_ref, ext_ref, *, tt, n_t):
    t = pl.program_id(1)

    @pl.when(t == 0)
    def _():
        ext_ref[0:CONV_PAD, :] = buf_ref[0]

    ext_ref[CONV_PAD:CONV_PAD + tt, :] = u_ref[...]
    acc = jnp.zeros((tt, CONV_WIDTH), F32) + b_ref[...]
    first = CONV_PAD - (CONV_K - 1)
    for j in range(CONV_K):
        acc = acc + ext_ref[pl.ds(first + j, tt), :] * w_ref[j:j + 1, :]
    mean = jnp.mean(acc, axis=-1, keepdims=True)
    cen = acc - mean
    var = jnp.mean(cen * cen, axis=-1, keepdims=True)
    cn = cen * lax.rsqrt(var + LN_EPS) * g_ref[...] + beta_ref[...]
    o_ref[...] = cn * _sigmoid(cn)
    if n_t > 1:
        ext_ref[0:CONV_PAD, :] = ext_ref[tt:tt + CONV_PAD, :]


def _conv(u_rows, row_block0, buf, dw_w, dw_b, ln_g, ln_b, *, batch, tt, n_t, out_rows):
    kern = functools.partial(_conv_kernel, tt=tt, n_t=n_t)
    const = lambda shape: pl.BlockSpec(shape, lambda b, t: tuple(0 for _ in shape))
    return pl.pallas_call(
        kern,
        grid=(batch, n_t),
        in_specs=[pl.BlockSpec((tt, CONV_WIDTH), lambda b, t: (row_block0 + b * n_t + t, 0)),
                  pl.BlockSpec((1, CONV_PAD, CONV_WIDTH), lambda b, t: (b, 0, 0)),
                  const((CONV_K, CONV_WIDTH)), const((1, CONV_WIDTH)),
                  const((1, CONV_WIDTH)), const((1, CONV_WIDTH))],
        out_specs=pl.BlockSpec((tt, CONV_WIDTH), lambda b, t: (b * n_t + t, 0)),
        out_shape=jax.ShapeDtypeStruct((out_rows, CONV_WIDTH), F32),
        scratch_shapes=[pltpu.VMEM((CONV_PAD + tt, CONV_WIDTH), F32)],
        compiler_params=_cparams(("parallel", "arbitrary")),
        name="conv",
    )(u_rows, buf, dw_w, dw_b, ln_g, ln_b)


def _mid1_kernel(xp_ref, xs_ref, rwp_ref, rws_ref, cvp_ref, cvs_ref, wo_ref, g_ref, wq_ref, x1_ref, q_ref,
                 *, n_pt):
    first = pl.program_id(0) < n_pt
    rw = _pick(first, rwp_ref, rws_ref)
    cv = _pick(first, cvp_ref, cvs_ref)
    mix = _dot(rw, wo_ref[:RW_WIDTH, :]) + _dot(cv, wo_ref[RW_WIDTH:, :])
    x1 = _pick(first, xp_ref, xs_ref) + mix
    x1_ref[...] = x1
    q = _dot(_rms(x1, g_ref[...]), wq_ref[...])
    q_ref[...] = (q * ATTN_SCALE).astype(BF16)


def _mid1(x_p, x_s, rw_p, rw_s, cv_p, cv_s, w_out_bf, g_x, w_q_bf):
    n = x_p.shape[0] + x_s.shape[0]
    n_pt = rw_p.shape[0] // TOK_TILE
    row = lambda w: pl.BlockSpec((TOK_TILE, w), lambda i: (i, 0))
    full = lambda a, b: pl.BlockSpec((a, b), lambda i: (0, 0))
    return pl.pallas_call(
        functools.partial(_mid1_kernel, n_pt=n_pt),
        grid=(n // TOK_TILE,),
        in_specs=[*_two_source_specs(D_MODEL, n_pt), *_two_source_specs(RW_WIDTH, n_pt),
                  *_two_source_specs(CONV_WIDTH, n_pt),
                  full(D_MODEL, D_MODEL), full(1, D_MODEL), full(D_MODEL, D_MODEL)],
        out_specs=[row(D_MODEL), row(D_MODEL)],
        out_shape=[jax.ShapeDtypeStruct((n, D_MODEL), F32), jax.ShapeDtypeStruct((n, D_MODEL), BF16)],
        compiler_params=_cparams(("parallel",)),
        name="mid1",
    )(x_p, x_s, rw_p, rw_s, cv_p, cv_s, w_out_bf, g_x, w_q_bf)


def _memkv_kernel(m_ref, g_ref, wk_ref, wv_ref, k_ref, v_ref):
    m = _rms(m_ref[...], g_ref[...])
    k_ref[...] = _dot(m, wk_ref[...])
    v_ref[...] = _dot(m, wv_ref[...])


def _memkv(mem, g_mem, w_mk_bf, w_mv_bf):
    n = mem.shape[0]
    tile = N_MEM
    row = pl.BlockSpec((tile, D_MODEL), lambda i: (i, 0))
    full = lambda a, b: pl.BlockSpec((a, b), lambda i: (0, 0))
    return pl.pallas_call(
        _memkv_kernel,
        grid=(n // tile,),
        in_specs=[row, full(1, D_MODEL), full(D_MODEL, D_MODEL), full(D_MODEL, D_MODEL)],
        out_specs=[row, row],
        out_shape=[jax.ShapeDtypeStruct((n, D_MODEL), F32), jax.ShapeDtypeStruct((n, D_MODEL), F32)],
        compiler_params=_cparams(("parallel",)),
        name="memkv",
    )(mem, g_mem, w_mk_bf, w_mv_bf)


def _attn_heads(q, k_ref, v_ref, store):
    for h in range(X_HEADS):
        hs = slice(h * X_HEAD_DIM, (h + 1) * X_HEAD_DIM)
        s = _dot_nt(q[:, hs], k_ref[0, :, hs])
        e = jnp.exp(s - jnp.max(s, axis=-1, keepdims=True))
        pr = e / jnp.sum(e, axis=-1, keepdims=True)
        store(hs, _dot(pr, v_ref[0, :, hs]).astype(BF16))


def _attn_rows_kernel(q_ref, k_ref, v_ref, o_ref):
    def store(hs, val):
        o_ref[:, hs] = val
    _attn_heads(q_ref[...], k_ref, v_ref, store)


def _attn_seq_kernel(q_ref, k_ref, v_ref, o_ref):
    def store(hs, val):
        o_ref[0, :, hs] = val
    _attn_heads(q_ref[0], k_ref, v_ref, store)


def _attn_rows(q, mk, mv, n_rows, tiles_per_seq):
    qspec = pl.BlockSpec((TOK_TILE, D_MODEL), lambda i, j: (i * tiles_per_seq + j, 0))
    kvspec = pl.BlockSpec((1, N_MEM, D_MODEL), lambda i, j: (i, 0, 0))
    return pl.pallas_call(
        _attn_rows_kernel,
        grid=(mk.shape[0], tiles_per_seq),
        in_specs=[qspec, kvspec, kvspec],
        out_specs=qspec,
        out_shape=jax.ShapeDtypeStruct((n_rows, D_MODEL), BF16),
        compiler_params=_cparams(("parallel", "parallel")),
        name="attn",
    )(q, mk, mv)


def _attn_seq(q3, mk, mv):
    b, t, _ = q3.shape
    qspec = pl.BlockSpec((1, t, D_MODEL), lambda i: (i, 0, 0))
    kvspec = pl.BlockSpec((1, N_MEM, D_MODEL), lambda i: (i, 0, 0))
    return pl.pallas_call(
        _attn_seq_kernel,
        grid=(b,),
        in_specs=[qspec, kvspec, kvspec],
        out_specs=qspec,
        out_shape=jax.ShapeDtypeStruct((b, t, D_MODEL), BF16),
        compiler_params=_cparams(("parallel",)),
        name="attn",
    )(q3, mk, mv)


def _mid2_kernel(x1_ref, op_ref, os_ref, wo_ref, g_ref, wr_ref, br_ref, tri_ref,
                 x2_ref, h3_ref, ids_ref, prob_ref, rank_ref, cnt_ref, carry_ref, *, n_pt):
    i = pl.program_id(0)

    @pl.when(i == 0)
    def _():
        carry_ref[...] = jnp.zeros_like(carry_ref)

    x2 = x1_ref[...] + _dot(_pick(i < n_pt, op_ref, os_ref), wo_ref[...])
    x2_ref[...] = x2
    h3 = _rms(x2, g_ref[...])
    h3_ref[...] = h3
    h3_hi, h3_lo = _split2(h3)
    wr_hi, wr_lo = _split2(wr_ref[...])
    logits = (jnp.dot(h3_hi, wr_hi, preferred_element_type=F32)
              + jnp.dot(h3_hi, wr_lo, preferred_element_type=F32)
              + jnp.dot(h3_lo, wr_hi, preferred_element_type=F32)) + br_ref[...]
    n = logits.shape[0]
    lane = lax.broadcasted_iota(I32, (n, N_EXPERTS), 1)
    work = logits
    vals, ids = [], []
    for _ in range(TOP_K):
        m = jnp.max(work, axis=-1, keepdims=True)
        idx = jnp.min(jnp.where(work == m, lane, N_EXPERTS), axis=-1, keepdims=True)
        vals.append(m)
        ids.append(idx)
        work = jnp.where(lane == idx, -jnp.inf, work)
    exps = [jnp.exp(vk - vals[0]) for vk in vals]
    den = exps[0] + exps[1] + exps[2] + exps[3]
    mask = jnp.zeros((n, N_EXPERTS), F32)
    for idx in ids:
        mask = mask + jnp.where(lane == idx, 1.0, 0.0)
    rank = jnp.dot(tri_ref[...], mask.astype(BF16), preferred_element_type=F32) + carry_ref[0:1, :]
    carry_ref[0:1, :] = carry_ref[0:1, :] + jnp.sum(mask, axis=0, keepdims=True)
    cnt_ref[...] = carry_ref[0:1, :]
    for kq in range(TOP_K):
        ids_ref[:, kq:kq + 1] = ids[kq]
        prob_ref[:, kq:kq + 1] = exps[kq] / den
        rank_ref[:, kq:kq + 1] = jnp.sum(jnp.where(lane == ids[kq], rank, 0.0), axis=-1, keepdims=True)


def _mid2(x1, o_p, o_s, w_o_bf, g_moe, w_router, b_router, tri):
    n = x1.shape[0]
    n_pt = o_p.shape[0] // TOK_TILE
    row = lambda w: pl.BlockSpec((TOK_TILE, w), lambda i: (i, 0))
    full = lambda a, b: pl.BlockSpec((a, b), lambda i: (0, 0))
    return pl.pallas_call(
        functools.partial(_mid2_kernel, n_pt=n_pt),
        grid=(n // TOK_TILE,),
        in_specs=[row(D_MODEL), *_two_source_specs(D_MODEL, n_pt), full(D_MODEL, D_MODEL), full(1, D_MODEL),
                  full(D_MODEL, N_EXPERTS), full(1, N_EXPERTS), full(TOK_TILE, TOK_TILE)],
        out_specs=[row(D_MODEL), row(D_MODEL), row(TOP_K), row(TOP_K), row(TOP_K), full(1, N_EXPERTS)],
        out_shape=[jax.ShapeDtypeStruct((n, D_MODEL), F32), jax.ShapeDtypeStruct((n, D_MODEL), F32),
                   jax.ShapeDtypeStruct((n, TOP_K), I32), jax.ShapeDtypeStruct((n, TOP_K), F32),
                   jax.ShapeDtypeStruct((n, TOP_K), F32), jax.ShapeDtypeStruct((1, N_EXPERTS), F32)],
        scratch_shapes=[pltpu.VMEM((8, N_EXPERTS), F32)],
        compiler_params=_cparams(("arbitrary",)),
        name="mid2",
    )(x1, o_p, o_s, w_o_bf, g_moe, w_router, b_router, tri)


def _ffn_kernel(te_ref, nt_ref, src_cur_ref, src_nxt_ref, h3_hbm, wg_ref, bg_ref, wu_ref, bu_ref,
                wd_ref, bd_ref, y_ref, xbuf, wbf, sem):
    i = pl.program_id(0)
    n_used = nt_ref[0]
    slot = lax.rem(i, 2)

    def gather(src_ref, dst_slot):
        for r in range(FFN_TILE):
            pltpu.make_async_copy(h3_hbm.at[pl.ds(src_ref[0, 0, r], 1), :],
                                  xbuf.at[dst_slot, pl.ds(r, 1), :], sem.at[dst_slot]).start()

    def wait(dst_slot):
        pltpu.make_async_copy(h3_hbm.at[pl.ds(0, FFN_TILE), :], xbuf.at[dst_slot], sem.at[dst_slot]).wait()

    @pl.when(i == 0)
    def _():
        gather(src_cur_ref, 0)

    @pl.when(jnp.logical_or(i == 0, te_ref[i] != te_ref[jnp.maximum(i - 1, 0)]))
    def _():
        wbf[0] = wg_ref[0].astype(BF16)
        wbf[1] = wu_ref[0].astype(BF16)
        wbf[2] = wd_ref[0].astype(BF16)

    @pl.when(i < n_used)
    def _():
        wait(slot)
        gather(src_nxt_ref, 1 - slot)
        x = xbuf[slot].astype(BF16)
        gate = jnp.minimum(jnp.dot(x, wbf[0], preferred_element_type=F32) + bg_ref[0], SWIGLU_LIMIT)
        up = jnp.clip(jnp.dot(x, wbf[1], preferred_element_type=F32) + bu_ref[0],
                      -SWIGLU_LIMIT, SWIGLU_LIMIT)
        hid = (up + 1.0) * gate * _sigmoid(gate * SWIGLU_ALPHA)
        y_ref[...] = jnp.dot(hid.astype(BF16), wbf[2], preferred_element_type=F32) + bd_ref[0]

    @pl.when(i == n_used)
    def _():
        wait(slot)

    @pl.when(i >= n_used)
    def _():
        y_ref[...] = jnp.zeros_like(y_ref)


def _ffn(tile_expert, n_used, src3, h3, wg, bg, wu, bu, wd, bd):
    n_tiles = src3.shape[0]
    last = n_tiles - 1
    wspec = pl.BlockSpec((1, D_MODEL, D_MODEL), lambda i, te, nt: (te[i], 0, 0))
    bspec = pl.BlockSpec((1, 1, D_MODEL), lambda i, te, nt: (te[i], 0, 0))
    grid_spec = pltpu.PrefetchScalarGridSpec(
        num_scalar_prefetch=2,
        grid=(n_tiles,),
        in_specs=[pl.BlockSpec((1, 1, FFN_TILE), lambda i, te, nt: (i, 0, 0), memory_space=pltpu.SMEM),
                  pl.BlockSpec((1, 1, FFN_TILE), lambda i, te, nt: (jnp.minimum(i + 1, last), 0, 0),
                               memory_space=pltpu.SMEM),
                  pl.BlockSpec(memory_space=pl.ANY),
                  wspec, bspec, wspec, bspec, wspec, bspec],
        out_specs=pl.BlockSpec((FFN_TILE, D_MODEL), lambda i, te, nt: (i, 0)),
        scratch_shapes=[pltpu.VMEM((2, FFN_TILE, D_MODEL), F32), pltpu.VMEM((3, D_MODEL, D_MODEL), BF16),
                        pltpu.SemaphoreType.DMA((2,))],
    )
    return pl.pallas_call(
        _ffn_kernel,
        grid_spec=grid_spec,
        out_shape=jax.ShapeDtypeStruct((n_tiles * FFN_TILE, D_MODEL), F32),
        compiler_params=_cparams(("arbitrary",)),
        name="ffn",
    )(tile_expert, n_used, src3, src3, h3, wg, bg, wu, bu, wd, bd)


def _combine_kernel(pos_cur_ref, pos_nxt_ref, ys_hbm, x2_ref, prob_ref, g_ref, o_ref, ybuf, sem, *, n_t):
    i = pl.program_id(0)
    slot = lax.rem(i, 2)

    def gather(pos_ref, dst_slot):
        for kq in range(TOP_K):
            for r in range(COMB_TILE):
                pltpu.make_async_copy(ys_hbm.at[pl.ds(pos_ref[0, 0, kq * COMB_TILE + r], 1), :],
                                      ybuf.at[dst_slot, kq, pl.ds(r, 1), :], sem.at[dst_slot]).start()

    @pl.when(i == 0)
    def _():
        gather(pos_cur_ref, 0)

    @pl.when(i + 1 < n_t)
    def _():
        gather(pos_nxt_ref, 1 - slot)

    for kq in range(TOP_K):
        pltpu.make_async_copy(ys_hbm.at[pl.ds(0, COMB_TILE), :], ybuf.at[slot, kq], sem.at[slot]).wait()
    acc = x2_ref[...]
    for kq in range(TOP_K):
        acc = acc + prob_ref[:, kq:kq + 1] * ybuf[slot, kq]
    o_ref[...] = _rms(acc, g_ref[...])


def _combine(pos3, ys, x2, prob, g_final, tile0, n_t):
    last = n_t - 1
    kern = functools.partial(_combine_kernel, n_t=n_t)
    return pl.pallas_call(
        kern,
        grid=(n_t,),
        in_specs=[pl.BlockSpec((1, 1, TOP_K * COMB_TILE), lambda i: (i, 0, 0), memory_space=pltpu.SMEM),
                  pl.BlockSpec((1, 1, TOP_K * COMB_TILE), lambda i: (jnp.minimum(i + 1, last), 0, 0),
                               memory_space=pltpu.SMEM),
                  pl.BlockSpec(memory_space=pl.ANY),
                  pl.BlockSpec((COMB_TILE, D_MODEL), lambda i: (tile0 + i, 0)),
                  pl.BlockSpec((COMB_TILE, TOP_K), lambda i: (tile0 + i, 0)),
                  pl.BlockSpec((1, D_MODEL), lambda i: (0, 0))],
        out_specs=pl.BlockSpec((COMB_TILE, D_MODEL), lambda i: (i, 0)),
        out_shape=jax.ShapeDtypeStruct((n_t * COMB_TILE, D_MODEL), F32),
        scratch_shapes=[pltpu.VMEM((2, TOP_K, COMB_TILE, D_MODEL), F32), pltpu.SemaphoreType.DMA((2,))],
        compiler_params=_cparams(("arbitrary",)),
        name="combine",
    )(pos3, pos3, ys, x2, prob, g_final)


def kernel(x_prompt, x_sample, mem_prompt, state_shift, state_wkv, state_conv, cache_mem_k, cache_mem_v,
           g_mix, w_in, mu_shift, w0, w_decay_up, a0, w_iclr_up, w_glora_up, k_k, k_a, r_k, gn_g, gn_b,
           dw_w, dw_b, cln_g, cln_b, w_out, g_xattn, g_mem, w_q, w_mk, w_mv, w_o,
           g_moe, w_router, b_router, w_moe_gate, b_moe_gate, w_moe_up, b_moe_up, w_moe_down, b_moe_down,
           g_final):
    bp, tp, _ = x_prompt.shape
    bs, ts, _ = x_sample.shape
    n_p, n_s = bp * tp, bs * ts
    n_all = n_p + n_s
    assert tp % WKV_BLOCK == 0 and tp % TOK_TILE == 0 and n_s == TOK_TILE
    assert ts <= SAMPLE_PAD and bs % SAMPLE_SEQS == 0
    assert n_p % COMB_TILE == 0 and n_s % COMB_TILE == 0
    row2 = lambda a: a.reshape(1, -1)

    x_p = x_prompt.reshape(n_p, D_MODEL)
    x_s = x_sample.reshape(n_s, D_MODEL)
    p_rw, u = _inproj(x_p, x_s, row2(g_mix[0]), w_in[0].astype(BF16))

    head_idx = jnp.arange(RW_WIDTH) // RW_HEAD_DIM
    eseg = (head_idx[:, None] == head_idx[None, :]).astype(BF16)
    wts = (row2(mu_shift[0]), row2(w0[0]), w_decay_up[0].astype(BF16), row2(a0[0]),
           w_iclr_up[0].astype(BF16), w_glora_up[0].astype(BF16), row2(k_k[0]), row2(k_a[0]),
           row2(r_k[0]), row2(gn_g[0]), row2(gn_b[0]), eseg)
    n_tp = tp // WKV_BLOCK
    rw_p, wkv_p = _wkv(p_rw, 0, jnp.zeros((bp, 1, RW_PROJ), F32),
                       jnp.zeros((bp, N_GROUPS, RW_HEAD_DIM, GW), F32), wts,
                       batch=bp, n_seq=1, tt_seq=WKV_BLOCK, chunk=WKV_CHUNK, t_valid=tp, n_t=n_tp,
                       out_rows=n_p)
    p_rw_s = p_rw[n_p:].reshape(bs, ts, RW_PROJ)
    p_rw_s_pad = jnp.pad(p_rw_s, ((0, 0), (0, SAMPLE_PAD - ts), (0, 0))).reshape(bs * SAMPLE_PAD, RW_PROJ)
    rw_s_pad, wkv_s = _wkv(p_rw_s_pad, 0, state_shift[0].reshape(bs, 1, RW_PROJ), _to_grouped(state_wkv[0]),
                           wts, batch=bs, n_seq=SAMPLE_SEQS, tt_seq=SAMPLE_PAD, chunk=SAMPLE_PAD,
                           t_valid=ts, n_t=1, out_rows=bs * SAMPLE_PAD)
    rw_s = rw_s_pad.reshape(bs, SAMPLE_PAD, RW_WIDTH)[:, :ts].reshape(n_s, RW_WIDTH)

    conv_w = (dw_w[0], row2(dw_b[0]), row2(cln_g[0]), row2(cln_b[0]))
    cv_p = _conv(u, 0, jnp.zeros((bp, CONV_PAD, CONV_WIDTH), F32), *conv_w,
                 batch=bp, tt=WKV_BLOCK, n_t=n_tp, out_rows=n_p)
    u_s = u[n_p:]
    u_s_pad = jnp.pad(u_s.reshape(bs, ts, CONV_WIDTH), ((0, 0), (0, SAMPLE_PAD - ts), (0, 0)))
    buf_s = jnp.pad(state_conv[0], ((0, 0), (CONV_PAD - (CONV_K - 1), 0), (0, 0)))
    cv_s_pad = _conv(u_s_pad.reshape(bs * SAMPLE_PAD, CONV_WIDTH), 0, buf_s, *conv_w,
                     batch=bs, tt=SAMPLE_PAD, n_t=1, out_rows=bs * SAMPLE_PAD)
    cv_s = cv_s_pad.reshape(bs, SAMPLE_PAD, CONV_WIDTH)[:, :ts].reshape(n_s, CONV_WIDTH)

    x1, q = _mid1(x_p, x_s, rw_p, rw_s, cv_p, cv_s, w_out[0].astype(BF16), row2(g_xattn[0]), w_q[0].astype(BF16))

    mk_p, mv_p = _memkv(mem_prompt.reshape(bp * N_MEM, D_MODEL), row2(g_mem[0]),
                        w_mk[0].astype(BF16), w_mv[0].astype(BF16))
    mk_p = mk_p.reshape(bp, N_MEM, D_MODEL)
    mv_p = mv_p.reshape(bp, N_MEM, D_MODEL)
    o_p = _attn_rows(q, mk_p, mv_p, n_p, tp // TOK_TILE)
    o_s = _attn_seq(q[n_p:].reshape(bs, ts, D_MODEL), cache_mem_k[0].reshape(bs, N_MEM, D_MODEL),
                    cache_mem_v[0].reshape(bs, N_MEM, D_MODEL)).reshape(n_s, D_MODEL)

    tri = (jnp.arange(TOK_TILE)[None, :] < jnp.arange(TOK_TILE)[:, None]).astype(BF16)
    x2, h3, ids, prob, rank, cnt = _mid2(x1, o_p, o_s, w_o[0].astype(BF16), row2(g_moe[0]),
                                         w_router[0], row2(b_router[0]), tri)

    counts = cnt[0].astype(I32)
    padded = ((counts + FFN_TILE - 1) // FFN_TILE) * FFN_TILE
    ends = jnp.cumsum(padded)
    offs = ends - padded
    pos = offs[ids] + rank.astype(I32)
    n_tiles = (n_all * TOP_K) // FFN_TILE + N_EXPERTS
    n_used = ends[-1] // FFN_TILE
    tile_ids = jnp.minimum(jnp.arange(n_tiles, dtype=I32), n_used - 1)
    tile_expert = jnp.sum((ends[None, :] // FFN_TILE) <= tile_ids[:, None], axis=1).astype(I32)
    tile_expert = jnp.minimum(tile_expert, N_EXPERTS - 1)
    tok = jnp.broadcast_to(jnp.arange(n_all, dtype=I32)[:, None], (n_all, TOP_K))
    src = jnp.zeros((n_tiles * FFN_TILE,), I32).at[pos.reshape(-1)].set(tok.reshape(-1))

    ys = _ffn(tile_expert, n_used.reshape(1).astype(I32), src.reshape(n_tiles, 1, FFN_TILE), h3,
              w_moe_gate[0], b_moe_gate[0].reshape(N_EXPERTS, 1, D_MODEL),
              w_moe_up[0], b_moe_up[0].reshape(N_EXPERTS, 1, D_MODEL),
              w_moe_down[0], b_moe_down[0].reshape(N_EXPERTS, 1, D_MODEL))

    def pos_tiles(p2):
        nt = p2.shape[0] // COMB_TILE
        return p2.reshape(nt, COMB_TILE, TOP_K).transpose(0, 2, 1).reshape(nt, 1, TOP_K * COMB_TILE)

    gfin = row2(g_final)
    y_p = _combine(pos_tiles(pos[:n_p]), ys, x2, prob, gfin, 0, n_p // COMB_TILE)
    y_s = _combine(pos_tiles(pos[n_p:]), ys, x2, prob, gfin, n_p // COMB_TILE, n_s // COMB_TILE)

    p_rw_p3 = p_rw[:n_p].reshape(bp, tp, RW_PROJ)
    u_p3 = u[:n_p].reshape(bp, tp, CONV_WIDTH)
    new_conv_s = jnp.concatenate([state_conv[0], u_s.reshape(bs, ts, CONV_WIDTH)], axis=1)[:, -(CONV_K - 1):]
    kv_shape = (1, bp, N_MEM, X_HEADS, X_HEAD_DIM)
    return (y_p.reshape(bp, tp, D_MODEL), y_s.reshape(bs, ts, D_MODEL),
            p_rw_p3[:, -1][None], _from_grouped(wkv_p)[None], u_p3[:, -(CONV_K - 1):][None],
            mk_p.reshape(kv_shape), mv_p.reshape(kv_shape),
            p_rw_s[:, -1][None], _from_grouped(wkv_s)[None], new_conv_s[None])
```

```python
import functools

import jax
import jax.numpy as jnp
from jax import lax
from jax.experimental import pallas as pl
from jax.experimental.pallas import tpu as pltpu

F32 = jnp.float32
BF16 = jnp.bfloat16
I32 = jnp.int32

D_MODEL = 1024
RW_HEADS = 8
RW_HEAD_DIM = 64
RW_WIDTH = RW_HEADS * RW_HEAD_DIM
DECAY_LORA = 64
ICLR_LORA = 64
GATE_LORA = 128
RW_PROJ = 3 * RW_WIDTH + DECAY_LORA + ICLR_LORA + GATE_LORA
GN_EPS = 64e-5
CONV_WIDTH = D_MODEL - RW_WIDTH
CONV_K = 31
LN_EPS = 1e-5
IN_PROJ = RW_PROJ + 2 * CONV_WIDTH
N_MEM = 256
X_HEADS = 4
X_HEAD_DIM = D_MODEL // X_HEADS
ATTN_SCALE = X_HEAD_DIM ** -0.5
N_EXPERTS = 32
TOP_K = 4
SWIGLU_LIMIT = 7.0
SWIGLU_ALPHA = 1.702
NORM_EPS = 1e-5

TOK_TILE = 512
WKV_BLOCK = 256
WKV_CHUNK = 64
SAMPLE_PAD = 8
SAMPLE_SEQS = 4
GROUP = 4
GW = GROUP * RW_HEAD_DIM
N_GROUPS = RW_HEADS // GROUP
FFN_TILE = 512
STRIP = 8
SLOT_ROWS = TOP_K * TOK_TILE + N_EXPERTS * STRIP
CONV_PAD = 32
VMEM_LIMIT = 56 * 1024 * 1024


def _cparams(sem):
    return pltpu.CompilerParams(dimension_semantics=sem, vmem_limit_bytes=VMEM_LIMIT)


def _dot(a, b):
    return jnp.dot(a.astype(BF16), b.astype(BF16), preferred_element_type=F32)


def _dot_nt(a, b):
    return lax.dot_general(a.astype(BF16), b.astype(BF16), (((1,), (1,)), ((), ())),
                           preferred_element_type=F32)


def _dot_tn(a, b):
    return lax.dot_general(a.astype(BF16), b.astype(BF16), (((0,), (0,)), ((), ())),
                           preferred_element_type=F32)


def _split2(x):
    hi = x.astype(BF16)
    lo = (x - hi.astype(F32)).astype(BF16)
    return hi, lo


def _split3(x):
    hi = x.astype(BF16)
    r1 = x - hi.astype(F32)
    mid = r1.astype(BF16)
    lo = (r1 - mid.astype(F32)).astype(BF16)
    return hi, mid, lo


def _segsum(x, eseg):
    hi, lo = _split2(x)
    return (jnp.dot(hi, eseg, preferred_element_type=F32)
            + jnp.dot(lo, eseg, preferred_element_type=F32))


def _rms(x, g):
    return x * lax.rsqrt(jnp.mean(x * x, axis=-1, keepdims=True) + NORM_EPS) * g


def _sigmoid(x):
    return 1.0 / (1.0 + jnp.exp(-x))


def _pick(first, a_ref, b_ref):
    return jnp.where(first, a_ref[...], b_ref[...])


def _two_source_specs(width, n_pt):
    return (pl.BlockSpec((TOK_TILE, width), lambda i: (jnp.minimum(i, n_pt - 1), 0)),
            pl.BlockSpec((TOK_TILE, width), lambda i: (jnp.maximum(i - n_pt, 0), 0)))


def _inproj_kernel(xp_ref, xs_ref, g_ref, w_ref, prw_ref, u_ref, *, n_pt):
    h = _rms(_pick(pl.program_id(0) < n_pt, xp_ref, xs_ref), g_ref[...])
    p = _dot(h, w_ref[...])
    prw_ref[...] = p[:, :RW_PROJ]
    u_ref[...] = p[:, RW_PROJ:RW_PROJ + CONV_WIDTH] * _sigmoid(p[:, RW_PROJ + CONV_WIDTH:])


def _inproj(x_p, x_s, g_mix, w_in_bf):
    n = x_p.shape[0] + x_s.shape[0]
    n_pt = x_p.shape[0] // TOK_TILE
    return pl.pallas_call(
        functools.partial(_inproj_kernel, n_pt=n_pt),
        grid=(n // TOK_TILE,),
        in_specs=[*_two_source_specs(D_MODEL, n_pt),
                  pl.BlockSpec((1, D_MODEL), lambda i: (0, 0)),
                  pl.BlockSpec((D_MODEL, IN_PROJ), lambda i: (0, 0))],
        out_specs=[pl.BlockSpec((TOK_TILE, RW_PROJ), lambda i: (i, 0)),
                   pl.BlockSpec((TOK_TILE, CONV_WIDTH), lambda i: (i, 0))],
        out_shape=[jax.ShapeDtypeStruct((n, RW_PROJ), F32),
                   jax.ShapeDtypeStruct((n, CONV_WIDTH), F32)],
        compiler_params=_cparams(("parallel",)),
        name="inproj",
    )(x_p, x_s, g_mix, w_in_bf)


def _bd(x, w):
    r, lanes = x.shape
    head = lax.broadcasted_iota(I32, (r, lanes), 1) >> (w.bit_length() - 1)
    return jnp.concatenate([jnp.where(head == h, x, 0.0) for h in range(GROUP)], axis=0).astype(BF16)


def _collapse(x):
    head = lax.broadcasted_iota(I32, (RW_HEAD_DIM, GW), 1) >> (RW_HEAD_DIM.bit_length() - 1)
    out = jnp.zeros((RW_HEAD_DIM, GW), F32)
    for h in range(GROUP):
        out = out + jnp.where(head == h, x[h * RW_HEAD_DIM:(h + 1) * RW_HEAD_DIM, :], 0.0)
    return out


def _wkv_kernel(p_ref, prev_ref, s0_ref, mu_ref, w0_ref, wdup_ref, a0_ref, aup_ref, gup_ref,
                kk_ref, ka_ref, rk_ref, gng_ref, gnb_ref, eseg_ref,
                out_ref, sfin_ref,
                s_ref, last_ref, rt_ref, at_ref, bt_ref, kt_ref, v_ref, pc_ref, y_ref,
                r_ref, km_ref, g_ref, *, n_seq, tt_seq, chunk, t_valid, n_t):
    t = pl.program_id(1)
    tt = n_seq * tt_seq
    n_chunk = tt_seq // chunk
    n_fused = chunk.bit_length() - 2
    nd = RW_HEAD_DIM
    c1, c2, c3 = RW_WIDTH, 2 * RW_WIDTH, 3 * RW_WIDTH
    c4 = c3 + DECAY_LORA
    c5 = c4 + ICLR_LORA

    @pl.when(t == 0)
    def _():
        s_ref[...] = s0_ref[...]
        if n_seq == 1:
            last_ref[0:1, :] = prev_ref[0]

    p = p_ref[...]
    row = lax.broadcasted_iota(I32, (tt, 1), 0)
    seq_row = row & (tt_seq - 1)
    rolled = pltpu.roll(p, shift=1, axis=0)
    if n_seq == 1:
        ps = jnp.where(seq_row == 0, last_ref[0:1, :], rolled)
        last_ref[0:1, :] = p[tt - 1:tt, :]
    else:
        prev_rows = jnp.concatenate(
            [jnp.broadcast_to(prev_ref[q], (tt_seq, RW_PROJ)) for q in range(n_seq)], axis=0)
        ps = jnp.where(seq_row == 0, prev_rows, rolled)
    z = p + mu_ref[...] * (ps - p)
    r, k, v = z[:, :c1], z[:, c1:c2], z[:, c2:c3]
    wd, ad, gd = z[:, c3:c4], z[:, c4:c5], z[:, c5:]
    wpre = w0_ref[...] + _dot(jnp.tanh(wd), wdup_ref[...])
    neg = -wpre
    w = -(jnp.maximum(neg, 0.0) + jnp.log(1.0 + jnp.exp(-jnp.abs(neg)))) - 0.5
    a = _sigmoid(a0_ref[...] + _dot(ad, aup_ref[...]))
    g_ref[...] = _dot(_sigmoid(gd), gup_ref[...])
    eseg = eseg_ref[...]
    kk = k * kk_ref[...]
    kk = kk / jnp.maximum(jnp.sqrt(_segsum(kk * kk, eseg)), 1e-12)
    kmod = k * (1.0 + (a - 1.0) * ka_ref[...])
    logd = -jnp.exp(w)
    if t_valid < tt_seq * n_t:
        valid = (t * tt_seq + seq_row) < t_valid
        logd = jnp.where(valid, logd, 0.0)
        kk = jnp.where(valid, kk, 0.0)
        kmod = jnp.where(valid, kmod, 0.0)
        v = jnp.where(valid, v, 0.0)
    ri = lax.broadcasted_iota(I32, (tt, tt), 0)
    ci = lax.broadcasted_iota(I32, (tt, tt), 1)
    shift = chunk.bit_length() - 1
    ltri = jnp.where(ci <= ri, jnp.where((ri >> shift) == (ci >> shift), 1.0, 0.0), 0.0).astype(BF16)
    hi, mid, lo = _split3(logd)
    logp = (jnp.dot(ltri, hi, preferred_element_type=F32)
            + jnp.dot(ltri, mid, preferred_element_type=F32)
            + jnp.dot(ltri, lo, preferred_element_type=F32))
    pcum = jnp.exp(logp)
    pinv = jnp.exp(-logp)
    r_ref[...] = r
    km_ref[...] = kmod
    v_ref[...] = v
    pc_ref[...] = pcum
    rt_ref[...] = r * pcum
    at_ref[...] = -kk * jnp.exp(logp - logd)
    bt_ref[...] = kk * a * pinv
    kt_ref[...] = kmod * pinv

    cw = GROUP * chunk
    ti = lax.broadcasted_iota(I32, (chunk, cw), 0)
    si = lax.broadcasted_iota(I32, (chunk, cw), 1) & (chunk - 1)
    strict = si < ti
    incl = si <= ti
    eye_c = jnp.where(si == ti, 1.0, 0.0)
    ji = lax.broadcasted_iota(I32, (nd, GW), 0)
    jj = lax.broadcasted_iota(I32, (nd, GW), 1) & (nd - 1)
    eye_n = jnp.where(ji == jj, 1.0, 0.0)

    chains = [(s, c, g) for s in range(n_seq) for c in range(n_chunk) for g in range(N_GROUPS)]

    def blk(ref, s, c, g):
        r0 = s * tt_seq + c * chunk
        return ref[r0:r0 + chunk, g * GW:(g + 1) * GW]

    am = {q: blk(at_ref, *q) for q in chains}
    rm = {q: blk(rt_ref, *q) for q in chains}
    bm = {q: blk(bt_ref, *q) for q in chains}
    km = {q: blk(kt_ref, *q) for q in chains}
    vm = {q: blk(v_ref, *q) for q in chains}
    a_ab, a_ak, a_rb, a_rk = {}, {}, {}, {}
    for q in chains:
        ar = jnp.concatenate([am[q], rm[q]], axis=0)
        s1 = _dot_nt(ar, _bd(bm[q], nd))
        s2 = _dot_nt(ar, _bd(km[q], nd))
        a_ab[q] = jnp.where(strict, s1[:chunk], 0.0)
        a_rb[q] = jnp.where(incl, s1[chunk:], 0.0)
        a_ak[q] = jnp.where(strict, s2[:chunk], 0.0)
        a_rk[q] = jnp.where(incl, s2[chunk:], 0.0)
    tinv = {q: eye_c + a_ab[q] for q in chains}
    pw = {q: _dot(a_ab[q], _bd(a_ab[q], chunk)) for q in chains}
    for lvl in range(n_fused):
        last = lvl == n_fused - 1
        for q in chains:
            lhs = tinv[q] if last else jnp.concatenate([tinv[q], pw[q]], axis=0)
            zz = _dot(lhs, _bd(pw[q], chunk))
            tinv[q] = tinv[q] + zz[:chunk]
            if not last:
                pw[q] = zz[chunk:]
    akv, rkv = {}, {}
    for q in chains:
        zv = _dot(jnp.concatenate([a_ak[q], a_rk[q]], axis=0), _bd(vm[q], nd))
        akv[q], rkv[q] = zv[:chunk], zv[chunk:]
    ap = {q: _dot(tinv[q], _bd(am[q], nd)) for q in chains}
    uv = {q: _dot(tinv[q], _bd(akv[q], nd)) for q in chains}
    rp = {q: rm[q] + _dot(a_rb[q], _bd(ap[q], nd)) for q in chains}
    yv = {q: _dot(a_rb[q], _bd(uv[q], nd)) + rkv[q] for q in chains}
    mm, gm = {}, {}
    for q in chains:
        s, c, g = q
        r_end = s * tt_seq + (c + 1) * chunk
        pc = pc_ref[r_end - 8:r_end, g * GW:(g + 1) * GW][7:8, :]
        cross = _dot_tn(jnp.concatenate([ap[q], uv[q]], axis=1), bm[q])
        vk = _dot_tn(vm[q], km[q])
        mm[q] = (eye_n + _collapse(cross[:GW])) * pc
        gm[q] = (_collapse(cross[GW:]) + _collapse(vk)) * pc

    for s in range(n_seq):
        for g in range(N_GROUPS):
            st = s_ref[s, g]
            for c in range(n_chunk):
                q = (s, c, g)
                r0 = s * tt_seq + c * chunk
                y_ref[r0:r0 + chunk, g * GW:(g + 1) * GW] = _dot_nt(rp[q], _bd(st, nd)) + yv[q]
                st = _dot(st, _bd(mm[q], nd)) + gm[q]
            s_ref[s, g] = st

    y = y_ref[...]
    inv_n = 1.0 / RW_HEAD_DIM
    mean = _segsum(y, eseg) * inv_n
    yc = y - mean
    var = _segsum(yc * yc, eseg) * inv_n
    yn = yc * lax.rsqrt(var + GN_EPS) * gng_ref[...] + gnb_ref[...]
    bonus = _segsum(r_ref[...] * km_ref[...] * rk_ref[...], eseg) * v_ref[...]
    out_ref[...] = (yn + bonus) * g_ref[...]

    @pl.when(t == n_t - 1)
    def _():
        sfin_ref[...] = s_ref[...]


def _wkv(p_rows, row_block0, prev, s0, wts, *, batch, n_seq, tt_seq, chunk, t_valid, n_t, out_rows):
    (mu, w0, wdup, a0, aup, gup, k_k, k_a, r_k, gn_g, gn_b, eseg) = wts
    tt = n_seq * tt_seq
    kern = functools.partial(_wkv_kernel, n_seq=n_seq, tt_seq=tt_seq, chunk=chunk, t_valid=t_valid, n_t=n_t)
    const = lambda shape: pl.BlockSpec(shape, lambda b, t: tuple(0 for _ in shape))
    blk = pltpu.VMEM((tt, RW_WIDTH), F32)
    st_spec = pl.BlockSpec((n_seq, N_GROUPS, RW_HEAD_DIM, GW), lambda b, t: (b, 0, 0, 0))
    return pl.pallas_call(
        kern,
        grid=(batch // n_seq, n_t),
        in_specs=[pl.BlockSpec((tt, RW_PROJ), lambda b, t: (row_block0 + b * n_t + t, 0)),
                  pl.BlockSpec((n_seq, 1, RW_PROJ), lambda b, t: (b, 0, 0)),
                  st_spec,
                  const((1, RW_PROJ)), const((1, RW_WIDTH)), const((DECAY_LORA, RW_WIDTH)),
                  const((1, RW_WIDTH)), const((ICLR_LORA, RW_WIDTH)), const((GATE_LORA, RW_WIDTH)),
                  const((1, RW_WIDTH)), const((1, RW_WIDTH)), const((1, RW_WIDTH)),
                  const((1, RW_WIDTH)), const((1, RW_WIDTH)), const((RW_WIDTH, RW_WIDTH))],
        out_specs=[pl.BlockSpec((tt, RW_WIDTH), lambda b, t: (b * n_t + t, 0)), st_spec],
        out_shape=[jax.ShapeDtypeStruct((out_rows, RW_WIDTH), F32),
                   jax.ShapeDtypeStruct((batch, N_GROUPS, RW_HEAD_DIM, GW), F32)],
        scratch_shapes=[pltpu.VMEM((n_seq, N_GROUPS, RW_HEAD_DIM, GW), F32),
                        pltpu.VMEM((8, RW_PROJ), F32),
                        blk, blk, blk, blk, blk, blk, blk, blk, blk, blk],
        compiler_params=_cparams(("parallel", "arbitrary")),
        name="wkv",
    )(p_rows, prev, s0, mu, w0, wdup, a0, aup, gup, k_k, k_a, r_k, gn_g, gn_b, eseg)


def _to_grouped(state):
    b = state.shape[0]
    return (state.reshape(b, N_GROUPS, GROUP, RW_HEAD_DIM, RW_HEAD_DIM).transpose(0, 1, 3, 2, 4)
            .reshape(b, N_GROUPS, RW_HEAD_DIM, GW))


def _from_grouped(state):
    b = state.shape[0]
    return (state.reshape(b, N_GROUPS, RW_HEAD_DIM, GROUP, RW_HEAD_DIM).transpose(0, 1, 3, 2, 4)
            .reshape(b, RW_HEADS, RW_HEAD_DIM, RW_HEAD_DIM))


def _conv_kernel(u_ref, buf_ref, w_ref, b_ref, g_ref, beta_ref, o_ref, ext_ref, *, tt, n_t):
    t = pl.program_id(1)

    @pl.when(t == 0)
    def _():
        ext_ref[0:CONV_PAD, :] = buf_ref[0]

    ext_ref[CONV_PAD:CONV_PAD + tt, :] = u_ref[...]
    acc = jnp.zeros((tt, CONV_WIDTH), F32) + b_ref[...]
    first = CONV_PAD - (CONV_K - 1)
    for j in range(CONV_K):
        acc = acc + ext_ref[pl.ds(first + j, tt), :] * w_ref[j:j + 1, :]
    mean = jnp.mean(acc, axis=-1, keepdims=True)
    cen = acc - mean
    var = jnp.mean(cen * cen, axis=-1, keepdims=True)
    cn = cen * lax.rsqrt(var + LN_EPS) * g_ref[...] + beta_ref[...]
    o_ref[...] = cn * _sigmoid(cn)
    if n_t > 1:
        ext_ref[0:CONV_PAD, :] = ext_ref[tt:tt + CONV_PAD, :]


def _conv(u_rows, row_block0, buf, dw_w, dw_b, ln_g, ln_b, *, batch, tt, n_t, out_rows):
    kern = functools.partial(_conv_kernel, tt=tt, n_t=n_t)
    const = lambda shape: pl.BlockSpec(shape, lambda b, t: tuple(0 for _ in shape))
    return pl.pallas_call(
        kern,
        grid=(batch, n_t),
        in_specs=[pl.BlockSpec((tt, CONV_WIDTH), lambda b, t: (row_block0 + b * n_t + t, 0)),
                  pl.BlockSpec((1, CONV_PAD, CONV_WIDTH), lambda b, t: (b, 0, 0)),
                  const((CONV_K, CONV_WIDTH)), const((1, CONV_WIDTH)),
                  const((1, CONV_WIDTH)), const((1, CONV_WIDTH))],
        out_specs=pl.BlockSpec((tt, CONV_WIDTH), lambda b, t: (b * n_t + t, 0)),
        out_shape=jax.ShapeDtypeStruct((out_rows, CONV_WIDTH), F32),
        scratch_shapes=[pltpu.VMEM((CONV_PAD + tt, CONV_WIDTH), F32)],
        compiler_params=_cparams(("parallel", "arbitrary")),
        name="conv",
    )(u_rows, buf, dw_w, dw_b, ln_g, ln_b)


def _mid1_kernel(xp_ref, xs_ref, rwp_ref, rws_ref, cvp_ref, cvs_ref, wo_ref, g_ref, wq_ref, x1_ref, q_ref,
                 *, n_pt):
    first = pl.program_id(0) < n_pt
    rw = _pick(first, rwp_ref, rws_ref)
    cv = _pick(first, cvp_ref, cvs_ref)
    mix = _dot(rw, wo_ref[:RW_WIDTH, :]) + _dot(cv, wo_ref[RW_WIDTH:, :])
    x1 = _pick(first, xp_ref, xs_ref) + mix
    x1_ref[...] = x1
    q = _dot(_rms(x1, g_ref[...]), wq_ref[...])
    q_ref[...] = (q * ATTN_SCALE).astype(BF16)


def _mid1(x_p, x_s, rw_p, rw_s, cv_p, cv_s, w_out_bf, g_x, w_q_bf):
    n = x_p.shape[0] + x_s.shape[0]
    n_pt = rw_p.shape[0] // TOK_TILE
    row = lambda w: pl.BlockSpec((TOK_TILE, w), lambda i: (i, 0))
    full = lambda a, b: pl.BlockSpec((a, b), lambda i: (0, 0))
    return pl.pallas_call(
        functools.partial(_mid1_kernel, n_pt=n_pt),
        grid=(n // TOK_TILE,),
        in_specs=[*_two_source_specs(D_MODEL, n_pt), *_two_source_specs(RW_WIDTH, n_pt),
                  *_two_source_specs(CONV_WIDTH, n_pt),
                  full(D_MODEL, D_MODEL), full(1, D_MODEL), full(D_MODEL, D_MODEL)],
        out_specs=[row(D_MODEL), row(D_MODEL)],
        out_shape=[jax.ShapeDtypeStruct((n, D_MODEL), F32), jax.ShapeDtypeStruct((n, D_MODEL), BF16)],
        compiler_params=_cparams(("parallel",)),
        name="mid1",
    )(x_p, x_s, rw_p, rw_s, cv_p, cv_s, w_out_bf, g_x, w_q_bf)


def _memkv_kernel(m_ref, g_ref, wk_ref, wv_ref, k_ref, v_ref):
    m = _rms(m_ref[...], g_ref[...])
    k_ref[...] = _dot(m, wk_ref[...])
    v_ref[...] = _dot(m, wv_ref[...])


def _memkv(mem, g_mem, w_mk_bf, w_mv_bf):
    n = mem.shape[0]
    tile = N_MEM
    row = pl.BlockSpec((tile, D_MODEL), lambda i: (i, 0))
    full = lambda a, b: pl.BlockSpec((a, b), lambda i: (0, 0))
    return pl.pallas_call(
        _memkv_kernel,
        grid=(n // tile,),
        in_specs=[row, full(1, D_MODEL), full(D_MODEL, D_MODEL), full(D_MODEL, D_MODEL)],
        out_specs=[row, row],
        out_shape=[jax.ShapeDtypeStruct((n, D_MODEL), F32), jax.ShapeDtypeStruct((n, D_MODEL), F32)],
        compiler_params=_cparams(("parallel",)),
        name="memkv",
    )(mem, g_mem, w_mk_bf, w_mv_bf)


def _attn_head(q_h, k_h, v_h):
    s = _dot_nt(q_h, k_h)
    e = jnp.exp(s - jnp.max(s, axis=-1, keepdims=True))
    pr = e / jnp.sum(e, axis=-1, keepdims=True)
    return _dot(pr, v_h).astype(BF16)


def _attn_rows_kernel(q_ref, k_ref, v_ref, o_ref):
    q = q_ref[...]
    for h in range(X_HEADS):
        hs = slice(h * X_HEAD_DIM, (h + 1) * X_HEAD_DIM)
        o_ref[:, hs] = _attn_head(q[:, hs], k_ref[0, :, hs], v_ref[0, :, hs])


def _attn_seq_kernel(q_ref, k_hbm, v_hbm, o_ref, kbuf, vbuf, sem, *, n_b):
    b = pl.program_id(0)
    slot = lax.rem(b, 2)

    def copies(seq, s):
        out = []
        for h in range(X_HEADS):
            out.append(pltpu.make_async_copy(k_hbm.at[0, seq, :, h, :], kbuf.at[s, h], sem.at[s]))
            out.append(pltpu.make_async_copy(v_hbm.at[0, seq, :, h, :], vbuf.at[s, h], sem.at[s]))
        return out

    @pl.when(b == 0)
    def _():
        for cp in copies(0, 0):
            cp.start()

    @pl.when(b + 1 < n_b)
    def _():
        for cp in copies(b + 1, 1 - slot):
            cp.start()

    for cp in copies(b, slot):
        cp.wait()
    q = q_ref[0]
    for h in range(X_HEADS):
        hs = slice(h * X_HEAD_DIM, (h + 1) * X_HEAD_DIM)
        o_ref[0, :, hs] = _attn_head(q[:, hs], kbuf[slot, h], vbuf[slot, h])


def _attn_rows(q, mk, mv, n_rows, tiles_per_seq):
    qspec = pl.BlockSpec((TOK_TILE, D_MODEL), lambda i, j: (i * tiles_per_seq + j, 0))
    kvspec = pl.BlockSpec((1, N_MEM, D_MODEL), lambda i, j: (i, 0, 0))
    return pl.pallas_call(
        _attn_rows_kernel,
        grid=(mk.shape[0], tiles_per_seq),
        in_specs=[qspec, kvspec, kvspec],
        out_specs=qspec,
        out_shape=jax.ShapeDtypeStruct((n_rows, D_MODEL), BF16),
        compiler_params=_cparams(("parallel", "parallel")),
        name="attn",
    )(q, mk, mv)


def _attn_seq(q3, mk, mv):
    b, t, _ = q3.shape
    qspec = pl.BlockSpec((1, t, D_MODEL), lambda i: (i, 0, 0))
    anyspec = pl.BlockSpec(memory_space=pl.ANY)
    head_buf = pltpu.VMEM((2, X_HEADS, N_MEM, X_HEAD_DIM), F32)
    return pl.pallas_call(
        functools.partial(_attn_seq_kernel, n_b=b),
        grid=(b,),
        in_specs=[qspec, anyspec, anyspec],
        out_specs=qspec,
        out_shape=jax.ShapeDtypeStruct((b, t, D_MODEL), BF16),
        scratch_shapes=[head_buf, head_buf, pltpu.SemaphoreType.DMA((2,))],
        compiler_params=_cparams(("arbitrary",)),
        name="attn",
    )(q3, mk, mv)


def _mid2_kernel(x1_ref, op_ref, os_ref, wo_ref, g_ref, wr_ref, br_ref, tri_ref,
                 x2_ref, h3_ref, ids_ref, prob_ref, lrank_ref, cnt_ref, base_ref, tot_ref, carry_ref, *, n_pt):
    i = pl.program_id(0)

    @pl.when(i == 0)
    def _():
        carry_ref[...] = jnp.zeros_like(carry_ref)

    x2 = x1_ref[...] + _dot(_pick(i < n_pt, op_ref, os_ref), wo_ref[...])
    x2_ref[...] = x2
    h3 = _rms(x2, g_ref[...])
    h3_ref[...] = h3.astype(BF16)
    h3_hi, h3_lo = _split2(h3)
    wr_hi, wr_lo = _split2(wr_ref[...])
    logits = (jnp.dot(h3_hi, wr_hi, preferred_element_type=F32)
              + jnp.dot(h3_hi, wr_lo, preferred_element_type=F32)
              + jnp.dot(h3_lo, wr_hi, preferred_element_type=F32)) + br_ref[...]
    n = logits.shape[0]
    lane = lax.broadcasted_iota(I32, (n, N_EXPERTS), 1)
    work = logits
    vals, ids = [], []
    for _ in range(TOP_K):
        m = jnp.max(work, axis=-1, keepdims=True)
        idx = jnp.min(jnp.where(work == m, lane, N_EXPERTS), axis=-1, keepdims=True)
        vals.append(m)
        ids.append(idx)
        work = jnp.where(lane == idx, -jnp.inf, work)
    exps = [jnp.exp(vk - vals[0]) for vk in vals]
    den = exps[0] + exps[1] + exps[2] + exps[3]
    mask = jnp.zeros((n, N_EXPERTS), F32)
    for idx in ids:
        mask = mask + jnp.where(lane == idx, 1.0, 0.0)
    lrank = jnp.dot(tri_ref[...], mask.astype(BF16), preferred_element_type=F32)
    cnt = jnp.sum(mask, axis=0, keepdims=True)
    cnt_ref[0] = cnt
    base_ref[0] = carry_ref[0:1, :]
    carry_ref[0:1, :] = carry_ref[0:1, :] + jnp.floor((cnt + (STRIP - 1)) * (1.0 / STRIP)) * STRIP
    tot_ref[...] = carry_ref[0:1, :]
    for kq in range(TOP_K):
        ids_ref[:, kq:kq + 1] = ids[kq]
        prob_ref[:, kq:kq + 1] = exps[kq] / den
        lrank_ref[:, kq:kq + 1] = jnp.sum(jnp.where(lane == ids[kq], lrank, 0.0), axis=-1, keepdims=True)


def _mid2(x1, o_p, o_s, w_o_bf, g_moe, w_router, b_router, tri):
    n = x1.shape[0]
    n_t = n // TOK_TILE
    n_pt = o_p.shape[0] // TOK_TILE
    row = lambda w: pl.BlockSpec((TOK_TILE, w), lambda i: (i, 0))
    full = lambda a, b: pl.BlockSpec((a, b), lambda i: (0, 0))
    per_tile = pl.BlockSpec((1, 1, N_EXPERTS), lambda i: (i, 0, 0))
    return pl.pallas_call(
        functools.partial(_mid2_kernel, n_pt=n_pt),
        grid=(n_t,),
        in_specs=[row(D_MODEL), *_two_source_specs(D_MODEL, n_pt), full(D_MODEL, D_MODEL), full(1, D_MODEL),
                  full(D_MODEL, N_EXPERTS), full(1, N_EXPERTS), full(TOK_TILE, TOK_TILE)],
        out_specs=[row(D_MODEL), row(D_MODEL), row(TOP_K), row(TOP_K), row(TOP_K), per_tile, per_tile,
                   full(1, N_EXPERTS)],
        out_shape=[jax.ShapeDtypeStruct((n, D_MODEL), F32), jax.ShapeDtypeStruct((n, D_MODEL), BF16),
                   jax.ShapeDtypeStruct((n, TOP_K), I32), jax.ShapeDtypeStruct((n, TOP_K), F32),
                   jax.ShapeDtypeStruct((n, TOP_K), F32),
                   jax.ShapeDtypeStruct((n_t, 1, N_EXPERTS), F32), jax.ShapeDtypeStruct((n_t, 1, N_EXPERTS), F32),
                   jax.ShapeDtypeStruct((1, N_EXPERTS), F32)],
        scratch_shapes=[pltpu.VMEM((8, N_EXPERTS), F32)],
        compiler_params=_cparams(("arbitrary",)),
        name="mid2",
    )(x1, o_p, o_s, w_o_bf, g_moe, w_router, b_router, tri)


def _slot_select(ids_ref, lrank_ref, lbase_v, weights):
    n = ids_ref.shape[0]
    lane = lax.broadcasted_iota(I32, (n, N_EXPERTS), 1)
    col = lax.broadcasted_iota(I32, (n, SLOT_ROWS), 1)
    sel = jnp.zeros((n, SLOT_ROWS), F32)
    for kq in range(TOP_K):
        base = jnp.sum(jnp.where(lane == ids_ref[:, kq:kq + 1], lbase_v, 0.0), axis=-1, keepdims=True)
        dest = (base + lrank_ref[:, kq:kq + 1]).astype(I32)
        sel = jnp.where(col == dest, weights[kq], sel)
    return sel.astype(BF16)


def _strip_copies(c8_ref, lb_ref, gb_ref, make):
    for e in range(N_EXPERTS):
        rows = c8_ref[0, 0, e]

        @pl.when(rows > 0)
        def _():
            make(pl.multiple_of(lb_ref[0, 0, e], STRIP), pl.multiple_of(gb_ref[0, 0, e], STRIP),
                 pl.multiple_of(rows, STRIP)).start()


def _dispatch_kernel(tot_ref, tailn_ref, tailo_ref, nt_ref,
                     c8_ref, lb_ref, gb_ref, h3_ref, ids_ref, lrank_ref, lbv_ref,
                     xs_hbm, cbuf, zbuf, sem, *, n_t, n_tiles):
    i = pl.program_id(0)
    slot = lax.rem(i, 2)
    sel = _slot_select(ids_ref, lrank_ref, lbv_ref[0], [1.0] * TOP_K)
    cbuf[slot] = _dot_tn(sel, h3_ref[...])
    _strip_copies(c8_ref, lb_ref, gb_ref,
                  lambda lr, gr, rows: pltpu.make_async_copy(cbuf.at[slot, pl.ds(lr, rows), :],
                                                             xs_hbm.at[pl.ds(gr, rows), :], sem.at[slot]))

    def wait_rows(s, rows):
        rows = pl.multiple_of(rows, STRIP)
        pltpu.make_async_copy(cbuf.at[s, pl.ds(0, rows), :], xs_hbm.at[pl.ds(0, rows), :], sem.at[s]).wait()

    @pl.when(i > 0)
    def _():
        wait_rows(1 - slot, tot_ref[jnp.maximum(i - 1, 0)])

    @pl.when(i == n_t - 1)
    def _():
        wait_rows(slot, tot_ref[i])
        zbuf[...] = jnp.zeros_like(zbuf)
        for e in range(N_EXPERTS):
            rows = pl.multiple_of(tailn_ref[e], STRIP)

            @pl.when(rows > 0)
            def _():
                pltpu.make_async_copy(zbuf.at[pl.ds(0, rows), :],
                                      xs_hbm.at[pl.ds(pl.multiple_of(tailo_ref[e], STRIP), rows), :],
                                      sem.at[2]).start()

        for e in range(N_EXPERTS):
            rows = pl.multiple_of(tailn_ref[e], STRIP)

            @pl.when(rows > 0)
            def _():
                pltpu.make_async_copy(zbuf.at[pl.ds(0, rows), :], xs_hbm.at[pl.ds(0, rows), :], sem.at[2]).wait()

        def tile_copy(j):
            return pltpu.make_async_copy(
                zbuf, xs_hbm.at[pl.ds(pl.multiple_of(j * FFN_TILE, FFN_TILE), FFN_TILE), :], sem.at[2])

        def start_tile(j, c):
            tile_copy(j).start()
            return c

        def wait_tile(j, c):
            tile_copy(j).wait()
            return c

        lax.fori_loop(nt_ref[0], n_tiles, start_tile, 0)
        lax.fori_loop(nt_ref[0], n_tiles, wait_tile, 0)


def _dispatch(tile_tot, tail_n, tail_off, n_used, c8, lbase, gbase, h3, ids, lrank, n_tiles):
    n_t = c8.shape[0]
    smem = pl.BlockSpec((1, 1, N_EXPERTS), lambda i, *_: (i, 0, 0), memory_space=pltpu.SMEM)
    row = lambda w: pl.BlockSpec((TOK_TILE, w), lambda i, *_: (i, 0))
    grid_spec = pltpu.PrefetchScalarGridSpec(
        num_scalar_prefetch=4,
        grid=(n_t,),
        in_specs=[smem, smem, smem, row(D_MODEL), row(TOP_K), row(TOP_K),
                  pl.BlockSpec((1, 1, N_EXPERTS), lambda i, *_: (i, 0, 0))],
        out_specs=pl.BlockSpec(memory_space=pl.ANY),
        scratch_shapes=[pltpu.VMEM((2, SLOT_ROWS, D_MODEL), F32), pltpu.VMEM((FFN_TILE, D_MODEL), F32),
                        pltpu.SemaphoreType.DMA((3,))],
    )
    return pl.pallas_call(
        functools.partial(_dispatch_kernel, n_t=n_t, n_tiles=n_tiles),
        grid_spec=grid_spec,
        out_shape=jax.ShapeDtypeStruct((n_tiles * FFN_TILE, D_MODEL), F32),
        compiler_params=_cparams(("arbitrary",)),
        name="dispatch",
    )(tile_tot, tail_n, tail_off, n_used, c8, lbase, gbase, h3, ids, lrank, lbase.astype(F32))


def _ffn_kernel(te_ref, nt_ref, xs_ref, wg_ref, bg_ref, wu_ref, bu_ref, wd_ref, bd_ref, y_ref, wbf):
    i = pl.program_id(0)
    n_used = nt_ref[0]

    @pl.when(jnp.logical_or(i == 0, te_ref[i] != te_ref[jnp.maximum(i - 1, 0)]))
    def _():
        wbf[0] = wg_ref[0].astype(BF16)
        wbf[1] = wu_ref[0].astype(BF16)
        wbf[2] = wd_ref[0].astype(BF16)

    @pl.when(i < n_used)
    def _():
        x = xs_ref[...].astype(BF16)
        gate = jnp.minimum(jnp.dot(x, wbf[0], preferred_element_type=F32) + bg_ref[0], SWIGLU_LIMIT)
        up = jnp.clip(jnp.dot(x, wbf[1], preferred_element_type=F32) + bu_ref[0],
                      -SWIGLU_LIMIT, SWIGLU_LIMIT)
        hid = (up + 1.0) * gate * _sigmoid(gate * SWIGLU_ALPHA)
        y_ref[...] = jnp.dot(hid.astype(BF16), wbf[2], preferred_element_type=F32) + bd_ref[0]

    @pl.when(i >= n_used)
    def _():
        y_ref[...] = jnp.zeros_like(y_ref)


def _ffn(tile_expert, n_used, xs, wg, bg, wu, bu, wd, bd):
    n_tiles = xs.shape[0] // FFN_TILE
    wspec = pl.BlockSpec((1, D_MODEL, D_MODEL), lambda i, te, nt: (te[i], 0, 0))
    bspec = pl.BlockSpec((1, 1, D_MODEL), lambda i, te, nt: (te[i], 0, 0))
    grid_spec = pltpu.PrefetchScalarGridSpec(
        num_scalar_prefetch=2,
        grid=(n_tiles,),
        in_specs=[pl.BlockSpec((FFN_TILE, D_MODEL), lambda i, te, nt: (jnp.minimum(i, nt[0] - 1), 0)),
                  wspec, bspec, wspec, bspec, wspec, bspec],
        out_specs=pl.BlockSpec((FFN_TILE, D_MODEL), lambda i, te, nt: (i, 0)),
        scratch_shapes=[pltpu.VMEM((3, D_MODEL, D_MODEL), BF16)],
    )
    return pl.pallas_call(
        _ffn_kernel,
        grid_spec=grid_spec,
        out_shape=jax.ShapeDtypeStruct((n_tiles * FFN_TILE, D_MODEL), F32),
        compiler_params=_cparams(("arbitrary",)),
        name="ffn",
    )(tile_expert, n_used, xs, wg, bg, wu, bu, wd, bd)


def _combine_kernel(tot_ref, c8_ref, lb_ref, gb_ref, c8n_ref, lbn_ref, gbn_ref,
                    ys_hbm, x2_ref, ids_ref, lrank_ref, prob_ref, lbv_ref, g_ref,
                    op_ref, os_ref, sbuf, sem, *, n_pt, n_t):
    i = pl.program_id(0)
    slot = lax.rem(i, 2)

    def fetch(c8, lb, gb, dst_slot):
        _strip_copies(c8, lb, gb,
                      lambda lr, gr, rows: pltpu.make_async_copy(ys_hbm.at[pl.ds(gr, rows), :],
                                                                 sbuf.at[dst_slot, pl.ds(lr, rows), :],
                                                                 sem.at[dst_slot]))

    @pl.when(i == 0)
    def _():
        sbuf[...] = jnp.zeros_like(sbuf)
        fetch(c8_ref, lb_ref, gb_ref, 0)

    @pl.when(i + 1 < n_t)
    def _():
        fetch(c8n_ref, lbn_ref, gbn_ref, 1 - slot)

    rows = pl.multiple_of(tot_ref[i], STRIP)
    pltpu.make_async_copy(ys_hbm.at[pl.ds(0, rows), :], sbuf.at[slot, pl.ds(0, rows), :], sem.at[slot]).wait()
    selw = _slot_select(ids_ref, lrank_ref, lbv_ref[0], [prob_ref[:, kq:kq + 1] for kq in range(TOP_K)])
    y = jnp.dot(selw, sbuf[slot].astype(BF16), preferred_element_type=F32)
    out = _rms(x2_ref[...] + y, g_ref[...])

    @pl.when(i < n_pt)
    def _():
        op_ref[...] = out

    @pl.when(i >= n_pt)
    def _():
        os_ref[...] = out


def _combine(tile_tot, c8, lbase, gbase, ys, x2, ids, lrank, prob, g_final, n_pt):
    n_t = c8.shape[0]
    last = n_t - 1
    cur = pl.BlockSpec((1, 1, N_EXPERTS), lambda i, *_: (i, 0, 0), memory_space=pltpu.SMEM)
    nxt = pl.BlockSpec((1, 1, N_EXPERTS), lambda i, *_: (jnp.minimum(i + 1, last), 0, 0), memory_space=pltpu.SMEM)
    row = lambda w: pl.BlockSpec((TOK_TILE, w), lambda i, *_: (i, 0))
    grid_spec = pltpu.PrefetchScalarGridSpec(
        num_scalar_prefetch=1,
        grid=(n_t,),
        in_specs=[cur, cur, cur, nxt, nxt, nxt,
                  pl.BlockSpec(memory_space=pl.ANY),
                  row(D_MODEL), row(TOP_K), row(TOP_K), row(TOP_K),
                  pl.BlockSpec((1, 1, N_EXPERTS), lambda i, *_: (i, 0, 0)),
                  pl.BlockSpec((1, D_MODEL), lambda i, *_: (0, 0))],
        out_specs=[pl.BlockSpec((TOK_TILE, D_MODEL), lambda i, *_: (jnp.minimum(i, n_pt - 1), 0)),
                   pl.BlockSpec((TOK_TILE, D_MODEL), lambda i, *_: (jnp.maximum(i - n_pt, 0), 0))],
        scratch_shapes=[pltpu.VMEM((2, SLOT_ROWS, D_MODEL), F32), pltpu.SemaphoreType.DMA((2,))],
    )
    return pl.pallas_call(
        functools.partial(_combine_kernel, n_pt=n_pt, n_t=n_t),
        grid_spec=grid_spec,
        out_shape=[jax.ShapeDtypeStruct((n_pt * TOK_TILE, D_MODEL), F32),
                   jax.ShapeDtypeStruct(((n_t - n_pt) * TOK_TILE, D_MODEL), F32)],
        compiler_params=_cparams(("arbitrary",)),
        name="combine",
    )(tile_tot, c8, lbase, gbase, c8, lbase, gbase, ys, x2, ids, lrank, prob, lbase.astype(F32), g_final)


def kernel(x_prompt, x_sample, mem_prompt, state_shift, state_wkv, state_conv, cache_mem_k, cache_mem_v,
           g_mix, w_in, mu_shift, w0, w_decay_up, a0, w_iclr_up, w_glora_up, k_k, k_a, r_k, gn_g, gn_b,
           dw_w, dw_b, cln_g, cln_b, w_out, g_xattn, g_mem, w_q, w_mk, w_mv, w_o,
           g_moe, w_router, b_router, w_moe_gate, b_moe_gate, w_moe_up, b_moe_up, w_moe_down, b_moe_down,
           g_final):
    bp, tp, _ = x_prompt.shape
    bs, ts, _ = x_sample.shape
    n_p, n_s = bp * tp, bs * ts
    n_all = n_p + n_s
    assert tp % WKV_BLOCK == 0 and tp % TOK_TILE == 0 and n_s == TOK_TILE
    assert ts <= SAMPLE_PAD and bs % SAMPLE_SEQS == 0
    row2 = lambda a: a.reshape(1, -1)

    x_p = x_prompt.reshape(n_p, D_MODEL)
    x_s = x_sample.reshape(n_s, D_MODEL)
    p_rw, u = _inproj(x_p, x_s, row2(g_mix[0]), w_in[0].astype(BF16))

    head_idx = jnp.arange(RW_WIDTH) // RW_HEAD_DIM
    eseg = (head_idx[:, None] == head_idx[None, :]).astype(BF16)
    wts = (row2(mu_shift[0]), row2(w0[0]), w_decay_up[0].astype(BF16), row2(a0[0]),
           w_iclr_up[0].astype(BF16), w_glora_up[0].astype(BF16), row2(k_k[0]), row2(k_a[0]),
           row2(r_k[0]), row2(gn_g[0]), row2(gn_b[0]), eseg)
    n_tp = tp // WKV_BLOCK
    rw_p, wkv_p = _wkv(p_rw, 0, jnp.zeros((bp, 1, RW_PROJ), F32),
                       jnp.zeros((bp, N_GROUPS, RW_HEAD_DIM, GW), F32), wts,
                       batch=bp, n_seq=1, tt_seq=WKV_BLOCK, chunk=WKV_CHUNK, t_valid=tp, n_t=n_tp,
                       out_rows=n_p)
    p_rw_s = p_rw[n_p:].reshape(bs, ts, RW_PROJ)
    p_rw_s_pad = jnp.pad(p_rw_s, ((0, 0), (0, SAMPLE_PAD - ts), (0, 0))).reshape(bs * SAMPLE_PAD, RW_PROJ)
    rw_s_pad, wkv_s = _wkv(p_rw_s_pad, 0, state_shift[0].reshape(bs, 1, RW_PROJ), _to_grouped(state_wkv[0]),
                           wts, batch=bs, n_seq=SAMPLE_SEQS, tt_seq=SAMPLE_PAD, chunk=SAMPLE_PAD,
                           t_valid=ts, n_t=1, out_rows=bs * SAMPLE_PAD)
    rw_s = rw_s_pad.reshape(bs, SAMPLE_PAD, RW_WIDTH)[:, :ts].reshape(n_s, RW_WIDTH)

    conv_w = (dw_w[0], row2(dw_b[0]), row2(cln_g[0]), row2(cln_b[0]))
    cv_p = _conv(u, 0, jnp.zeros((bp, CONV_PAD, CONV_WIDTH), F32), *conv_w,
                 batch=bp, tt=WKV_BLOCK, n_t=n_tp, out_rows=n_p)
    u_s = u[n_p:]
    u_s_pad = jnp.pad(u_s.reshape(bs, ts, CONV_WIDTH), ((0, 0), (0, SAMPLE_PAD - ts), (0, 0)))
    buf_s = jnp.pad(state_conv[0], ((0, 0), (CONV_PAD - (CONV_K - 1), 0), (0, 0)))
    cv_s_pad = _conv(u_s_pad.reshape(bs * SAMPLE_PAD, CONV_WIDTH), 0, buf_s, *conv_w,
                     batch=bs, tt=SAMPLE_PAD, n_t=1, out_rows=bs * SAMPLE_PAD)
    cv_s = cv_s_pad.reshape(bs, SAMPLE_PAD, CONV_WIDTH)[:, :ts].reshape(n_s, CONV_WIDTH)

    x1, q = _mid1(x_p, x_s, rw_p, rw_s, cv_p, cv_s, w_out[0].astype(BF16), row2(g_xattn[0]), w_q[0].astype(BF16))

    mk_p, mv_p = _memkv(mem_prompt.reshape(bp * N_MEM, D_MODEL), row2(g_mem[0]),
                        w_mk[0].astype(BF16), w_mv[0].astype(BF16))
    mk_p = mk_p.reshape(bp, N_MEM, D_MODEL)
    mv_p = mv_p.reshape(bp, N_MEM, D_MODEL)
    o_p = _attn_rows(q, mk_p, mv_p, n_p, tp // TOK_TILE)
    o_s = _attn_seq(q[n_p:].reshape(bs, ts, D_MODEL), cache_mem_k, cache_mem_v).reshape(n_s, D_MODEL)

    tri = (jnp.arange(TOK_TILE)[None, :] < jnp.arange(TOK_TILE)[:, None]).astype(BF16)
    x2, h3, ids, prob, lrank, cnt3, base3, tot = _mid2(x1, o_p, o_s, w_o[0].astype(BF16), row2(g_moe[0]),
                                                       w_router[0], row2(b_router[0]), tri)

    n_t = n_all // TOK_TILE
    cnt = cnt3.astype(I32)
    c8 = ((cnt + STRIP - 1) // STRIP) * STRIP
    lbase = jnp.cumsum(c8, axis=-1) - c8
    tile_tot = jnp.sum(c8, axis=(1, 2))
    used = tot[0].astype(I32)
    padded = ((used + FFN_TILE - 1) // FFN_TILE) * FFN_TILE
    ends = jnp.cumsum(padded)
    offs = ends - padded
    gbase = offs[None, None, :] + base3.astype(I32)
    tail_n = padded - used
    n_tiles = -(-(n_all * TOP_K + n_t * N_EXPERTS * (STRIP - 1)) // FFN_TILE) + N_EXPERTS
    n_used = (ends[-1] // FFN_TILE).reshape(1)
    tile_ids = jnp.minimum(jnp.arange(n_tiles, dtype=I32), n_used - 1)
    tile_expert = jnp.sum((ends[None, :] // FFN_TILE) <= tile_ids[:, None], axis=1).astype(I32)
    tile_expert = jnp.minimum(tile_expert, N_EXPERTS - 1)

    xs = _dispatch(tile_tot, tail_n, offs + used, n_used, c8, lbase, gbase, h3, ids, lrank, n_tiles)
    ys = _ffn(tile_expert, n_used, xs,
              w_moe_gate[0], b_moe_gate[0].reshape(N_EXPERTS, 1, D_MODEL),
              w_moe_up[0], b_moe_up[0].reshape(N_EXPERTS, 1, D_MODEL),
              w_moe_down[0], b_moe_down[0].reshape(N_EXPERTS, 1, D_MODEL))

    y_p, y_s = _combine(tile_tot, c8, lbase, gbase, ys, x2, ids, lrank, prob, row2(g_final), n_p // TOK_TILE)

    last_rows = lambda a, rows: jnp.stack([a[(b + 1) * tp - rows:(b + 1) * tp] for b in range(bp)])
    new_conv_s = jnp.concatenate([state_conv[0], u_s.reshape(bs, ts, CONV_WIDTH)], axis=1)[:, -(CONV_K - 1):]
    kv_shape = (1, bp, N_MEM, X_HEADS, X_HEAD_DIM)
    return (y_p.reshape(bp, tp, D_MODEL), y_s.reshape(bs, ts, D_MODEL),
            last_rows(p_rw, 1).reshape(1, bp, RW_PROJ), _from_grouped(wkv_p)[None],
            last_rows(u, CONV_K - 1)[None],
            mk_p.reshape(kv_shape), mv_p.reshape(kv_shape),
            p_rw_s[:, -1][None], _from_grouped(wkv_s)[None], new_conv_s[None])
```

```python
import functools

import jax
import jax.numpy as jnp
from jax import lax
from jax.experimental import pallas as pl
from jax.experimental.pallas import tpu as pltpu

F32 = jnp.float32
BF16 = jnp.bfloat16
I32 = jnp.int32

D_MODEL = 1024
RW_HEADS = 8
RW_HEAD_DIM = 64
RW_WIDTH = RW_HEADS * RW_HEAD_DIM
DECAY_LORA = 64
ICLR_LORA = 64
GATE_LORA = 128
RW_PROJ = 3 * RW_WIDTH + DECAY_LORA + ICLR_LORA + GATE_LORA
GN_EPS = 64e-5
CONV_WIDTH = D_MODEL - RW_WIDTH
CONV_K = 31
LN_EPS = 1e-5
IN_PROJ = RW_PROJ + 2 * CONV_WIDTH
N_MEM = 256
X_HEADS = 4
X_HEAD_DIM = D_MODEL // X_HEADS
ATTN_SCALE = X_HEAD_DIM ** -0.5
N_EXPERTS = 32
TOP_K = 4
SWIGLU_LIMIT = 7.0
SWIGLU_ALPHA = 1.702
NORM_EPS = 1e-5

TOK_TILE = 512
WKV_BLOCK = 256
WKV_CHUNK = 64
SAMPLE_PAD = 8
SAMPLE_SEQS = 4
ATTN_SEQS = 4
GROUP = 4
GW = GROUP * RW_HEAD_DIM
N_GROUPS = RW_HEADS // GROUP
FFN_TILE = 512
STRIP = 8
SLOT_ROWS = TOP_K * TOK_TILE + N_EXPERTS * STRIP
CONV_PAD = 32
VMEM_LIMIT = 56 * 1024 * 1024


def _cparams(sem):
    return pltpu.CompilerParams(dimension_semantics=sem, vmem_limit_bytes=VMEM_LIMIT)


def _dot(a, b):
    return jnp.dot(a.astype(BF16), b.astype(BF16), preferred_element_type=F32)


def _dot_nt(a, b):
    return lax.dot_general(a.astype(BF16), b.astype(BF16), (((1,), (1,)), ((), ())),
                           preferred_element_type=F32)


def _dot_tn(a, b):
    return lax.dot_general(a.astype(BF16), b.astype(BF16), (((0,), (0,)), ((), ())),
                           preferred_element_type=F32)


def _split2(x):
    hi = x.astype(BF16)
    lo = (x - hi.astype(F32)).astype(BF16)
    return hi, lo


def _split3(x):
    hi = x.astype(BF16)
    r1 = x - hi.astype(F32)
    mid = r1.astype(BF16)
    lo = (r1 - mid.astype(F32)).astype(BF16)
    return hi, mid, lo


def _segsum(x, eseg):
    hi, lo = _split2(x)
    return (jnp.dot(hi, eseg, preferred_element_type=F32)
            + jnp.dot(lo, eseg, preferred_element_type=F32))


def _rms(x, g):
    return x * lax.rsqrt(jnp.mean(x * x, axis=-1, keepdims=True) + NORM_EPS) * g


def _sigmoid(x):
    return 1.0 / (1.0 + jnp.exp(-x))


def _pick(first, a_ref, b_ref):
    return jnp.where(first, a_ref[...], b_ref[...])


def _two_source_specs(width, n_pt):
    return (pl.BlockSpec((TOK_TILE, width), lambda i: (jnp.minimum(i, n_pt - 1), 0)),
            pl.BlockSpec((TOK_TILE, width), lambda i: (jnp.maximum(i - n_pt, 0), 0)))


def _inproj_kernel(xp_ref, xs_ref, g_ref, w_ref, prw_ref, u_ref, *, n_pt):
    h = _rms(_pick(pl.program_id(0) < n_pt, xp_ref, xs_ref), g_ref[...])
    p = _dot(h, w_ref[...])
    prw_ref[...] = p[:, :RW_PROJ]
    u_ref[...] = p[:, RW_PROJ:RW_PROJ + CONV_WIDTH] * _sigmoid(p[:, RW_PROJ + CONV_WIDTH:])


def _inproj(x_p, x_s, g_mix, w_in_bf):
    n = x_p.shape[0] + x_s.shape[0]
    n_pt = x_p.shape[0] // TOK_TILE
    return pl.pallas_call(
        functools.partial(_inproj_kernel, n_pt=n_pt),
        grid=(n // TOK_TILE,),
        in_specs=[*_two_source_specs(D_MODEL, n_pt),
                  pl.BlockSpec((1, D_MODEL), lambda i: (0, 0)),
                  pl.BlockSpec((D_MODEL, IN_PROJ), lambda i: (0, 0))],
        out_specs=[pl.BlockSpec((TOK_TILE, RW_PROJ), lambda i: (i, 0)),
                   pl.BlockSpec((TOK_TILE, CONV_WIDTH), lambda i: (i, 0))],
        out_shape=[jax.ShapeDtypeStruct((n, RW_PROJ), F32),
                   jax.ShapeDtypeStruct((n, CONV_WIDTH), F32)],
        compiler_params=_cparams(("parallel",)),
        name="inproj",
    )(x_p, x_s, g_mix, w_in_bf)


def _bd(x, w):
    r, lanes = x.shape
    head = lax.broadcasted_iota(I32, (r, lanes), 1) >> (w.bit_length() - 1)
    return jnp.concatenate([jnp.where(head == h, x, 0.0) for h in range(GROUP)], axis=0).astype(BF16)


def _collapse(x):
    head = lax.broadcasted_iota(I32, (RW_HEAD_DIM, GW), 1) >> (RW_HEAD_DIM.bit_length() - 1)
    out = jnp.zeros((RW_HEAD_DIM, GW), F32)
    for h in range(GROUP):
        out = out + jnp.where(head == h, x[h * RW_HEAD_DIM:(h + 1) * RW_HEAD_DIM, :], 0.0)
    return out


def _conv_module(t, u_ref, hist_ref, cw_ref, cb_ref, cg_ref, cbeta_ref, cv_ref, ext_ref, sh_ref,
                 *, n_seq, tt_seq, n_t):
    first = CONV_PAD - (CONV_K - 1)
    span = tt_seq + CONV_PAD - 8
    for s in range(n_seq):
        rows = slice(s * tt_seq, (s + 1) * tt_seq)

        @pl.when(t == 0)
        def _():
            ext_ref[s, 0:CONV_PAD, :] = hist_ref[s]

        ext_ref[s, CONV_PAD:CONV_PAD + tt_seq, :] = u_ref[rows, :]
        for b in range(1, 8):
            sh_ref[b, 0:span, :] = ext_ref[s, pl.ds(b, span), :]
        acc = jnp.zeros((tt_seq, CONV_WIDTH), F32) + cb_ref[...]
        for j in range(CONV_K):
            a8, b = divmod(first + j, 8)
            win = ext_ref[s, pl.ds(8 * a8, tt_seq), :] if b == 0 else sh_ref[b, pl.ds(8 * a8, tt_seq), :]
            acc = acc + win * cw_ref[j:j + 1, :]
        mean = jnp.mean(acc, axis=-1, keepdims=True)
        cen = acc - mean
        var = jnp.mean(cen * cen, axis=-1, keepdims=True)
        cn = cen * lax.rsqrt(var + LN_EPS) * cg_ref[...] + cbeta_ref[...]
        cv_ref[rows, :] = cn * _sigmoid(cn)
        if n_t > 1:
            ext_ref[s, 0:CONV_PAD, :] = ext_ref[s, tt_seq:tt_seq + CONV_PAD, :]


def _wkv_kernel(p_ref, prev_ref, s0_ref, mu_ref, w0_ref, wdup_ref, a0_ref, aup_ref, gup_ref,
                kk_ref, ka_ref, rk_ref, gng_ref, gnb_ref, eseg_ref,
                u_ref, hist_ref, cw_ref, cb_ref, cg_ref, cbeta_ref,
                out_ref, sfin_ref, cv_ref,
                s_ref, last_ref, rt_ref, at_ref, bt_ref, kt_ref, v_ref, pc_ref, y_ref,
                r_ref, km_ref, g_ref, ext_ref, sh_ref, *, n_seq, tt_seq, chunk, t_valid, n_t):
    t = pl.program_id(1)
    tt = n_seq * tt_seq
    n_chunk = tt_seq // chunk
    n_fused = chunk.bit_length() - 2
    nd = RW_HEAD_DIM
    c1, c2, c3 = RW_WIDTH, 2 * RW_WIDTH, 3 * RW_WIDTH
    c4 = c3 + DECAY_LORA
    c5 = c4 + ICLR_LORA

    @pl.when(t == 0)
    def _():
        s_ref[...] = s0_ref[...]
        if n_seq == 1:
            last_ref[0:1, :] = prev_ref[0]

    _conv_module(t, u_ref, hist_ref, cw_ref, cb_ref, cg_ref, cbeta_ref, cv_ref, ext_ref, sh_ref,
                 n_seq=n_seq, tt_seq=tt_seq, n_t=n_t)

    p = p_ref[...]
    row = lax.broadcasted_iota(I32, (tt, 1), 0)
    seq_row = row & (tt_seq - 1)
    rolled = pltpu.roll(p, shift=1, axis=0)
    if n_seq == 1:
        ps = jnp.where(seq_row == 0, last_ref[0:1, :], rolled)
        last_ref[0:1, :] = p[tt - 1:tt, :]
    else:
        prev_rows = jnp.concatenate(
            [jnp.broadcast_to(prev_ref[q], (tt_seq, RW_PROJ)) for q in range(n_seq)], axis=0)
        ps = jnp.where(seq_row == 0, prev_rows, rolled)
    z = p + mu_ref[...] * (ps - p)
    r, k, v = z[:, :c1], z[:, c1:c2], z[:, c2:c3]
    wd, ad, gd = z[:, c3:c4], z[:, c4:c5], z[:, c5:]
    wpre = w0_ref[...] + _dot(jnp.tanh(wd), wdup_ref[...])
    neg = -wpre
    w = -(jnp.maximum(neg, 0.0) + jnp.log(1.0 + jnp.exp(-jnp.abs(neg)))) - 0.5
    a = _sigmoid(a0_ref[...] + _dot(ad, aup_ref[...]))
    g_ref[...] = _dot(_sigmoid(gd), gup_ref[...])
    eseg = eseg_ref[...]
    kk = k * kk_ref[...]
    kk = kk / jnp.maximum(jnp.sqrt(_segsum(kk * kk, eseg)), 1e-12)
    kmod = k * (1.0 + (a - 1.0) * ka_ref[...])
    logd = -jnp.exp(w)
    if t_valid < tt_seq * n_t:
        valid = (t * tt_seq + seq_row) < t_valid
        logd = jnp.where(valid, logd, 0.0)
        kk = jnp.where(valid, kk, 0.0)
        kmod = jnp.where(valid, kmod, 0.0)
        v = jnp.where(valid, v, 0.0)
    ri = lax.broadcasted_iota(I32, (tt, tt), 0)
    ci = lax.broadcasted_iota(I32, (tt, tt), 1)
    shift = chunk.bit_length() - 1
    ltri = jnp.where(ci <= ri, jnp.where((ri >> shift) == (ci >> shift), 1.0, 0.0), 0.0).astype(BF16)
    hi, mid, lo = _split3(logd)
    logp = (jnp.dot(ltri, hi, preferred_element_type=F32)
            + jnp.dot(ltri, mid, preferred_element_type=F32)
            + jnp.dot(ltri, lo, preferred_element_type=F32))
    pcum = jnp.exp(logp)
    pinv = jnp.exp(-logp)
    r_ref[...] = r
    km_ref[...] = kmod
    v_ref[...] = v
    pc_ref[...] = pcum
    rt_ref[...] = r * pcum
    at_ref[...] = -kk * jnp.exp(logp - logd)
    bt_ref[...] = kk * a * pinv
    kt_ref[...] = kmod * pinv

    cw = GROUP * chunk
    ti = lax.broadcasted_iota(I32, (chunk, cw), 0)
    si = lax.broadcasted_iota(I32, (chunk, cw), 1) & (chunk - 1)
    strict = si < ti
    incl = si <= ti
    eye_c = jnp.where(si == ti, 1.0, 0.0)
    ji = lax.broadcasted_iota(I32, (nd, GW), 0)
    jj = lax.broadcasted_iota(I32, (nd, GW), 1) & (nd - 1)
    eye_n = jnp.where(ji == jj, 1.0, 0.0)

    chains = [(s, c, g) for s in range(n_seq) for c in range(n_chunk) for g in range(N_GROUPS)]

    def blk(ref, s, c, g):
        r0 = s * tt_seq + c * chunk
        return ref[r0:r0 + chunk, g * GW:(g + 1) * GW]

    am = {q: blk(at_ref, *q) for q in chains}
    rm = {q: blk(rt_ref, *q) for q in chains}
    bm = {q: blk(bt_ref, *q) for q in chains}
    km = {q: blk(kt_ref, *q) for q in chains}
    vm = {q: blk(v_ref, *q) for q in chains}
    a_ab, a_ak, a_rb, a_rk = {}, {}, {}, {}
    for q in chains:
        ar = jnp.concatenate([am[q], rm[q]], axis=0)
        s1 = _dot_nt(ar, _bd(bm[q], nd))
        s2 = _dot_nt(ar, _bd(km[q], nd))
        a_ab[q] = jnp.where(strict, s1[:chunk], 0.0)
        a_rb[q] = jnp.where(incl, s1[chunk:], 0.0)
        a_ak[q] = jnp.where(strict, s2[:chunk], 0.0)
        a_rk[q] = jnp.where(incl, s2[chunk:], 0.0)
    tinv = {q: eye_c + a_ab[q] for q in chains}
    pw = {q: _dot(a_ab[q], _bd(a_ab[q], chunk)) for q in chains}
    for lvl in range(n_fused):
        last = lvl == n_fused - 1
        for q in chains:
            lhs = tinv[q] if last else jnp.concatenate([tinv[q], pw[q]], axis=0)
            zz = _dot(lhs, _bd(pw[q], chunk))
            tinv[q] = tinv[q] + zz[:chunk]
            if not last:
                pw[q] = zz[chunk:]
    akv, rkv = {}, {}
    for q in chains:
        zv = _dot(jnp.concatenate([a_ak[q], a_rk[q]], axis=0), _bd(vm[q], nd))
        akv[q], rkv[q] = zv[:chunk], zv[chunk:]
    ap = {q: _dot(tinv[q], _bd(am[q], nd)) for q in chains}
    uv = {q: _dot(tinv[q], _bd(akv[q], nd)) for q in chains}
    rp = {q: rm[q] + _dot(a_rb[q], _bd(ap[q], nd)) for q in chains}
    yv = {q: _dot(a_rb[q], _bd(uv[q], nd)) + rkv[q] for q in chains}
    mm, gm = {}, {}
    for q in chains:
        s, c, g = q
        r_end = s * tt_seq + (c + 1) * chunk
        pc = pc_ref[r_end - 8:r_end, g * GW:(g + 1) * GW][7:8, :]
        cross = _dot_tn(jnp.concatenate([ap[q], uv[q]], axis=1), bm[q])
        vk = _dot_tn(vm[q], km[q])
        mm[q] = (eye_n + _collapse(cross[:GW])) * pc
        gm[q] = (_collapse(cross[GW:]) + _collapse(vk)) * pc

    for s in range(n_seq):
        for g in range(N_GROUPS):
            st = s_ref[s, g]
            for c in range(n_chunk):
                q = (s, c, g)
                r0 = s * tt_seq + c * chunk
                y_ref[r0:r0 + chunk, g * GW:(g + 1) * GW] = _dot_nt(rp[q], _bd(st, nd)) + yv[q]
                st = _dot(st, _bd(mm[q], nd)) + gm[q]
            s_ref[s, g] = st

    y = y_ref[...]
    inv_n = 1.0 / RW_HEAD_DIM
    mean = _segsum(y, eseg) * inv_n
    yc = y - mean
    var = _segsum(yc * yc, eseg) * inv_n
    yn = yc * lax.rsqrt(var + GN_EPS) * gng_ref[...] + gnb_ref[...]
    bonus = _segsum(r_ref[...] * km_ref[...] * rk_ref[...], eseg) * v_ref[...]
    out_ref[...] = (yn + bonus) * g_ref[...]

    @pl.when(t == n_t - 1)
    def _():
        sfin_ref[...] = s_ref[...]


def _mixer(p_rows, u_rows, row_block0, prev, s0, hist, wts, conv_w, *, batch, n_seq, tt_seq, chunk, t_valid,
           n_t, out_rows):
    (mu, w0, wdup, a0, aup, gup, k_k, k_a, r_k, gn_g, gn_b, eseg) = wts
    dw_w, dw_b, ln_g, ln_b = conv_w
    tt = n_seq * tt_seq
    kern = functools.partial(_wkv_kernel, n_seq=n_seq, tt_seq=tt_seq, chunk=chunk, t_valid=t_valid, n_t=n_t)
    const = lambda shape: pl.BlockSpec(shape, lambda b, t: tuple(0 for _ in shape))
    row_out = lambda w: pl.BlockSpec((tt, w), lambda b, t: (b * n_t + t, 0))
    blk = pltpu.VMEM((tt, RW_WIDTH), F32)
    st_spec = pl.BlockSpec((n_seq, N_GROUPS, RW_HEAD_DIM, GW), lambda b, t: (b, 0, 0, 0))
    return pl.pallas_call(
        kern,
        grid=(batch // n_seq, n_t),
        in_specs=[pl.BlockSpec((tt, RW_PROJ), lambda b, t: (row_block0 + b * n_t + t, 0)),
                  pl.BlockSpec((n_seq, 1, RW_PROJ), lambda b, t: (b, 0, 0)),
                  st_spec,
                  const((1, RW_PROJ)), const((1, RW_WIDTH)), const((DECAY_LORA, RW_WIDTH)),
                  const((1, RW_WIDTH)), const((ICLR_LORA, RW_WIDTH)), const((GATE_LORA, RW_WIDTH)),
                  const((1, RW_WIDTH)), const((1, RW_WIDTH)), const((1, RW_WIDTH)),
                  const((1, RW_WIDTH)), const((1, RW_WIDTH)), const((RW_WIDTH, RW_WIDTH)),
                  pl.BlockSpec((tt, CONV_WIDTH), lambda b, t: (row_block0 + b * n_t + t, 0)),
                  pl.BlockSpec((n_seq, CONV_PAD, CONV_WIDTH), lambda b, t: (b, 0, 0)),
                  const((CONV_K, CONV_WIDTH)), const((1, CONV_WIDTH)),
                  const((1, CONV_WIDTH)), const((1, CONV_WIDTH))],
        out_specs=[row_out(RW_WIDTH), st_spec, row_out(CONV_WIDTH)],
        out_shape=[jax.ShapeDtypeStruct((out_rows, RW_WIDTH), F32),
                   jax.ShapeDtypeStruct((batch, N_GROUPS, RW_HEAD_DIM, GW), F32),
                   jax.ShapeDtypeStruct((out_rows, CONV_WIDTH), F32)],
        scratch_shapes=[pltpu.VMEM((n_seq, N_GROUPS, RW_HEAD_DIM, GW), F32),
                        pltpu.VMEM((8, RW_PROJ), F32),
                        blk, blk, blk, blk, blk, blk, blk, blk, blk, blk,
                        pltpu.VMEM((n_seq, CONV_PAD + tt_seq, CONV_WIDTH), F32),
                        pltpu.VMEM((8, CONV_PAD + tt_seq, CONV_WIDTH), F32)],
        compiler_params=_cparams(("parallel", "arbitrary")),
        name="mixer",
    )(p_rows, prev, s0, mu, w0, wdup, a0, aup, gup, k_k, k_a, r_k, gn_g, gn_b, eseg,
      u_rows, hist, dw_w, dw_b, ln_g, ln_b)


def _to_grouped(state):
    b = state.shape[0]
    return (state.reshape(b, N_GROUPS, GROUP, RW_HEAD_DIM, RW_HEAD_DIM).transpose(0, 1, 3, 2, 4)
            .reshape(b, N_GROUPS, RW_HEAD_DIM, GW))


def _from_grouped(state):
    b = state.shape[0]
    return (state.reshape(b, N_GROUPS, RW_HEAD_DIM, GROUP, RW_HEAD_DIM).transpose(0, 1, 3, 2, 4)
            .reshape(b, RW_HEADS, RW_HEAD_DIM, RW_HEAD_DIM))


def _mid1_kernel(xp_ref, xs_ref, rwp_ref, rws_ref, cvp_ref, cvs_ref, wo_ref, g_ref, wq_ref, x1_ref, q_ref,
                 *, n_pt):
    first = pl.program_id(0) < n_pt
    rw = _pick(first, rwp_ref, rws_ref)
    cv = _pick(first, cvp_ref, cvs_ref)
    mix = _dot(rw, wo_ref[:RW_WIDTH, :]) + _dot(cv, wo_ref[RW_WIDTH:, :])
    x1 = _pick(first, xp_ref, xs_ref) + mix
    x1_ref[...] = x1
    q = _dot(_rms(x1, g_ref[...]), wq_ref[...])
    q_ref[...] = (q * ATTN_SCALE).astype(BF16)


def _mid1(x_p, x_s, rw_p, rw_s, cv_p, cv_s, w_out_bf, g_x, w_q_bf):
    n = x_p.shape[0] + x_s.shape[0]
    n_pt = rw_p.shape[0] // TOK_TILE
    row = lambda w: pl.BlockSpec((TOK_TILE, w), lambda i: (i, 0))
    full = lambda a, b: pl.BlockSpec((a, b), lambda i: (0, 0))
    return pl.pallas_call(
        functools.partial(_mid1_kernel, n_pt=n_pt),
        grid=(n // TOK_TILE,),
        in_specs=[*_two_source_specs(D_MODEL, n_pt), *_two_source_specs(RW_WIDTH, n_pt),
                  *_two_source_specs(CONV_WIDTH, n_pt),
                  full(D_MODEL, D_MODEL), full(1, D_MODEL), full(D_MODEL, D_MODEL)],
        out_specs=[row(D_MODEL), row(D_MODEL)],
        out_shape=[jax.ShapeDtypeStruct((n, D_MODEL), F32), jax.ShapeDtypeStruct((n, D_MODEL), BF16)],
        compiler_params=_cparams(("parallel",)),
        name="mid1",
    )(x_p, x_s, rw_p, rw_s, cv_p, cv_s, w_out_bf, g_x, w_q_bf)


def _memkv_kernel(m_ref, g_ref, wk_ref, wv_ref, k_ref, v_ref):
    m = _rms(m_ref[...], g_ref[...])
    k_ref[...] = _dot(m, wk_ref[...])
    v_ref[...] = _dot(m, wv_ref[...])


def _memkv(mem, g_mem, w_mk_bf, w_mv_bf):
    n = mem.shape[0]
    tile = N_MEM
    row = pl.BlockSpec((tile, D_MODEL), lambda i: (i, 0))
    full = lambda a, b: pl.BlockSpec((a, b), lambda i: (0, 0))
    return pl.pallas_call(
        _memkv_kernel,
        grid=(n // tile,),
        in_specs=[row, full(1, D_MODEL), full(D_MODEL, D_MODEL), full(D_MODEL, D_MODEL)],
        out_specs=[row, row],
        out_shape=[jax.ShapeDtypeStruct((n, D_MODEL), F32), jax.ShapeDtypeStruct((n, D_MODEL), F32)],
        compiler_params=_cparams(("parallel",)),
        name="memkv",
    )(mem, g_mem, w_mk_bf, w_mv_bf)


def _attn_head(q_h, k_h, v_h):
    s = _dot_nt(q_h, k_h)
    e = jnp.exp(s - jnp.max(s, axis=-1, keepdims=True))
    pr = e / jnp.sum(e, axis=-1, keepdims=True)
    return _dot(pr, v_h).astype(BF16)


def _attn_rows_kernel(q_ref, k_ref, v_ref, o_ref):
    q = q_ref[...]
    for h in range(X_HEADS):
        hs = slice(h * X_HEAD_DIM, (h + 1) * X_HEAD_DIM)
        o_ref[:, hs] = _attn_head(q[:, hs], k_ref[0, :, hs], v_ref[0, :, hs])


def _attn_seq_kernel(q_ref, k_hbm, v_hbm, o_ref, kbuf, vbuf, sem, *, n_steps):
    i = pl.program_id(0)
    slot = lax.rem(i, 2)

    def copies(step, s):
        out = []
        for g in range(ATTN_SEQS):
            seq = step * ATTN_SEQS + g
            for h in range(X_HEADS):
                lanes = pl.ds(h * X_HEAD_DIM, X_HEAD_DIM)
                out.append(pltpu.make_async_copy(k_hbm.at[0, seq, :, h, :], kbuf.at[s, g, :, lanes], sem.at[s]))
                out.append(pltpu.make_async_copy(v_hbm.at[0, seq, :, h, :], vbuf.at[s, g, :, lanes], sem.at[s]))
        return out

    @pl.when(i == 0)
    def _():
        for cp in copies(0, 0):
            cp.start()

    @pl.when(i + 1 < n_steps)
    def _():
        for cp in copies(i + 1, 1 - slot):
            cp.start()

    for cp in copies(i, slot):
        cp.wait()
    head_of_lane = lax.broadcasted_iota(I32, (SAMPLE_PAD * X_HEADS, D_MODEL), 1) >> (X_HEAD_DIM.bit_length() - 1)
    head_of_row = lax.broadcasted_iota(I32, (SAMPLE_PAD * X_HEADS, D_MODEL), 0) >> (SAMPLE_PAD.bit_length() - 1)
    own = head_of_lane == head_of_row
    for g in range(ATTN_SEQS):
        q8 = q_ref[g].astype(F32)
        qbd = jnp.where(own, jnp.concatenate([q8] * X_HEADS, axis=0), 0.0)
        s = _dot_nt(qbd, kbuf[slot, g])
        e = jnp.exp(s - jnp.max(s, axis=-1, keepdims=True))
        pr = e / jnp.sum(e, axis=-1, keepdims=True)
        o_all = jnp.where(own, _dot(pr, vbuf[slot, g]), 0.0)
        o8 = o_all[0:SAMPLE_PAD]
        for h in range(1, X_HEADS):
            o8 = o8 + o_all[SAMPLE_PAD * h:SAMPLE_PAD * (h + 1)]
        o_ref[g] = o8.astype(BF16)


def _attn_rows(q, mk, mv, n_rows, tiles_per_seq):
    qspec = pl.BlockSpec((TOK_TILE, D_MODEL), lambda i, j: (i * tiles_per_seq + j, 0))
    kvspec = pl.BlockSpec((1, N_MEM, D_MODEL), lambda i, j: (i, 0, 0))
    return pl.pallas_call(
        _attn_rows_kernel,
        grid=(mk.shape[0], tiles_per_seq),
        in_specs=[qspec, kvspec, kvspec],
        out_specs=qspec,
        out_shape=jax.ShapeDtypeStruct((n_rows, D_MODEL), BF16),
        compiler_params=_cparams(("parallel", "parallel")),
        name="attn",
    )(q, mk, mv)


def _attn_seq(q3, mk, mv):
    b, t, _ = q3.shape
    qspec = pl.BlockSpec((ATTN_SEQS, t, D_MODEL), lambda i: (i, 0, 0))
    anyspec = pl.BlockSpec(memory_space=pl.ANY)
    head_buf = pltpu.VMEM((2, ATTN_SEQS, N_MEM, D_MODEL), F32)
    return pl.pallas_call(
        functools.partial(_attn_seq_kernel, n_steps=b // ATTN_SEQS),
        grid=(b // ATTN_SEQS,),
        in_specs=[qspec, anyspec, anyspec],
        out_specs=qspec,
        out_shape=jax.ShapeDtypeStruct((b, t, D_MODEL), BF16),
        scratch_shapes=[head_buf, head_buf, pltpu.SemaphoreType.DMA((2,))],
        compiler_params=_cparams(("arbitrary",)),
        name="attn",
    )(q3, mk, mv)


def _mid2_kernel(x1_ref, op_ref, os_ref, wo_ref, g_ref, wr_ref, br_ref, tri_ref,
                 x2_ref, h3_ref, ids_ref, prob_ref, lrank_ref, cnt_ref, base_ref, tot_ref, carry_ref, *, n_pt):
    i = pl.program_id(0)

    @pl.when(i == 0)
    def _():
        carry_ref[...] = jnp.zeros_like(carry_ref)

    x2 = x1_ref[...] + _dot(_pick(i < n_pt, op_ref, os_ref), wo_ref[...])
    x2_ref[...] = x2
    h3 = _rms(x2, g_ref[...])
    h3_ref[...] = h3.astype(BF16)
    h3_hi, h3_lo = _split2(h3)
    wr_hi, wr_lo = _split2(wr_ref[...])
    logits = (jnp.dot(h3_hi, wr_hi, preferred_element_type=F32)
              + jnp.dot(h3_hi, wr_lo, preferred_element_type=F32)
              + jnp.dot(h3_lo, wr_hi, preferred_element_type=F32)) + br_ref[...]
    n = logits.shape[0]
    lane = lax.broadcasted_iota(I32, (n, N_EXPERTS), 1)
    work = logits
    vals, ids = [], []
    for _ in range(TOP_K):
        m = jnp.max(work, axis=-1, keepdims=True)
        idx = jnp.min(jnp.where(work == m, lane, N_EXPERTS), axis=-1, keepdims=True)
        vals.append(m)
        ids.append(idx)
        work = jnp.where(lane == idx, -jnp.inf, work)
    exps = [jnp.exp(vk - vals[0]) for vk in vals]
    den = exps[0] + exps[1] + exps[2] + exps[3]
    mask = jnp.zeros((n, N_EXPERTS), F32)
    for idx in ids:
        mask = mask + jnp.where(lane == idx, 1.0, 0.0)
    lrank = jnp.dot(tri_ref[...], mask.astype(BF16), preferred_element_type=F32)
    cnt = jnp.sum(mask, axis=0, keepdims=True)
    cnt_ref[0] = cnt
    base_ref[0] = carry_ref[0:1, :]
    carry_ref[0:1, :] = carry_ref[0:1, :] + jnp.floor((cnt + (STRIP - 1)) * (1.0 / STRIP)) * STRIP
    tot_ref[...] = carry_ref[0:1, :]
    for kq in range(TOP_K):
        ids_ref[:, kq:kq + 1] = ids[kq]
        prob_ref[:, kq:kq + 1] = exps[kq] / den
        lrank_ref[:, kq:kq + 1] = jnp.sum(jnp.where(lane == ids[kq], lrank, 0.0), axis=-1, keepdims=True)


def _mid2(x1, o_p, o_s, w_o_bf, g_moe, w_router, b_router, tri):
    n = x1.shape[0]
    n_t = n // TOK_TILE
    n_pt = o_p.shape[0] // TOK_TILE
    row = lambda w: pl.BlockSpec((TOK_TILE, w), lambda i: (i, 0))
    full = lambda a, b: pl.BlockSpec((a, b), lambda i: (0, 0))
    per_tile = pl.BlockSpec((1, 1, N_EXPERTS), lambda i: (i, 0, 0))
    return pl.pallas_call(
        functools.partial(_mid2_kernel, n_pt=n_pt),
        grid=(n_t,),
        in_specs=[row(D_MODEL), *_two_source_specs(D_MODEL, n_pt), full(D_MODEL, D_MODEL), full(1, D_MODEL),
                  full(D_MODEL, N_EXPERTS), full(1, N_EXPERTS), full(TOK_TILE, TOK_TILE)],
        out_specs=[row(D_MODEL), row(D_MODEL), row(TOP_K), row(TOP_K), row(TOP_K), per_tile, per_tile,
                   full(1, N_EXPERTS)],
        out_shape=[jax.ShapeDtypeStruct((n, D_MODEL), F32), jax.ShapeDtypeStruct((n, D_MODEL), BF16),
                   jax.ShapeDtypeStruct((n, TOP_K), I32), jax.ShapeDtypeStruct((n, TOP_K), F32),
                   jax.ShapeDtypeStruct((n, TOP_K), F32),
                   jax.ShapeDtypeStruct((n_t, 1, N_EXPERTS), F32), jax.ShapeDtypeStruct((n_t, 1, N_EXPERTS), F32),
                   jax.ShapeDtypeStruct((1, N_EXPERTS), F32)],
        scratch_shapes=[pltpu.VMEM((8, N_EXPERTS), F32)],
        compiler_params=_cparams(("arbitrary",)),
        name="mid2",
    )(x1, o_p, o_s, w_o_bf, g_moe, w_router, b_router, tri)


def _slot_select(ids_ref, lrank_ref, lbase_v, weights):
    n = ids_ref.shape[0]
    lane = lax.broadcasted_iota(I32, (n, N_EXPERTS), 1)
    col = lax.broadcasted_iota(I32, (n, SLOT_ROWS), 1)
    sel = jnp.zeros((n, SLOT_ROWS), F32)
    for kq in range(TOP_K):
        base = jnp.sum(jnp.where(lane == ids_ref[:, kq:kq + 1], lbase_v, 0.0), axis=-1, keepdims=True)
        dest = (base + lrank_ref[:, kq:kq + 1]).astype(I32)
        sel = jnp.where(col == dest, weights[kq], sel)
    return sel.astype(BF16)


def _strip_copies(c8_ref, lb_ref, gb_ref, make):
    for e in range(N_EXPERTS):
        rows = c8_ref[0, 0, e]

        @pl.when(rows > 0)
        def _():
            make(pl.multiple_of(lb_ref[0, 0, e], STRIP), pl.multiple_of(gb_ref[0, 0, e], STRIP),
                 pl.multiple_of(rows, STRIP)).start()


def _dispatch_kernel(tot_ref, tailn_ref, tailo_ref, nt_ref,
                     c8_ref, lb_ref, gb_ref, h3_ref, ids_ref, lrank_ref, lbv_ref,
                     xs_hbm, cbuf, zbuf, sem, *, n_t, n_tiles):
    i = pl.program_id(0)
    slot = lax.rem(i, 2)
    sel = _slot_select(ids_ref, lrank_ref, lbv_ref[0], [1.0] * TOP_K)
    cbuf[slot] = _dot_tn(sel, h3_ref[...])
    _strip_copies(c8_ref, lb_ref, gb_ref,
                  lambda lr, gr, rows: pltpu.make_async_copy(cbuf.at[slot, pl.ds(lr, rows), :],
                                                             xs_hbm.at[pl.ds(gr, rows), :], sem.at[slot]))

    def wait_rows(s, rows):
        rows = pl.multiple_of(rows, STRIP)
        pltpu.make_async_copy(cbuf.at[s, pl.ds(0, rows), :], xs_hbm.at[pl.ds(0, rows), :], sem.at[s]).wait()

    @pl.when(i > 0)
    def _():
        wait_rows(1 - slot, tot_ref[jnp.maximum(i - 1, 0)])

    @pl.when(i == n_t - 1)
    def _():
        wait_rows(slot, tot_ref[i])
        zbuf[...] = jnp.zeros_like(zbuf)
        for e in range(N_EXPERTS):
            rows = pl.multiple_of(tailn_ref[e], STRIP)

            @pl.when(rows > 0)
            def _():
                pltpu.make_async_copy(zbuf.at[pl.ds(0, rows), :],
                                      xs_hbm.at[pl.ds(pl.multiple_of(tailo_ref[e], STRIP), rows), :],
                                      sem.at[2]).start()

        for e in range(N_EXPERTS):
            rows = pl.multiple_of(tailn_ref[e], STRIP)

            @pl.when(rows > 0)
            def _():
                pltpu.make_async_copy(zbuf.at[pl.ds(0, rows), :], xs_hbm.at[pl.ds(0, rows), :], sem.at[2]).wait()

        def tile_copy(j):
            return pltpu.make_async_copy(
                zbuf, xs_hbm.at[pl.ds(pl.multiple_of(j * FFN_TILE, FFN_TILE), FFN_TILE), :], sem.at[2])

        def start_tile(j, c):
            tile_copy(j).start()
            return c

        def wait_tile(j, c):
            tile_copy(j).wait()
            return c

        lax.fori_loop(nt_ref[0], n_tiles, start_tile, 0)
        lax.fori_loop(nt_ref[0], n_tiles, wait_tile, 0)


def _dispatch(tile_tot, tail_n, tail_off, n_used, c8, lbase, gbase, h3, ids, lrank, n_tiles):
    n_t = c8.shape[0]
    smem = pl.BlockSpec((1, 1, N_EXPERTS), lambda i, *_: (i, 0, 0), memory_space=pltpu.SMEM)
    row = lambda w: pl.BlockSpec((TOK_TILE, w), lambda i, *_: (i, 0))
    grid_spec = pltpu.PrefetchScalarGridSpec(
        num_scalar_prefetch=4,
        grid=(n_t,),
        in_specs=[smem, smem, smem, row(D_MODEL), row(TOP_K), row(TOP_K),
                  pl.BlockSpec((1, 1, N_EXPERTS), lambda i, *_: (i, 0, 0))],
        out_specs=pl.BlockSpec(memory_space=pl.ANY),
        scratch_shapes=[pltpu.VMEM((2, SLOT_ROWS, D_MODEL), F32), pltpu.VMEM((FFN_TILE, D_MODEL), F32),
                        pltpu.SemaphoreType.DMA((3,))],
    )
    return pl.pallas_call(
        functools.partial(_dispatch_kernel, n_t=n_t, n_tiles=n_tiles),
        grid_spec=grid_spec,
        out_shape=jax.ShapeDtypeStruct((n_tiles * FFN_TILE, D_MODEL), F32),
        compiler_params=_cparams(("arbitrary",)),
        name="dispatch",
    )(tile_tot, tail_n, tail_off, n_used, c8, lbase, gbase, h3, ids, lrank, lbase.astype(F32))


def _ffn_kernel(te_ref, nt_ref, xs_ref, wg_ref, bg_ref, wu_ref, bu_ref, wd_ref, bd_ref, y_ref, wbf):
    i = pl.program_id(0)
    n_used = nt_ref[0]

    @pl.when(jnp.logical_or(i == 0, te_ref[i] != te_ref[jnp.maximum(i - 1, 0)]))
    def _():
        wbf[0] = wg_ref[0].astype(BF16)
        wbf[1] = wu_ref[0].astype(BF16)
        wbf[2] = wd_ref[0].astype(BF16)

    @pl.when(i < n_used)
    def _():
        x = xs_ref[...].astype(BF16)
        gate = jnp.minimum(jnp.dot(x, wbf[0], preferred_element_type=F32) + bg_ref[0], SWIGLU_LIMIT)
        up = jnp.clip(jnp.dot(x, wbf[1], preferred_element_type=F32) + bu_ref[0],
                      -SWIGLU_LIMIT, SWIGLU_LIMIT)
        hid = (up + 1.0) * gate * _sigmoid(gate * SWIGLU_ALPHA)
        y_ref[...] = jnp.dot(hid.astype(BF16), wbf[2], preferred_element_type=F32) + bd_ref[0]

    @pl.when(i >= n_used)
    def _():
        y_ref[...] = jnp.zeros_like(y_ref)


def _ffn(tile_expert, n_used, xs, wg, bg, wu, bu, wd, bd):
    n_tiles = xs.shape[0] // FFN_TILE
    wspec = pl.BlockSpec((1, D_MODEL, D_MODEL), lambda i, te, nt: (te[i], 0, 0))
    bspec = pl.BlockSpec((1, 1, D_MODEL), lambda i, te, nt: (te[i], 0, 0))
    grid_spec = pltpu.PrefetchScalarGridSpec(
        num_scalar_prefetch=2,
        grid=(n_tiles,),
        in_specs=[pl.BlockSpec((FFN_TILE, D_MODEL), lambda i, te, nt: (jnp.minimum(i, nt[0] - 1), 0)),
                  wspec, bspec, wspec, bspec, wspec, bspec],
        out_specs=pl.BlockSpec((FFN_TILE, D_MODEL), lambda i, te, nt: (i, 0)),
        scratch_shapes=[pltpu.VMEM((3, D_MODEL, D_MODEL), BF16)],
    )
    return pl.pallas_call(
        _ffn_kernel,
        grid_spec=grid_spec,
        out_shape=jax.ShapeDtypeStruct((n_tiles * FFN_TILE, D_MODEL), F32),
        compiler_params=_cparams(("arbitrary",)),
        name="ffn",
    )(tile_expert, n_used, xs, wg, bg, wu, bu, wd, bd)


def _combine_kernel(tot_ref, c8_ref, lb_ref, gb_ref, c8n_ref, lbn_ref, gbn_ref,
                    ys_hbm, x2_ref, ids_ref, lrank_ref, prob_ref, lbv_ref, g_ref,
                    op_ref, os_ref, sbuf, sem, *, n_pt, n_t):
    i = pl.program_id(0)
    slot = lax.rem(i, 2)

    def fetch(c8, lb, gb, dst_slot):
        _strip_copies(c8, lb, gb,
                      lambda lr, gr, rows: pltpu.make_async_copy(ys_hbm.at[pl.ds(gr, rows), :],
                                                                 sbuf.at[dst_slot, pl.ds(lr, rows), :],
                                                                 sem.at[dst_slot]))

    @pl.when(i == 0)
    def _():
        sbuf[...] = jnp.zeros_like(sbuf)
        fetch(c8_ref, lb_ref, gb_ref, 0)

    @pl.when(i + 1 < n_t)
    def _():
        fetch(c8n_ref, lbn_ref, gbn_ref, 1 - slot)

    rows = pl.multiple_of(tot_ref[i], STRIP)
    pltpu.make_async_copy(ys_hbm.at[pl.ds(0, rows), :], sbuf.at[slot, pl.ds(0, rows), :], sem.at[slot]).wait()
    selw = _slot_select(ids_ref, lrank_ref, lbv_ref[0], [prob_ref[:, kq:kq + 1] for kq in range(TOP_K)])
    y = jnp.dot(selw, sbuf[slot].astype(BF16), preferred_element_type=F32)
    out = _rms(x2_ref[...] + y, g_ref[...])

    @pl.when(i < n_pt)
    def _():
        op_ref[...] = out

    @pl.when(i >= n_pt)
    def _():
        os_ref[...] = out


def _combine(tile_tot, c8, lbase, gbase, ys, x2, ids, lrank, prob, g_final, n_pt):
    n_t = c8.shape[0]
    last = n_t - 1
    cur = pl.BlockSpec((1, 1, N_EXPERTS), lambda i, *_: (i, 0, 0), memory_space=pltpu.SMEM)
    nxt = pl.BlockSpec((1, 1, N_EXPERTS), lambda i, *_: (jnp.minimum(i + 1, last), 0, 0), memory_space=pltpu.SMEM)
    row = lambda w: pl.BlockSpec((TOK_TILE, w), lambda i, *_: (i, 0))
    grid_spec = pltpu.PrefetchScalarGridSpec(
        num_scalar_prefetch=1,
        grid=(n_t,),
        in_specs=[cur, cur, cur, nxt, nxt, nxt,
                  pl.BlockSpec(memory_space=pl.ANY),
                  row(D_MODEL), row(TOP_K), row(TOP_K), row(TOP_K),
                  pl.BlockSpec((1, 1, N_EXPERTS), lambda i, *_: (i, 0, 0)),
                  pl.BlockSpec((1, D_MODEL), lambda i, *_: (0, 0))],
        out_specs=[pl.BlockSpec((TOK_TILE, D_MODEL), lambda i, *_: (jnp.minimum(i, n_pt - 1), 0)),
                   pl.BlockSpec((TOK_TILE, D_MODEL), lambda i, *_: (jnp.maximum(i - n_pt, 0), 0))],
        scratch_shapes=[pltpu.VMEM((2, SLOT_ROWS, D_MODEL), F32), pltpu.SemaphoreType.DMA((2,))],
    )
    return pl.pallas_call(
        functools.partial(_combine_kernel, n_pt=n_pt, n_t=n_t),
        grid_spec=grid_spec,
        out_shape=[jax.ShapeDtypeStruct((n_pt * TOK_TILE, D_MODEL), F32),
                   jax.ShapeDtypeStruct(((n_t - n_pt) * TOK_TILE, D_MODEL), F32)],
        compiler_params=_cparams(("arbitrary",)),
        name="combine",
    )(tile_tot, c8, lbase, gbase, c8, lbase, gbase, ys, x2, ids, lrank, prob, lbase.astype(F32), g_final)


def kernel(x_prompt, x_sample, mem_prompt, state_shift, state_wkv, state_conv, cache_mem_k, cache_mem_v,
           g_mix, w_in, mu_shift, w0, w_decay_up, a0, w_iclr_up, w_glora_up, k_k, k_a, r_k, gn_g, gn_b,
           dw_w, dw_b, cln_g, cln_b, w_out, g_xattn, g_mem, w_q, w_mk, w_mv, w_o,
           g_moe, w_router, b_router, w_moe_gate, b_moe_gate, w_moe_up, b_moe_up, w_moe_down, b_moe_down,
           g_final):
    bp, tp, _ = x_prompt.shape
    bs, ts, _ = x_sample.shape
    n_p, n_s = bp * tp, bs * ts
    n_all = n_p + n_s
    assert tp % WKV_BLOCK == 0 and tp % TOK_TILE == 0 and n_s == TOK_TILE
    assert ts <= SAMPLE_PAD and bs % SAMPLE_SEQS == 0 and bs % ATTN_SEQS == 0
    row2 = lambda a: a.reshape(1, -1)

    x_p = x_prompt.reshape(n_p, D_MODEL)
    x_s = x_sample.reshape(n_s, D_MODEL)
    p_rw, u = _inproj(x_p, x_s, row2(g_mix[0]), w_in[0].astype(BF16))

    head_idx = jnp.arange(RW_WIDTH) // RW_HEAD_DIM
    eseg = (head_idx[:, None] == head_idx[None, :]).astype(BF16)
    wts = (row2(mu_shift[0]), row2(w0[0]), w_decay_up[0].astype(BF16), row2(a0[0]),
           w_iclr_up[0].astype(BF16), w_glora_up[0].astype(BF16), row2(k_k[0]), row2(k_a[0]),
           row2(r_k[0]), row2(gn_g[0]), row2(gn_b[0]), eseg)
    conv_w = (dw_w[0], row2(dw_b[0]), row2(cln_g[0]), row2(cln_b[0]))
    n_tp = tp // WKV_BLOCK
    rw_p, wkv_p, cv_p = _mixer(p_rw, u, 0, jnp.zeros((bp, 1, RW_PROJ), F32),
                               jnp.zeros((bp, N_GROUPS, RW_HEAD_DIM, GW), F32),
                               jnp.zeros((bp, CONV_PAD, CONV_WIDTH), F32), wts, conv_w,
                               batch=bp, n_seq=1, tt_seq=WKV_BLOCK, chunk=WKV_CHUNK, t_valid=tp, n_t=n_tp,
                               out_rows=n_p)
    pad_seq = lambda a: jnp.pad(a.reshape(bs, ts, -1), ((0, 0), (0, SAMPLE_PAD - ts), (0, 0))).reshape(
        bs * SAMPLE_PAD, -1)
    unpad_seq = lambda a: a.reshape(bs, SAMPLE_PAD, -1)[:, :ts].reshape(n_s, -1)
    p_rw_s = p_rw[n_p:].reshape(bs, ts, RW_PROJ)
    u_s = u[n_p:]
    hist_s = jnp.pad(state_conv[0], ((0, 0), (CONV_PAD - (CONV_K - 1), 0), (0, 0)))
    rw_s_pad, wkv_s, cv_s_pad = _mixer(pad_seq(p_rw_s), pad_seq(u_s), 0, state_shift[0].reshape(bs, 1, RW_PROJ),
                                       _to_grouped(state_wkv[0]), hist_s, wts, conv_w,
                                       batch=bs, n_seq=SAMPLE_SEQS, tt_seq=SAMPLE_PAD, chunk=SAMPLE_PAD,
                                       t_valid=ts, n_t=1, out_rows=bs * SAMPLE_PAD)
    rw_s, cv_s = unpad_seq(rw_s_pad), unpad_seq(cv_s_pad)

    x1, q = _mid1(x_p, x_s, rw_p, rw_s, cv_p, cv_s, w_out[0].astype(BF16), row2(g_xattn[0]), w_q[0].astype(BF16))

    mk_p, mv_p = _memkv(mem_prompt.reshape(bp * N_MEM, D_MODEL), row2(g_mem[0]),
                        w_mk[0].astype(BF16), w_mv[0].astype(BF16))
    mk_p = mk_p.reshape(bp, N_MEM, D_MODEL)
    mv_p = mv_p.reshape(bp, N_MEM, D_MODEL)
    o_p = _attn_rows(q, mk_p, mv_p, n_p, tp // TOK_TILE)
    o_s = unpad_seq(_attn_seq(pad_seq(q[n_p:]).reshape(bs, SAMPLE_PAD, D_MODEL), cache_mem_k, cache_mem_v))

    tri = (jnp.arange(TOK_TILE)[None, :] < jnp.arange(TOK_TILE)[:, None]).astype(BF16)
    x2, h3, ids, prob, lrank, cnt3, base3, tot = _mid2(x1, o_p, o_s, w_o[0].astype(BF16), row2(g_moe[0]),
                                                       w_router[0], row2(b_router[0]), tri)

    n_t = n_all // TOK_TILE
    cnt = cnt3.astype(I32)
    c8 = ((cnt + STRIP - 1) // STRIP) * STRIP
    lbase = jnp.cumsum(c8, axis=-1) - c8
    tile_tot = jnp.sum(c8, axis=(1, 2))
    used = tot[0].astype(I32)
    padded = ((used + FFN_TILE - 1) // FFN_TILE) * FFN_TILE
    ends = jnp.cumsum(padded)
    offs = ends - padded
    gbase = offs[None, None, :] + base3.astype(I32)
    tail_n = padded - used
    n_tiles = -(-(n_all * TOP_K + n_t * N_EXPERTS * (STRIP - 1)) // FFN_TILE) + N_EXPERTS
    n_used = (ends[-1] // FFN_TILE).reshape(1)
    tile_ids = jnp.minimum(jnp.arange(n_tiles, dtype=I32), n_used - 1)
    tile_expert = jnp.sum((ends[None, :] // FFN_TILE) <= tile_ids[:, None], axis=1).astype(I32)
    tile_expert = jnp.minimum(tile_expert, N_EXPERTS - 1)

    xs = _dispatch(tile_tot, tail_n, offs + used, n_used, c8, lbase, gbase, h3, ids, lrank, n_tiles)
    ys = _ffn(tile_expert, n_used, xs,
              w_moe_gate[0], b_moe_gate[0].reshape(N_EXPERTS, 1, D_MODEL),
              w_moe_up[0], b_moe_up[0].reshape(N_EXPERTS, 1, D_MODEL),
              w_moe_down[0], b_moe_down[0].reshape(N_EXPERTS, 1, D_MODEL))

    y_p, y_s = _combine(tile_tot, c8, lbase, gbase, ys, x2, ids, lrank, prob, row2(g_final), n_p // TOK_TILE)

    last_rows = lambda a, rows: jnp.stack([a[(b + 1) * tp - rows:(b + 1) * tp] for b in range(bp)])
    new_conv_s = jnp.concatenate([state_conv[0], u_s.reshape(bs, ts, CONV_WIDTH)], axis=1)[:, -(CONV_K - 1):]
    kv_shape = (1, bp, N_MEM, X_HEADS, X_HEAD_DIM)
    return (y_p.reshape(bp, tp, D_MODEL), y_s.reshape(bs, ts, D_MODEL),
            last_rows(p_rw, 1).reshape(1, bp, RW_PROJ), _from_grouped(wkv_p)[None],
            last_rows(u, CONV_K - 1)[None],
            mk_p.reshape(kv_shape), mv_p.reshape(kv_shape),
            p_rw_s[:, -1][None], _from_grouped(wkv_s)[None], new_conv_s[None])
```

```python
import functools

import jax
import jax.numpy as jnp
from jax import lax
from jax.experimental import pallas as pl
from jax.experimental.pallas import tpu as pltpu

F32 = jnp.float32
BF16 = jnp.bfloat16
I32 = jnp.int32

D_MODEL = 1024
RW_HEADS = 8
RW_HEAD_DIM = 64
RW_WIDTH = RW_HEADS * RW_HEAD_DIM
DECAY_LORA = 64
ICLR_LORA = 64
GATE_LORA = 128
RW_PROJ = 3 * RW_WIDTH + DECAY_LORA + ICLR_LORA + GATE_LORA
GN_EPS = 64e-5
CONV_WIDTH = D_MODEL - RW_WIDTH
CONV_K = 31
LN_EPS = 1e-5
IN_PROJ = RW_PROJ + 2 * CONV_WIDTH
N_MEM = 256
X_HEADS = 4
X_HEAD_DIM = D_MODEL // X_HEADS
ATTN_SCALE = X_HEAD_DIM ** -0.5
N_EXPERTS = 32
TOP_K = 4
SWIGLU_LIMIT = 7.0
SWIGLU_ALPHA = 1.702
NORM_EPS = 1e-5

TOK_TILE = 512
WKV_BLOCK = 256
WKV_CHUNK = 64
SAMPLE_PAD = 8
SAMPLE_SEQS = 4
ATTN_SEQS = 4
GROUP = 4
GW = GROUP * RW_HEAD_DIM
N_GROUPS = RW_HEADS // GROUP
CHAIN_BATCH = 8
FFN_TILE = 512
STRIP = 8
SLOT_ROWS = TOP_K * TOK_TILE + N_EXPERTS * STRIP
CONV_PAD = 32
VMEM_LIMIT = 56 * 1024 * 1024


def _cparams(sem):
    return pltpu.CompilerParams(dimension_semantics=sem, vmem_limit_bytes=VMEM_LIMIT)


def _dot(a, b):
    return jnp.dot(a.astype(BF16), b.astype(BF16), preferred_element_type=F32)


def _dot_nt(a, b):
    return lax.dot_general(a.astype(BF16), b.astype(BF16), (((1,), (1,)), ((), ())),
                           preferred_element_type=F32)


def _dot_tn(a, b):
    return lax.dot_general(a.astype(BF16), b.astype(BF16), (((0,), (0,)), ((), ())),
                           preferred_element_type=F32)


def _split2(x):
    hi = x.astype(BF16)
    lo = (x - hi.astype(F32)).astype(BF16)
    return hi, lo


def _split3(x):
    hi = x.astype(BF16)
    r1 = x - hi.astype(F32)
    mid = r1.astype(BF16)
    lo = (r1 - mid.astype(F32)).astype(BF16)
    return hi, mid, lo


def _segsum(x, eseg):
    hi, lo = _split2(x)
    return (jnp.dot(hi, eseg, preferred_element_type=F32)
            + jnp.dot(lo, eseg, preferred_element_type=F32))


def _rms(x, g):
    return x * lax.rsqrt(jnp.mean(x * x, axis=-1, keepdims=True) + NORM_EPS) * g


def _sigmoid(x):
    return 1.0 / (1.0 + jnp.exp(-x))


def _pick(first, a_ref, b_ref):
    return jnp.where(first, a_ref[...], b_ref[...])


def _two_source_specs(width, n_pt):
    return (pl.BlockSpec((TOK_TILE, width), lambda i: (jnp.minimum(i, n_pt - 1), 0)),
            pl.BlockSpec((TOK_TILE, width), lambda i: (jnp.maximum(i - n_pt, 0), 0)))


def _inproj_kernel(xp_ref, xs_ref, g_ref, w_ref, prw_ref, u_ref, *, n_pt):
    h = _rms(_pick(pl.program_id(0) < n_pt, xp_ref, xs_ref), g_ref[...])
    p = _dot(h, w_ref[...])
    prw_ref[...] = p[:, :RW_PROJ]
    u_ref[...] = p[:, RW_PROJ:RW_PROJ + CONV_WIDTH] * _sigmoid(p[:, RW_PROJ + CONV_WIDTH:])


def _inproj(x_p, x_s, g_mix, w_in_bf):
    n = x_p.shape[0] + x_s.shape[0]
    n_pt = x_p.shape[0] // TOK_TILE
    return pl.pallas_call(
        functools.partial(_inproj_kernel, n_pt=n_pt),
        grid=(n // TOK_TILE,),
        in_specs=[*_two_source_specs(D_MODEL, n_pt),
                  pl.BlockSpec((1, D_MODEL), lambda i: (0, 0)),
                  pl.BlockSpec((D_MODEL, IN_PROJ), lambda i: (0, 0))],
        out_specs=[pl.BlockSpec((TOK_TILE, RW_PROJ), lambda i: (i, 0)),
                   pl.BlockSpec((TOK_TILE, CONV_WIDTH), lambda i: (i, 0))],
        out_shape=[jax.ShapeDtypeStruct((n, RW_PROJ), F32),
                   jax.ShapeDtypeStruct((n, CONV_WIDTH), F32)],
        compiler_params=_cparams(("parallel",)),
        name="inproj",
    )(x_p, x_s, g_mix, w_in_bf)


def _bd(x, w):
    r, lanes = x.shape
    head = lax.broadcasted_iota(I32, (r, lanes), 1) >> (w.bit_length() - 1)
    return jnp.concatenate([jnp.where(head == h, x, 0.0) for h in range(GROUP)], axis=0).astype(BF16)


def _collapse(x):
    head = lax.broadcasted_iota(I32, (RW_HEAD_DIM, GW), 1) >> (RW_HEAD_DIM.bit_length() - 1)
    out = jnp.zeros((RW_HEAD_DIM, GW), F32)
    for h in range(GROUP):
        out = out + jnp.where(head == h, x[h * RW_HEAD_DIM:(h + 1) * RW_HEAD_DIM, :], 0.0)
    return out


def _conv_module(t, u_ref, hist_ref, cw_ref, cb_ref, cg_ref, cbeta_ref, cv_ref, ext_ref, sh_ref,
                 *, n_seq, tt_seq, n_t):
    first = CONV_PAD - (CONV_K - 1)
    span = tt_seq + CONV_PAD - 8
    for s in range(n_seq):
        rows = slice(s * tt_seq, (s + 1) * tt_seq)

        @pl.when(t == 0)
        def _():
            ext_ref[s, 0:CONV_PAD, :] = hist_ref[s]

        ext_ref[s, CONV_PAD:CONV_PAD + tt_seq, :] = u_ref[rows, :]
        for b in range(1, 8):
            sh_ref[b, 0:span, :] = ext_ref[s, pl.ds(b, span), :]
        acc = jnp.zeros((tt_seq, CONV_WIDTH), F32) + cb_ref[...]
        for j in range(CONV_K):
            a8, b = divmod(first + j, 8)
            win = ext_ref[s, pl.ds(8 * a8, tt_seq), :] if b == 0 else sh_ref[b, pl.ds(8 * a8, tt_seq), :]
            acc = acc + win * cw_ref[j:j + 1, :]
        mean = jnp.mean(acc, axis=-1, keepdims=True)
        cen = acc - mean
        var = jnp.mean(cen * cen, axis=-1, keepdims=True)
        cn = cen * lax.rsqrt(var + LN_EPS) * cg_ref[...] + cbeta_ref[...]
        cv_ref[rows, :] = cn * _sigmoid(cn)
        if n_t > 1:
            ext_ref[s, 0:CONV_PAD, :] = ext_ref[s, tt_seq:tt_seq + CONV_PAD, :]


def _wkv_kernel(p_ref, prev_ref, s0_ref, mu_ref, w0_ref, wdup_ref, a0_ref, aup_ref, gup_ref,
                kk_ref, ka_ref, rk_ref, gng_ref, gnb_ref, eseg_ref,
                u_ref, hist_ref, cw_ref, cb_ref, cg_ref, cbeta_ref,
                out_ref, sfin_ref, cv_ref,
                s_ref, last_ref, rt_ref, at_ref, bt_ref, kt_ref, v_ref, pc_ref, y_ref,
                r_ref, km_ref, g_ref, ext_ref, sh_ref, *, n_seq, tt_seq, chunk, t_valid, n_t):
    t = pl.program_id(1)
    tt = n_seq * tt_seq
    n_chunk = tt_seq // chunk
    n_fused = chunk.bit_length() - 2
    nd = RW_HEAD_DIM
    c1, c2, c3 = RW_WIDTH, 2 * RW_WIDTH, 3 * RW_WIDTH
    c4 = c3 + DECAY_LORA
    c5 = c4 + ICLR_LORA

    @pl.when(t == 0)
    def _():
        for s in range(n_seq):
            for h in range(RW_HEADS):
                g, hg = divmod(h, GROUP)
                s_ref[s, g, :, hg * nd:(hg + 1) * nd] = s0_ref[s, h]
        if n_seq == 1:
            last_ref[0:1, :] = prev_ref[0]

    _conv_module(t, u_ref, hist_ref, cw_ref, cb_ref, cg_ref, cbeta_ref, cv_ref, ext_ref, sh_ref,
                 n_seq=n_seq, tt_seq=tt_seq, n_t=n_t)

    p = p_ref[...]
    row = lax.broadcasted_iota(I32, (tt, 1), 0)
    seq_row = row & (tt_seq - 1)
    rolled = pltpu.roll(p, shift=1, axis=0)
    if n_seq == 1:
        ps = jnp.where(seq_row == 0, last_ref[0:1, :], rolled)
        last_ref[0:1, :] = p[tt - 1:tt, :]
    else:
        prev_rows = jnp.concatenate(
            [jnp.broadcast_to(prev_ref[q], (tt_seq, RW_PROJ)) for q in range(n_seq)], axis=0)
        ps = jnp.where(seq_row == 0, prev_rows, rolled)
    z = p + mu_ref[...] * (ps - p)
    r, k, v = z[:, :c1], z[:, c1:c2], z[:, c2:c3]
    wd, ad, gd = z[:, c3:c4], z[:, c4:c5], z[:, c5:]
    wpre = w0_ref[...] + _dot(jnp.tanh(wd), wdup_ref[...])
    neg = -wpre
    w = -(jnp.maximum(neg, 0.0) + jnp.log(1.0 + jnp.exp(-jnp.abs(neg)))) - 0.5
    a = _sigmoid(a0_ref[...] + _dot(ad, aup_ref[...]))
    g_ref[...] = _dot(_sigmoid(gd), gup_ref[...])
    eseg = eseg_ref[...]
    kk = k * kk_ref[...]
    kk = kk / jnp.maximum(jnp.sqrt(_segsum(kk * kk, eseg)), 1e-12)
    kmod = k * (1.0 + (a - 1.0) * ka_ref[...])
    logd = -jnp.exp(w)
    if t_valid < tt_seq * n_t:
        valid = (t * tt_seq + seq_row) < t_valid
        logd = jnp.where(valid, logd, 0.0)
        kk = jnp.where(valid, kk, 0.0)
        kmod = jnp.where(valid, kmod, 0.0)
        v = jnp.where(valid, v, 0.0)
    ri = lax.broadcasted_iota(I32, (tt, tt), 0)
    ci = lax.broadcasted_iota(I32, (tt, tt), 1)
    shift = chunk.bit_length() - 1
    ltri = jnp.where(ci <= ri, jnp.where((ri >> shift) == (ci >> shift), 1.0, 0.0), 0.0).astype(BF16)
    hi, mid, lo = _split3(logd)
    logp = (jnp.dot(ltri, hi, preferred_element_type=F32)
            + jnp.dot(ltri, mid, preferred_element_type=F32)
            + jnp.dot(ltri, lo, preferred_element_type=F32))
    pcum = jnp.exp(logp)
    pinv = jnp.exp(-logp)
    r_ref[...] = r
    km_ref[...] = kmod
    v_ref[...] = v
    pc_ref[...] = pcum
    rt_ref[...] = r * pcum
    at_ref[...] = -kk * jnp.exp(logp - logd)
    bt_ref[...] = kk * a * pinv
    kt_ref[...] = kmod * pinv

    cw = GROUP * chunk
    ti = lax.broadcasted_iota(I32, (chunk, cw), 0)
    si = lax.broadcasted_iota(I32, (chunk, cw), 1) & (chunk - 1)
    strict = si < ti
    incl = si <= ti
    eye_c = jnp.where(si == ti, 1.0, 0.0)
    ji = lax.broadcasted_iota(I32, (nd, GW), 0)
    jj = lax.broadcasted_iota(I32, (nd, GW), 1) & (nd - 1)
    eye_n = jnp.where(ji == jj, 1.0, 0.0)

    all_chains = [(s, c, g) for s in range(n_seq) for c in range(n_chunk) for g in range(N_GROUPS)]

    def blk(ref, s, c, g):
        r0 = s * tt_seq + c * chunk
        return ref[r0:r0 + chunk, g * GW:(g + 1) * GW]

    rp, yv, mm, gm = {}, {}, {}, {}

    def chunk_operators(chains):
        am = {q: blk(at_ref, *q) for q in chains}
        rm = {q: blk(rt_ref, *q) for q in chains}
        bm = {q: blk(bt_ref, *q) for q in chains}
        km = {q: blk(kt_ref, *q) for q in chains}
        vm = {q: blk(v_ref, *q) for q in chains}
        a_ab, a_ak, a_rb, a_rk = {}, {}, {}, {}
        for q in chains:
            ar = jnp.concatenate([am[q], rm[q]], axis=0)
            s1 = _dot_nt(ar, _bd(bm[q], nd))
            s2 = _dot_nt(ar, _bd(km[q], nd))
            a_ab[q] = jnp.where(strict, s1[:chunk], 0.0)
            a_rb[q] = jnp.where(incl, s1[chunk:], 0.0)
            a_ak[q] = jnp.where(strict, s2[:chunk], 0.0)
            a_rk[q] = jnp.where(incl, s2[chunk:], 0.0)
        tinv = {q: eye_c + a_ab[q] for q in chains}
        pw = {q: _dot(a_ab[q], _bd(a_ab[q], chunk)) for q in chains}
        for lvl in range(n_fused):
            last = lvl == n_fused - 1
            for q in chains:
                lhs = tinv[q] if last else jnp.concatenate([tinv[q], pw[q]], axis=0)
                zz = _dot(lhs, _bd(pw[q], chunk))
                tinv[q] = tinv[q] + zz[:chunk]
                if not last:
                    pw[q] = zz[chunk:]
        akv, rkv = {}, {}
        for q in chains:
            zv = _dot(jnp.concatenate([a_ak[q], a_rk[q]], axis=0), _bd(vm[q], nd))
            akv[q], rkv[q] = zv[:chunk], zv[chunk:]
        ap = {q: _dot(tinv[q], _bd(am[q], nd)) for q in chains}
        uv = {q: _dot(tinv[q], _bd(akv[q], nd)) for q in chains}
        for q in chains:
            rp[q] = rm[q] + _dot(a_rb[q], _bd(ap[q], nd))
            yv[q] = _dot(a_rb[q], _bd(uv[q], nd)) + rkv[q]
        for q in chains:
            s, c, g = q
            r_end = s * tt_seq + (c + 1) * chunk
            pc = pc_ref[r_end - 8:r_end, g * GW:(g + 1) * GW][7:8, :]
            cross = _dot_tn(jnp.concatenate([ap[q], uv[q]], axis=1), bm[q])
            vk = _dot_tn(vm[q], km[q])
            mm[q] = (eye_n + _collapse(cross[:GW])) * pc
            gm[q] = (_collapse(cross[GW:]) + _collapse(vk)) * pc

    for i0 in range(0, len(all_chains), CHAIN_BATCH):
        chunk_operators(all_chains[i0:i0 + CHAIN_BATCH])

    for s in range(n_seq):
        for g in range(N_GROUPS):
            st = s_ref[s, g]
            for c in range(n_chunk):
                q = (s, c, g)
                r0 = s * tt_seq + c * chunk
                y_ref[r0:r0 + chunk, g * GW:(g + 1) * GW] = _dot_nt(rp[q], _bd(st, nd)) + yv[q]
                st = _dot(st, _bd(mm[q], nd)) + gm[q]
            s_ref[s, g] = st

    y = y_ref[...]
    inv_n = 1.0 / RW_HEAD_DIM
    mean = _segsum(y, eseg) * inv_n
    yc = y - mean
    var = _segsum(yc * yc, eseg) * inv_n
    yn = yc * lax.rsqrt(var + GN_EPS) * gng_ref[...] + gnb_ref[...]
    bonus = _segsum(r_ref[...] * km_ref[...] * rk_ref[...], eseg) * v_ref[...]
    out_ref[...] = (yn + bonus) * g_ref[...]

    @pl.when(t == n_t - 1)
    def _():
        for s in range(n_seq):
            for h in range(RW_HEADS):
                g, hg = divmod(h, GROUP)
                sfin_ref[s, h] = s_ref[s, g, :, hg * nd:(hg + 1) * nd]


def _mixer(p_rows, u_rows, row_block0, prev, s0, hist, wts, conv_w, *, batch, n_seq, tt_seq, chunk, t_valid,
           n_t, out_rows):
    (mu, w0, wdup, a0, aup, gup, k_k, k_a, r_k, gn_g, gn_b, eseg) = wts
    dw_w, dw_b, ln_g, ln_b = conv_w
    tt = n_seq * tt_seq
    kern = functools.partial(_wkv_kernel, n_seq=n_seq, tt_seq=tt_seq, chunk=chunk, t_valid=t_valid, n_t=n_t)
    const = lambda shape: pl.BlockSpec(shape, lambda b, t: tuple(0 for _ in shape))
    row_out = lambda w: pl.BlockSpec((tt, w), lambda b, t: (b * n_t + t, 0))
    blk = pltpu.VMEM((tt, RW_WIDTH), F32)
    st_spec = pl.BlockSpec((n_seq, RW_HEADS, RW_HEAD_DIM, RW_HEAD_DIM), lambda b, t: (b, 0, 0, 0))
    return pl.pallas_call(
        kern,
        grid=(batch // n_seq, n_t),
        in_specs=[pl.BlockSpec((tt, RW_PROJ), lambda b, t: (row_block0 + b * n_t + t, 0)),
                  pl.BlockSpec((n_seq, 1, RW_PROJ), lambda b, t: (b, 0, 0)),
                  st_spec,
                  const((1, RW_PROJ)), const((1, RW_WIDTH)), const((DECAY_LORA, RW_WIDTH)),
                  const((1, RW_WIDTH)), const((ICLR_LORA, RW_WIDTH)), const((GATE_LORA, RW_WIDTH)),
                  const((1, RW_WIDTH)), const((1, RW_WIDTH)), const((1, RW_WIDTH)),
                  const((1, RW_WIDTH)), const((1, RW_WIDTH)), const((RW_WIDTH, RW_WIDTH)),
                  pl.BlockSpec((tt, CONV_WIDTH), lambda b, t: (row_block0 + b * n_t + t, 0)),
                  pl.BlockSpec((n_seq, CONV_PAD, CONV_WIDTH), lambda b, t: (b, 0, 0)),
                  const((CONV_K, CONV_WIDTH)), const((1, CONV_WIDTH)),
                  const((1, CONV_WIDTH)), const((1, CONV_WIDTH))],
        out_specs=[row_out(RW_WIDTH), st_spec, row_out(CONV_WIDTH)],
        out_shape=[jax.ShapeDtypeStruct((out_rows, RW_WIDTH), F32),
                   jax.ShapeDtypeStruct((batch, RW_HEADS, RW_HEAD_DIM, RW_HEAD_DIM), F32),
                   jax.ShapeDtypeStruct((out_rows, CONV_WIDTH), F32)],
        scratch_shapes=[pltpu.VMEM((n_seq, N_GROUPS, RW_HEAD_DIM, GW), F32),
                        pltpu.VMEM((8, RW_PROJ), F32),
                        blk, blk, blk, blk, blk, blk, blk, blk, blk, blk,
                        pltpu.VMEM((n_seq, CONV_PAD + tt_seq, CONV_WIDTH), F32),
                        pltpu.VMEM((8, CONV_PAD + tt_seq, CONV_WIDTH), F32)],
        compiler_params=_cparams(("parallel", "arbitrary")),
        name="mixer",
    )(p_rows, prev, s0, mu, w0, wdup, a0, aup, gup, k_k, k_a, r_k, gn_g, gn_b, eseg,
      u_rows, hist, dw_w, dw_b, ln_g, ln_b)


def _mid1_kernel(xp_ref, xs_ref, rwp_ref, rws_ref, cvp_ref, cvs_ref, wo_ref, g_ref, wq_ref, x1_ref, q_ref,
                 *, n_pt):
    first = pl.program_id(0) < n_pt
    rw = _pick(first, rwp_ref, rws_ref)
    cv = _pick(first, cvp_ref, cvs_ref)
    mix = _dot(rw, wo_ref[:RW_WIDTH, :]) + _dot(cv, wo_ref[RW_WIDTH:, :])
    x1 = _pick(first, xp_ref, xs_ref) + mix
    x1_ref[...] = x1
    q = _dot(_rms(x1, g_ref[...]), wq_ref[...])
    q_ref[...] = (q * ATTN_SCALE).astype(BF16)


def _mid1(x_p, x_s, rw_p, rw_s, cv_p, cv_s, w_out_bf, g_x, w_q_bf):
    n = x_p.shape[0] + x_s.shape[0]
    n_pt = rw_p.shape[0] // TOK_TILE
    row = lambda w: pl.BlockSpec((TOK_TILE, w), lambda i: (i, 0))
    full = lambda a, b: pl.BlockSpec((a, b), lambda i: (0, 0))
    return pl.pallas_call(
        functools.partial(_mid1_kernel, n_pt=n_pt),
        grid=(n // TOK_TILE,),
        in_specs=[*_two_source_specs(D_MODEL, n_pt), *_two_source_specs(RW_WIDTH, n_pt),
                  *_two_source_specs(CONV_WIDTH, n_pt),
                  full(D_MODEL, D_MODEL), full(1, D_MODEL), full(D_MODEL, D_MODEL)],
        out_specs=[row(D_MODEL), row(D_MODEL)],
        out_shape=[jax.ShapeDtypeStruct((n, D_MODEL), F32), jax.ShapeDtypeStruct((n, D_MODEL), BF16)],
        compiler_params=_cparams(("parallel",)),
        name="mid1",
    )(x_p, x_s, rw_p, rw_s, cv_p, cv_s, w_out_bf, g_x, w_q_bf)


def _memkv_kernel(m_ref, g_ref, wk_ref, wv_ref, k_ref, v_ref):
    m = _rms(m_ref[...], g_ref[...])
    k_ref[...] = _dot(m, wk_ref[...])
    v_ref[...] = _dot(m, wv_ref[...])


def _memkv(mem, g_mem, w_mk_bf, w_mv_bf):
    n = mem.shape[0]
    tile = N_MEM
    row = pl.BlockSpec((tile, D_MODEL), lambda i: (i, 0))
    full = lambda a, b: pl.BlockSpec((a, b), lambda i: (0, 0))
    return pl.pallas_call(
        _memkv_kernel,
        grid=(n // tile,),
        in_specs=[row, full(1, D_MODEL), full(D_MODEL, D_MODEL), full(D_MODEL, D_MODEL)],
        out_specs=[row, row],
        out_shape=[jax.ShapeDtypeStruct((n, D_MODEL), F32), jax.ShapeDtypeStruct((n, D_MODEL), F32)],
        compiler_params=_cparams(("parallel",)),
        name="memkv",
    )(mem, g_mem, w_mk_bf, w_mv_bf)


def _attn_head(q_h, k_h, v_h):
    s = _dot_nt(q_h, k_h)
    e = jnp.exp(s - jnp.max(s, axis=-1, keepdims=True))
    pr = e / jnp.sum(e, axis=-1, keepdims=True)
    return _dot(pr, v_h).astype(BF16)


def _attn_rows_kernel(q_ref, k_ref, v_ref, o_ref):
    q = q_ref[...]
    for h in range(X_HEADS):
        hs = slice(h * X_HEAD_DIM, (h + 1) * X_HEAD_DIM)
        o_ref[:, hs] = _attn_head(q[:, hs], k_ref[0, :, hs], v_ref[0, :, hs])


def _attn_seq_kernel(q_ref, k_hbm, v_hbm, o_ref, kbuf, vbuf, sem, *, n_steps):
    i = pl.program_id(0)
    slot = lax.rem(i, 2)

    def copies(step, s):
        out = []
        for g in range(ATTN_SEQS):
            seq = step * ATTN_SEQS + g
            for h in range(X_HEADS):
                lanes = pl.ds(h * X_HEAD_DIM, X_HEAD_DIM)
                out.append(pltpu.make_async_copy(k_hbm.at[0, seq, :, h, :], kbuf.at[s, g, :, lanes], sem.at[s]))
                out.append(pltpu.make_async_copy(v_hbm.at[0, seq, :, h, :], vbuf.at[s, g, :, lanes], sem.at[s]))
        return out

    @pl.when(i == 0)
    def _():
        for cp in copies(0, 0):
            cp.start()

    @pl.when(i + 1 < n_steps)
    def _():
        for cp in copies(i + 1, 1 - slot):
            cp.start()

    for cp in copies(i, slot):
        cp.wait()
    head_of_lane = lax.broadcasted_iota(I32, (SAMPLE_PAD * X_HEADS, D_MODEL), 1) >> (X_HEAD_DIM.bit_length() - 1)
    head_of_row = lax.broadcasted_iota(I32, (SAMPLE_PAD * X_HEADS, D_MODEL), 0) >> (SAMPLE_PAD.bit_length() - 1)
    own = head_of_lane == head_of_row
    for g in range(ATTN_SEQS):
        q8 = q_ref[g].astype(F32)
        qbd = jnp.where(own, jnp.concatenate([q8] * X_HEADS, axis=0), 0.0)
        s = _dot_nt(qbd, kbuf[slot, g])
        e = jnp.exp(s - jnp.max(s, axis=-1, keepdims=True))
        pr = e / jnp.sum(e, axis=-1, keepdims=True)
        o_all = jnp.where(own, _dot(pr, vbuf[slot, g]), 0.0)
        o8 = o_all[0:SAMPLE_PAD]
        for h in range(1, X_HEADS):
            o8 = o8 + o_all[SAMPLE_PAD * h:SAMPLE_PAD * (h + 1)]
        o_ref[g] = o8.astype(BF16)


def _attn_rows(q, mk, mv, n_rows, tiles_per_seq):
    qspec = pl.BlockSpec((TOK_TILE, D_MODEL), lambda i, j: (i * tiles_per_seq + j, 0))
    kvspec = pl.BlockSpec((1, N_MEM, D_MODEL), lambda i, j: (i, 0, 0))
    return pl.pallas_call(
        _attn_rows_kernel,
        grid=(mk.shape[0], tiles_per_seq),
        in_specs=[qspec, kvspec, kvspec],
        out_specs=qspec,
        out_shape=jax.ShapeDtypeStruct((n_rows, D_MODEL), BF16),
        compiler_params=_cparams(("parallel", "parallel")),
        name="attn",
    )(q, mk, mv)


def _attn_seq(q3, mk, mv):
    b, t, _ = q3.shape
    qspec = pl.BlockSpec((ATTN_SEQS, t, D_MODEL), lambda i: (i, 0, 0))
    anyspec = pl.BlockSpec(memory_space=pl.ANY)
    head_buf = pltpu.VMEM((2, ATTN_SEQS, N_MEM, D_MODEL), F32)
    return pl.pallas_call(
        functools.partial(_attn_seq_kernel, n_steps=b // ATTN_SEQS),
        grid=(b // ATTN_SEQS,),
        in_specs=[qspec, anyspec, anyspec],
        out_specs=qspec,
        out_shape=jax.ShapeDtypeStruct((b, t, D_MODEL), BF16),
        scratch_shapes=[head_buf, head_buf, pltpu.SemaphoreType.DMA((2,))],
        compiler_params=_cparams(("arbitrary",)),
        name="attn",
    )(q3, mk, mv)


def _mid2_kernel(x1_ref, op_ref, os_ref, wo_ref, g_ref, wr_ref, br_ref, tri_ref,
                 x2_ref, h3_ref, ids_ref, prob_ref, lrank_ref, cnt_ref, base_ref, tot_ref, carry_ref, *, n_pt):
    i = pl.program_id(0)

    @pl.when(i == 0)
    def _():
        carry_ref[...] = jnp.zeros_like(carry_ref)

    x2 = x1_ref[...] + _dot(_pick(i < n_pt, op_ref, os_ref), wo_ref[...])
    x2_ref[...] = x2
    h3 = _rms(x2, g_ref[...])
    h3_ref[...] = h3.astype(BF16)
    h3_hi, h3_lo = _split2(h3)
    wr_hi, wr_lo = _split2(wr_ref[...])
    logits = (jnp.dot(h3_hi, wr_hi, preferred_element_type=F32)
              + jnp.dot(h3_hi, wr_lo, preferred_element_type=F32)
              + jnp.dot(h3_lo, wr_hi, preferred_element_type=F32)) + br_ref[...]
    n = logits.shape[0]
    lane = lax.broadcasted_iota(I32, (n, N_EXPERTS), 1)
    work = logits
    vals, ids = [], []
    for _ in range(TOP_K):
        m = jnp.max(work, axis=-1, keepdims=True)
        idx = jnp.min(jnp.where(work == m, lane, N_EXPERTS), axis=-1, keepdims=True)
        vals.append(m)
        ids.append(idx)
        work = jnp.where(lane == idx, -jnp.inf, work)
    exps = [jnp.exp(vk - vals[0]) for vk in vals]
    den = exps[0] + exps[1] + exps[2] + exps[3]
    mask = jnp.zeros((n, N_EXPERTS), F32)
    for idx in ids:
        mask = mask + jnp.where(lane == idx, 1.0, 0.0)
    lrank = jnp.dot(tri_ref[...], mask.astype(BF16), preferred_element_type=F32)
    cnt = jnp.sum(mask, axis=0, keepdims=True)
    cnt_ref[0] = cnt
    base_ref[0] = carry_ref[0:1, :]
    carry_ref[0:1, :] = carry_ref[0:1, :] + jnp.floor((cnt + (STRIP - 1)) * (1.0 / STRIP)) * STRIP
    tot_ref[...] = carry_ref[0:1, :]
    for kq in range(TOP_K):
        ids_ref[:, kq:kq + 1] = ids[kq]
        prob_ref[:, kq:kq + 1] = exps[kq] / den
        lrank_ref[:, kq:kq + 1] = jnp.sum(jnp.where(lane == ids[kq], lrank, 0.0), axis=-1, keepdims=True)


def _mid2(x1, o_p, o_s, w_o_bf, g_moe, w_router, b_router, tri):
    n = x1.shape[0]
    n_t = n // TOK_TILE
    n_pt = o_p.shape[0] // TOK_TILE
    row = lambda w: pl.BlockSpec((TOK_TILE, w), lambda i: (i, 0))
    full = lambda a, b: pl.BlockSpec((a, b), lambda i: (0, 0))
    per_tile = pl.BlockSpec((1, 1, N_EXPERTS), lambda i: (i, 0, 0))
    return pl.pallas_call(
        functools.partial(_mid2_kernel, n_pt=n_pt),
        grid=(n_t,),
        in_specs=[row(D_MODEL), *_two_source_specs(D_MODEL, n_pt), full(D_MODEL, D_MODEL), full(1, D_MODEL),
                  full(D_MODEL, N_EXPERTS), full(1, N_EXPERTS), full(TOK_TILE, TOK_TILE)],
        out_specs=[row(D_MODEL), row(D_MODEL), row(TOP_K), row(TOP_K), row(TOP_K), per_tile, per_tile,
                   full(1, N_EXPERTS)],
        out_shape=[jax.ShapeDtypeStruct((n, D_MODEL), F32), jax.ShapeDtypeStruct((n, D_MODEL), BF16),
                   jax.ShapeDtypeStruct((n, TOP_K), I32), jax.ShapeDtypeStruct((n, TOP_K), F32),
                   jax.ShapeDtypeStruct((n, TOP_K), F32),
                   jax.ShapeDtypeStruct((n_t, 1, N_EXPERTS), F32), jax.ShapeDtypeStruct((n_t, 1, N_EXPERTS), F32),
                   jax.ShapeDtypeStruct((1, N_EXPERTS), F32)],
        scratch_shapes=[pltpu.VMEM((8, N_EXPERTS), F32)],
        compiler_params=_cparams(("arbitrary",)),
        name="mid2",
    )(x1, o_p, o_s, w_o_bf, g_moe, w_router, b_router, tri)


def _slot_select(ids_ref, lrank_ref, lbase_v, weights):
    n = ids_ref.shape[0]
    lane = lax.broadcasted_iota(I32, (n, N_EXPERTS), 1)
    col = lax.broadcasted_iota(I32, (n, SLOT_ROWS), 1)
    sel = jnp.zeros((n, SLOT_ROWS), F32)
    for kq in range(TOP_K):
        base = jnp.sum(jnp.where(lane == ids_ref[:, kq:kq + 1], lbase_v, 0.0), axis=-1, keepdims=True)
        dest = (base + lrank_ref[:, kq:kq + 1]).astype(I32)
        sel = jnp.where(col == dest, weights[kq], sel)
    return sel.astype(BF16)


def _strip_copies(c8_ref, lb_ref, gb_ref, make):
    for e in range(N_EXPERTS):
        rows = c8_ref[0, 0, e]

        @pl.when(rows > 0)
        def _():
            make(pl.multiple_of(lb_ref[0, 0, e], STRIP), pl.multiple_of(gb_ref[0, 0, e], STRIP),
                 pl.multiple_of(rows, STRIP)).start()


def _dispatch_kernel(tot_ref, tailn_ref, tailo_ref, nt_ref,
                     c8_ref, lb_ref, gb_ref, h3_ref, ids_ref, lrank_ref, lbv_ref,
                     xs_hbm, cbuf, zbuf, sem, *, n_t, n_tiles):
    i = pl.program_id(0)
    slot = lax.rem(i, 2)
    sel = _slot_select(ids_ref, lrank_ref, lbv_ref[0], [1.0] * TOP_K)
    cbuf[slot] = _dot_tn(sel, h3_ref[...])
    _strip_copies(c8_ref, lb_ref, gb_ref,
                  lambda lr, gr, rows: pltpu.make_async_copy(cbuf.at[slot, pl.ds(lr, rows), :],
                                                             xs_hbm.at[pl.ds(gr, rows), :], sem.at[slot]))

    def wait_rows(s, rows):
        rows = pl.multiple_of(rows, STRIP)
        pltpu.make_async_copy(cbuf.at[s, pl.ds(0, rows), :], xs_hbm.at[pl.ds(0, rows), :], sem.at[s]).wait()

    @pl.when(i > 0)
    def _():
        wait_rows(1 - slot, tot_ref[jnp.maximum(i - 1, 0)])

    @pl.when(i == n_t - 1)
    def _():
        wait_rows(slot, tot_ref[i])
        zbuf[...] = jnp.zeros_like(zbuf)
        for e in range(N_EXPERTS):
            rows = pl.multiple_of(tailn_ref[e], STRIP)

            @pl.when(rows > 0)
            def _():
                pltpu.make_async_copy(zbuf.at[pl.ds(0, rows), :],
                                      xs_hbm.at[pl.ds(pl.multiple_of(tailo_ref[e], STRIP), rows), :],
                                      sem.at[2]).start()

        for e in range(N_EXPERTS):
            rows = pl.multiple_of(tailn_ref[e], STRIP)

            @pl.when(rows > 0)
            def _():
                pltpu.make_async_copy(zbuf.at[pl.ds(0, rows), :], xs_hbm.at[pl.ds(0, rows), :], sem.at[2]).wait()

        def tile_copy(j):
            return pltpu.make_async_copy(
                zbuf, xs_hbm.at[pl.ds(pl.multiple_of(j * FFN_TILE, FFN_TILE), FFN_TILE), :], sem.at[2])

        def start_tile(j, c):
            tile_copy(j).start()
            return c

        def wait_tile(j, c):
            tile_copy(j).wait()
            return c

        lax.fori_loop(nt_ref[0], n_tiles, start_tile, 0)
        lax.fori_loop(nt_ref[0], n_tiles, wait_tile, 0)


def _dispatch(tile_tot, tail_n, tail_off, n_used, c8, lbase, gbase, h3, ids, lrank, n_tiles):
    n_t = c8.shape[0]
    smem = pl.BlockSpec((1, 1, N_EXPERTS), lambda i, *_: (i, 0, 0), memory_space=pltpu.SMEM)
    row = lambda w: pl.BlockSpec((TOK_TILE, w), lambda i, *_: (i, 0))
    grid_spec = pltpu.PrefetchScalarGridSpec(
        num_scalar_prefetch=4,
        grid=(n_t,),
        in_specs=[smem, smem, smem, row(D_MODEL), row(TOP_K), row(TOP_K),
                  pl.BlockSpec((1, 1, N_EXPERTS), lambda i, *_: (i, 0, 0))],
        out_specs=pl.BlockSpec(memory_space=pl.ANY),
        scratch_shapes=[pltpu.VMEM((2, SLOT_ROWS, D_MODEL), F32), pltpu.VMEM((FFN_TILE, D_MODEL), F32),
                        pltpu.SemaphoreType.DMA((3,))],
    )
    return pl.pallas_call(
        functools.partial(_dispatch_kernel, n_t=n_t, n_tiles=n_tiles),
        grid_spec=grid_spec,
        out_shape=jax.ShapeDtypeStruct((n_tiles * FFN_TILE, D_MODEL), F32),
        compiler_params=_cparams(("arbitrary",)),
        name="dispatch",
    )(tile_tot, tail_n, tail_off, n_used, c8, lbase, gbase, h3, ids, lrank, lbase.astype(F32))


def _ffn_kernel(te_ref, tv_ref, nt_ref, xs_ref, wg_ref, bg_ref, wu_ref, bu_ref, wd_ref, bd_ref, y_ref, wbf):
    del nt_ref
    i = pl.program_id(0)
    valid = tv_ref[i]
    half = FFN_TILE // 2

    @pl.when(jnp.logical_or(i == 0, te_ref[i] != te_ref[jnp.maximum(i - 1, 0)]))
    def _():
        wbf[0] = wg_ref[0].astype(BF16)
        wbf[1] = wu_ref[0].astype(BF16)
        wbf[2] = wd_ref[0].astype(BF16)

    def expert(rows):
        x = xs_ref[0:rows, :].astype(BF16)
        gate = jnp.minimum(jnp.dot(x, wbf[0], preferred_element_type=F32) + bg_ref[0], SWIGLU_LIMIT)
        up = jnp.clip(jnp.dot(x, wbf[1], preferred_element_type=F32) + bu_ref[0],
                      -SWIGLU_LIMIT, SWIGLU_LIMIT)
        hid = (up + 1.0) * gate * _sigmoid(gate * SWIGLU_ALPHA)
        y_ref[0:rows, :] = jnp.dot(hid.astype(BF16), wbf[2], preferred_element_type=F32) + bd_ref[0]

    @pl.when(valid > half)
    def _():
        expert(FFN_TILE)

    @pl.when(jnp.logical_and(valid > 0, valid <= half))
    def _():
        expert(half)
        y_ref[half:, :] = jnp.zeros((FFN_TILE - half, D_MODEL), F32)

    @pl.when(valid == 0)
    def _():
        y_ref[...] = jnp.zeros_like(y_ref)


def _ffn(tile_expert, tile_valid, n_used, xs, wg, bg, wu, bu, wd, bd):
    n_tiles = xs.shape[0] // FFN_TILE
    wspec = pl.BlockSpec((1, D_MODEL, D_MODEL), lambda i, te, tv, nt: (te[i], 0, 0))
    bspec = pl.BlockSpec((1, 1, D_MODEL), lambda i, te, tv, nt: (te[i], 0, 0))
    grid_spec = pltpu.PrefetchScalarGridSpec(
        num_scalar_prefetch=3,
        grid=(n_tiles,),
        in_specs=[pl.BlockSpec((FFN_TILE, D_MODEL), lambda i, te, tv, nt: (jnp.minimum(i, nt[0] - 1), 0)),
                  wspec, bspec, wspec, bspec, wspec, bspec],
        out_specs=pl.BlockSpec((FFN_TILE, D_MODEL), lambda i, te, tv, nt: (i, 0)),
        scratch_shapes=[pltpu.VMEM((3, D_MODEL, D_MODEL), BF16)],
    )
    return pl.pallas_call(
        _ffn_kernel,
        grid_spec=grid_spec,
        out_shape=jax.ShapeDtypeStruct((n_tiles * FFN_TILE, D_MODEL), F32),
        compiler_params=_cparams(("arbitrary",)),
        name="ffn",
    )(tile_expert, tile_valid, n_used, xs, wg, bg, wu, bu, wd, bd)


def _combine_kernel(tot_ref, c8_ref, lb_ref, gb_ref, c8n_ref, lbn_ref, gbn_ref,
                    ys_hbm, x2_ref, ids_ref, lrank_ref, prob_ref, lbv_ref, g_ref,
                    op_ref, os_ref, sbuf, sem, *, n_pt, n_t):
    i = pl.program_id(0)
    slot = lax.rem(i, 2)

    def fetch(c8, lb, gb, dst_slot):
        _strip_copies(c8, lb, gb,
                      lambda lr, gr, rows: pltpu.make_async_copy(ys_hbm.at[pl.ds(gr, rows), :],
                                                                 sbuf.at[dst_slot, pl.ds(lr, rows), :],
                                                                 sem.at[dst_slot]))

    @pl.when(i == 0)
    def _():
        sbuf[...] = jnp.zeros_like(sbuf)
        fetch(c8_ref, lb_ref, gb_ref, 0)

    @pl.when(i + 1 < n_t)
    def _():
        fetch(c8n_ref, lbn_ref, gbn_ref, 1 - slot)

    rows = pl.multiple_of(tot_ref[i], STRIP)
    pltpu.make_async_copy(ys_hbm.at[pl.ds(0, rows), :], sbuf.at[slot, pl.ds(0, rows), :], sem.at[slot]).wait()
    selw = _slot_select(ids_ref, lrank_ref, lbv_ref[0], [prob_ref[:, kq:kq + 1] for kq in range(TOP_K)])
    y = jnp.dot(selw, sbuf[slot].astype(BF16), preferred_element_type=F32)
    out = _rms(x2_ref[...] + y, g_ref[...])

    @pl.when(i < n_pt)
    def _():
        op_ref[...] = out

    @pl.when(i >= n_pt)
    def _():
        os_ref[...] = out


def _combine(tile_tot, c8, lbase, gbase, ys, x2, ids, lrank, prob, g_final, n_pt):
    n_t = c8.shape[0]
    last = n_t - 1
    cur = pl.BlockSpec((1, 1, N_EXPERTS), lambda i, *_: (i, 0, 0), memory_space=pltpu.SMEM)
    nxt = pl.BlockSpec((1, 1, N_EXPERTS), lambda i, *_: (jnp.minimum(i + 1, last), 0, 0), memory_space=pltpu.SMEM)
    row = lambda w: pl.BlockSpec((TOK_TILE, w), lambda i, *_: (i, 0))
    grid_spec = pltpu.PrefetchScalarGridSpec(
        num_scalar_prefetch=1,
        grid=(n_t,),
        in_specs=[cur, cur, cur, nxt, nxt, nxt,
                  pl.BlockSpec(memory_space=pl.ANY),
                  row(D_MODEL), row(TOP_K), row(TOP_K), row(TOP_K),
                  pl.BlockSpec((1, 1, N_EXPERTS), lambda i, *_: (i, 0, 0)),
                  pl.BlockSpec((1, D_MODEL), lambda i, *_: (0, 0))],
        out_specs=[pl.BlockSpec((TOK_TILE, D_MODEL), lambda i, *_: (jnp.minimum(i, n_pt - 1), 0)),
                   pl.BlockSpec((TOK_TILE, D_MODEL), lambda i, *_: (jnp.maximum(i - n_pt, 0), 0))],
        scratch_shapes=[pltpu.VMEM((2, SLOT_ROWS, D_MODEL), F32), pltpu.SemaphoreType.DMA((2,))],
    )
    return pl.pallas_call(
        functools.partial(_combine_kernel, n_pt=n_pt, n_t=n_t),
        grid_spec=grid_spec,
        out_shape=[jax.ShapeDtypeStruct((n_pt * TOK_TILE, D_MODEL), F32),
                   jax.ShapeDtypeStruct(((n_t - n_pt) * TOK_TILE, D_MODEL), F32)],
        compiler_params=_cparams(("arbitrary",)),
        name="combine",
    )(tile_tot, c8, lbase, gbase, c8, lbase, gbase, ys, x2, ids, lrank, prob, lbase.astype(F32), g_final)


def kernel(x_prompt, x_sample, mem_prompt, state_shift, state_wkv, state_conv, cache_mem_k, cache_mem_v,
           g_mix, w_in, mu_shift, w0, w_decay_up, a0, w_iclr_up, w_glora_up, k_k, k_a, r_k, gn_g, gn_b,
           dw_w, dw_b, cln_g, cln_b, w_out, g_xattn, g_mem, w_q, w_mk, w_mv, w_o,
           g_moe, w_router, b_router, w_moe_gate, b_moe_gate, w_moe_up, b_moe_up, w_moe_down, b_moe_down,
           g_final):
    bp, tp, _ = x_prompt.shape
    bs, ts, _ = x_sample.shape
    n_p, n_s = bp * tp, bs * ts
    n_all = n_p + n_s
    assert tp % WKV_BLOCK == 0 and tp % TOK_TILE == 0 and n_s == TOK_TILE
    assert ts <= SAMPLE_PAD and bs % SAMPLE_SEQS == 0 and bs % ATTN_SEQS == 0
    row2 = lambda a: a.reshape(1, -1)

    x_p = x_prompt.reshape(n_p, D_MODEL)
    x_s = x_sample.reshape(n_s, D_MODEL)
    p_rw, u = _inproj(x_p, x_s, row2(g_mix[0]), w_in[0].astype(BF16))

    head_idx = jnp.arange(RW_WIDTH) // RW_HEAD_DIM
    eseg = (head_idx[:, None] == head_idx[None, :]).astype(BF16)
    wts = (row2(mu_shift[0]), row2(w0[0]), w_decay_up[0].astype(BF16), row2(a0[0]),
           w_iclr_up[0].astype(BF16), w_glora_up[0].astype(BF16), row2(k_k[0]), row2(k_a[0]),
           row2(r_k[0]), row2(gn_g[0]), row2(gn_b[0]), eseg)
    conv_w = (dw_w[0], row2(dw_b[0]), row2(cln_g[0]), row2(cln_b[0]))
    n_tp = tp // WKV_BLOCK
    rw_p, wkv_p, cv_p = _mixer(p_rw, u, 0, jnp.zeros((bp, 1, RW_PROJ), F32),
                               jnp.zeros((bp, RW_HEADS, RW_HEAD_DIM, RW_HEAD_DIM), F32),
                               jnp.zeros((bp, CONV_PAD, CONV_WIDTH), F32), wts, conv_w,
                               batch=bp, n_seq=1, tt_seq=WKV_BLOCK, chunk=WKV_CHUNK, t_valid=tp, n_t=n_tp,
                               out_rows=n_p)
    pad_seq = lambda a: jnp.pad(a.reshape(bs, ts, -1), ((0, 0), (0, SAMPLE_PAD - ts), (0, 0))).reshape(
        bs * SAMPLE_PAD, -1)
    unpad_seq = lambda a: a.reshape(bs, SAMPLE_PAD, -1)[:, :ts].reshape(n_s, -1)
    p_rw_s = p_rw[n_p:].reshape(bs, ts, RW_PROJ)
    u_s = u[n_p:]
    hist_s = jnp.pad(state_conv[0], ((0, 0), (CONV_PAD - (CONV_K - 1), 0), (0, 0)))
    rw_s_pad, wkv_s, cv_s_pad = _mixer(pad_seq(p_rw_s), pad_seq(u_s), 0, state_shift[0].reshape(bs, 1, RW_PROJ),
                                       state_wkv[0], hist_s, wts, conv_w,
                                       batch=bs, n_seq=SAMPLE_SEQS, tt_seq=SAMPLE_PAD, chunk=SAMPLE_PAD,
                                       t_valid=ts, n_t=1, out_rows=bs * SAMPLE_PAD)
    rw_s, cv_s = unpad_seq(rw_s_pad), unpad_seq(cv_s_pad)

    x1, q = _mid1(x_p, x_s, rw_p, rw_s, cv_p, cv_s, w_out[0].astype(BF16), row2(g_xattn[0]), w_q[0].astype(BF16))

    mk_p, mv_p = _memkv(mem_prompt.reshape(bp * N_MEM, D_MODEL), row2(g_mem[0]),
                        w_mk[0].astype(BF16), w_mv[0].astype(BF16))
    mk_p = mk_p.reshape(bp, N_MEM, D_MODEL)
    mv_p = mv_p.reshape(bp, N_MEM, D_MODEL)
    o_p = _attn_rows(q, mk_p, mv_p, n_p, tp // TOK_TILE)
    o_s = unpad_seq(_attn_seq(pad_seq(q[n_p:]).reshape(bs, SAMPLE_PAD, D_MODEL), cache_mem_k, cache_mem_v))

    tri = (jnp.arange(TOK_TILE)[None, :] < jnp.arange(TOK_TILE)[:, None]).astype(BF16)
    x2, h3, ids, prob, lrank, cnt3, base3, tot = _mid2(x1, o_p, o_s, w_o[0].astype(BF16), row2(g_moe[0]),
                                                       w_router[0], row2(b_router[0]), tri)

    n_t = n_all // TOK_TILE
    cnt = cnt3.astype(I32)
    c8 = ((cnt + STRIP - 1) // STRIP) * STRIP
    lbase = jnp.cumsum(c8, axis=-1) - c8
    tile_tot = jnp.sum(c8, axis=(1, 2))
    used = tot[0].astype(I32)
    padded = ((used + FFN_TILE - 1) // FFN_TILE) * FFN_TILE
    ends = jnp.cumsum(padded)
    offs = ends - padded
    gbase = offs[None, None, :] + base3.astype(I32)
    tail_n = padded - used
    n_tiles = -(-(n_all * TOP_K + n_t * N_EXPERTS * (STRIP - 1)) // FFN_TILE) + N_EXPERTS
    n_used = (ends[-1] // FFN_TILE).reshape(1)
    tile_ids = jnp.minimum(jnp.arange(n_tiles, dtype=I32), n_used - 1)
    tile_expert = jnp.sum((ends[None, :] // FFN_TILE) <= tile_ids[:, None], axis=1).astype(I32)
    tile_expert = jnp.minimum(tile_expert, N_EXPERTS - 1)
    tile_row0 = jnp.arange(n_tiles, dtype=I32) * FFN_TILE
    tile_valid = jnp.clip((offs + used)[tile_expert] - tile_row0, 0, FFN_TILE)
    tile_valid = jnp.where(tile_row0 < ends[-1], tile_valid, 0).astype(I32)

    xs = _dispatch(tile_tot, tail_n, offs + used, n_used, c8, lbase, gbase, h3, ids, lrank, n_tiles)
    ys = _ffn(tile_expert, tile_valid, n_used, xs,
              w_moe_gate[0], b_moe_gate[0].reshape(N_EXPERTS, 1, D_MODEL),
              w_moe_up[0], b_moe_up[0].reshape(N_EXPERTS, 1, D_MODEL),
              w_moe_down[0], b_moe_down[0].reshape(N_EXPERTS, 1, D_MODEL))

    y_p, y_s = _combine(tile_tot, c8, lbase, gbase, ys, x2, ids, lrank, prob, row2(g_final), n_p // TOK_TILE)

    last_rows = lambda a, rows: jnp.stack([a[(b + 1) * tp - rows:(b + 1) * tp] for b in range(bp)])
    new_conv_s = jnp.concatenate([state_conv[0], u_s.reshape(bs, ts, CONV_WIDTH)], axis=1)[:, -(CONV_K - 1):]
    kv_shape = (1, bp, N_MEM, X_HEADS, X_HEAD_DIM)
    return (y_p.reshape(bp, tp, D_MODEL), y_s.reshape(bs, ts, D_MODEL),
            last_rows(p_rw, 1).reshape(1, bp, RW_PROJ), wkv_p[None],
            last_rows(u, CONV_K - 1)[None],
            mk_p.reshape(kv_shape), mv_p.reshape(kv_shape),
            p_rw_s[:, -1][None], wkv_s[None], new_conv_s[None])
```

```python
import functools

import jax
import jax.numpy as jnp
from jax import lax
from jax.experimental import pallas as pl
from jax.experimental.pallas import tpu as pltpu

F32 = jnp.float32
BF16 = jnp.bfloat16
I32 = jnp.int32

D_MODEL = 1024
RW_HEADS = 8
RW_HEAD_DIM = 64
RW_WIDTH = RW_HEADS * RW_HEAD_DIM
DECAY_LORA = 64
ICLR_LORA = 64
GATE_LORA = 128
RW_PROJ = 3 * RW_WIDTH + DECAY_LORA + ICLR_LORA + GATE_LORA
GN_EPS = 64e-5
CONV_WIDTH = D_MODEL - RW_WIDTH
CONV_K = 31
LN_EPS = 1e-5
IN_PROJ = RW_PROJ + 2 * CONV_WIDTH
N_MEM = 256
X_HEADS = 4
X_HEAD_DIM = D_MODEL // X_HEADS
ATTN_SCALE = X_HEAD_DIM ** -0.5
N_EXPERTS = 32
TOP_K = 4
SWIGLU_LIMIT = 7.0
SWIGLU_ALPHA = 1.702
NORM_EPS = 1e-5

TOK_TILE = 512
WKV_BLOCK = 256
WKV_CHUNK = 64
SAMPLE_PAD = 8
SAMPLE_SEQS = 4
ATTN_SEQS = 4
GROUP = 4
GW = GROUP * RW_HEAD_DIM
N_GROUPS = RW_HEADS // GROUP
CHAIN_BATCH = 8
FFN_TILE = 512
STRIP = 8
SLOT_ROWS = TOP_K * TOK_TILE + N_EXPERTS * STRIP
CONV_PAD = 32
VMEM_LIMIT = 56 * 1024 * 1024


def _cparams(sem):
    return pltpu.CompilerParams(dimension_semantics=sem, vmem_limit_bytes=VMEM_LIMIT)


def _dot(a, b):
    return jnp.dot(a.astype(BF16), b.astype(BF16), preferred_element_type=F32)


def _dot_nt(a, b):
    return lax.dot_general(a.astype(BF16), b.astype(BF16), (((1,), (1,)), ((), ())),
                           preferred_element_type=F32)


def _dot_tn(a, b):
    return lax.dot_general(a.astype(BF16), b.astype(BF16), (((0,), (0,)), ((), ())),
                           preferred_element_type=F32)


def _split2(x):
    hi = x.astype(BF16)
    lo = (x - hi.astype(F32)).astype(BF16)
    return hi, lo


def _split3(x):
    hi = x.astype(BF16)
    r1 = x - hi.astype(F32)
    mid = r1.astype(BF16)
    lo = (r1 - mid.astype(F32)).astype(BF16)
    return hi, mid, lo


def _segsum(x, eseg):
    hi, lo = _split2(x)
    return (jnp.dot(hi, eseg, preferred_element_type=F32)
            + jnp.dot(lo, eseg, preferred_element_type=F32))


def _rms(x, g):
    return x * lax.rsqrt(jnp.mean(x * x, axis=-1, keepdims=True) + NORM_EPS) * g


def _sigmoid(x):
    return 1.0 / (1.0 + jnp.exp(-x))


def _pick(first, a_ref, b_ref):
    return jnp.where(first, a_ref[...], b_ref[...])


def _two_source_specs(width, n_pt):
    return (pl.BlockSpec((TOK_TILE, width), lambda i: (jnp.minimum(i, n_pt - 1), 0)),
            pl.BlockSpec((TOK_TILE, width), lambda i: (jnp.maximum(i - n_pt, 0), 0)))


def _inproj_kernel(xp_ref, xs_ref, g_ref, w_ref, prw_ref, u_ref, *, n_pt):
    h = _rms(_pick(pl.program_id(0) < n_pt, xp_ref, xs_ref), g_ref[...])
    p = _dot(h, w_ref[...])
    prw_ref[...] = p[:, :RW_PROJ]
    u_ref[...] = p[:, RW_PROJ:RW_PROJ + CONV_WIDTH] * _sigmoid(p[:, RW_PROJ + CONV_WIDTH:])


def _inproj(x_p, x_s, g_mix, w_in_bf):
    n = x_p.shape[0] + x_s.shape[0]
    n_pt = x_p.shape[0] // TOK_TILE
    return pl.pallas_call(
        functools.partial(_inproj_kernel, n_pt=n_pt),
        grid=(n // TOK_TILE,),
        in_specs=[*_two_source_specs(D_MODEL, n_pt),
                  pl.BlockSpec((1, D_MODEL), lambda i: (0, 0)),
                  pl.BlockSpec((D_MODEL, IN_PROJ), lambda i: (0, 0))],
        out_specs=[pl.BlockSpec((TOK_TILE, RW_PROJ), lambda i: (i, 0)),
                   pl.BlockSpec((TOK_TILE, CONV_WIDTH), lambda i: (i, 0))],
        out_shape=[jax.ShapeDtypeStruct((n, RW_PROJ), F32),
                   jax.ShapeDtypeStruct((n, CONV_WIDTH), F32)],
        compiler_params=_cparams(("parallel",)),
        name="inproj",
    )(x_p, x_s, g_mix, w_in_bf)


def _bd(x, w):
    r, lanes = x.shape
    head = lax.broadcasted_iota(I32, (r, lanes), 1) >> (w.bit_length() - 1)
    return jnp.concatenate([jnp.where(head == h, x, 0.0) for h in range(GROUP)], axis=0).astype(BF16)


def _collapse(x):
    head = lax.broadcasted_iota(I32, (RW_HEAD_DIM, GW), 1) >> (RW_HEAD_DIM.bit_length() - 1)
    out = jnp.zeros((RW_HEAD_DIM, GW), F32)
    for h in range(GROUP):
        out = out + jnp.where(head == h, x[h * RW_HEAD_DIM:(h + 1) * RW_HEAD_DIM, :], 0.0)
    return out


def _conv_module(t, u_ref, hist_ref, cw_ref, cb_ref, cg_ref, cbeta_ref, cv_ref, ext_ref, sh_ref,
                 *, n_seq, tt_seq, n_t):
    first = CONV_PAD - (CONV_K - 1)
    span = tt_seq + CONV_PAD - 8
    for s in range(n_seq):
        rows = slice(s * tt_seq, (s + 1) * tt_seq)

        @pl.when(t == 0)
        def _():
            ext_ref[s, 0:8, :] = jnp.zeros((8, CONV_WIDTH), F32)
            ext_ref[s, first:CONV_PAD, :] = hist_ref[0, s]

        ext_ref[s, CONV_PAD:CONV_PAD + tt_seq, :] = u_ref[rows, :]
        for b in range(1, 8):
            sh_ref[b, 0:span, :] = ext_ref[s, pl.ds(b, span), :]
        acc = jnp.zeros((tt_seq, CONV_WIDTH), F32) + cb_ref[...]
        for j in range(CONV_K):
            a8, b = divmod(first + j, 8)
            win = ext_ref[s, pl.ds(8 * a8, tt_seq), :] if b == 0 else sh_ref[b, pl.ds(8 * a8, tt_seq), :]
            acc = acc + win * cw_ref[j:j + 1, :]
        mean = jnp.mean(acc, axis=-1, keepdims=True)
        cen = acc - mean
        var = jnp.mean(cen * cen, axis=-1, keepdims=True)
        cn = cen * lax.rsqrt(var + LN_EPS) * cg_ref[...] + cbeta_ref[...]
        cv_ref[rows, :] = cn * _sigmoid(cn)
        if n_t > 1:
            ext_ref[s, 0:CONV_PAD, :] = ext_ref[s, tt_seq:tt_seq + CONV_PAD, :]


def _wkv_kernel(p_ref, prev_ref, s0_ref, mu_ref, w0_ref, wdup_ref, a0_ref, aup_ref, gup_ref,
                kk_ref, ka_ref, rk_ref, gng_ref, gnb_ref, eseg_ref,
                u_ref, hist_ref, cw_ref, cb_ref, cg_ref, cbeta_ref,
                out_ref, sfin_ref, cv_ref,
                s_ref, last_ref, rt_ref, at_ref, bt_ref, kt_ref, v_ref, pc_ref, y_ref,
                r_ref, km_ref, g_ref, ext_ref, sh_ref, *, n_seq, tt_seq, chunk, t_valid, n_t):
    t = pl.program_id(1)
    tt = n_seq * tt_seq
    n_chunk = tt_seq // chunk
    n_fused = chunk.bit_length() - 2
    nd = RW_HEAD_DIM
    c1, c2, c3 = RW_WIDTH, 2 * RW_WIDTH, 3 * RW_WIDTH
    c4 = c3 + DECAY_LORA
    c5 = c4 + ICLR_LORA

    @pl.when(t == 0)
    def _():
        for s in range(n_seq):
            for h in range(RW_HEADS):
                g, hg = divmod(h, GROUP)
                s_ref[s, g, :, hg * nd:(hg + 1) * nd] = s0_ref[0, s, h]
        if n_seq == 1:
            last_ref[0:1, :] = prev_ref[0]

    _conv_module(t, u_ref, hist_ref, cw_ref, cb_ref, cg_ref, cbeta_ref, cv_ref, ext_ref, sh_ref,
                 n_seq=n_seq, tt_seq=tt_seq, n_t=n_t)

    p = p_ref[...]
    row = lax.broadcasted_iota(I32, (tt, 1), 0)
    seq_row = row & (tt_seq - 1)
    rolled = pltpu.roll(p, shift=1, axis=0)
    if n_seq == 1:
        ps = jnp.where(seq_row == 0, last_ref[0:1, :], rolled)
        last_ref[0:1, :] = p[tt - 1:tt, :]
    else:
        prev_rows = jnp.concatenate(
            [jnp.broadcast_to(prev_ref[q], (tt_seq, RW_PROJ)) for q in range(n_seq)], axis=0)
        ps = jnp.where(seq_row == 0, prev_rows, rolled)
    z = p + mu_ref[...] * (ps - p)
    r, k, v = z[:, :c1], z[:, c1:c2], z[:, c2:c3]
    wd, ad, gd = z[:, c3:c4], z[:, c4:c5], z[:, c5:]
    wpre = w0_ref[...] + _dot(jnp.tanh(wd), wdup_ref[...])
    neg = -wpre
    w = -(jnp.maximum(neg, 0.0) + jnp.log(1.0 + jnp.exp(-jnp.abs(neg)))) - 0.5
    a = _sigmoid(a0_ref[...] + _dot(ad, aup_ref[...]))
    g_ref[...] = _dot(_sigmoid(gd), gup_ref[...])
    eseg = eseg_ref[...]
    kk = k * kk_ref[...]
    kk = kk / jnp.maximum(jnp.sqrt(_segsum(kk * kk, eseg)), 1e-12)
    kmod = k * (1.0 + (a - 1.0) * ka_ref[...])
    logd = -jnp.exp(w)
    if t_valid < tt_seq * n_t:
        valid = (t * tt_seq + seq_row) < t_valid
        logd = jnp.where(valid, logd, 0.0)
        kk = jnp.where(valid, kk, 0.0)
        kmod = jnp.where(valid, kmod, 0.0)
        v = jnp.where(valid, v, 0.0)
    ri = lax.broadcasted_iota(I32, (tt, tt), 0)
    ci = lax.broadcasted_iota(I32, (tt, tt), 1)
    shift = chunk.bit_length() - 1
    ltri = jnp.where(ci <= ri, jnp.where((ri >> shift) == (ci >> shift), 1.0, 0.0), 0.0).astype(BF16)
    hi, mid, lo = _split3(logd)
    logp = (jnp.dot(ltri, hi, preferred_element_type=F32)
            + jnp.dot(ltri, mid, preferred_element_type=F32)
            + jnp.dot(ltri, lo, preferred_element_type=F32))
    pcum = jnp.exp(logp)
    pinv = jnp.exp(-logp)
    r_ref[...] = r
    km_ref[...] = kmod
    v_ref[...] = v
    pc_ref[...] = pcum
    rt_ref[...] = r * pcum
    at_ref[...] = -kk * jnp.exp(logp - logd)
    bt_ref[...] = kk * a * pinv
    kt_ref[...] = kmod * pinv

    cw = GROUP * chunk
    ti = lax.broadcasted_iota(I32, (chunk, cw), 0)
    si = lax.broadcasted_iota(I32, (chunk, cw), 1) & (chunk - 1)
    strict = si < ti
    incl = si <= ti
    eye_c = jnp.where(si == ti, 1.0, 0.0)
    ji = lax.broadcasted_iota(I32, (nd, GW), 0)
    jj = lax.broadcasted_iota(I32, (nd, GW), 1) & (nd - 1)
    eye_n = jnp.where(ji == jj, 1.0, 0.0)

    all_chains = [(s, c, g) for s in range(n_seq) for c in range(n_chunk) for g in range(N_GROUPS)]

    def blk(ref, s, c, g):
        r0 = s * tt_seq + c * chunk
        return ref[r0:r0 + chunk, g * GW:(g + 1) * GW]

    rp, yv, mm, gm = {}, {}, {}, {}

    def chunk_operators(chains):
        am = {q: blk(at_ref, *q) for q in chains}
        rm = {q: blk(rt_ref, *q) for q in chains}
        bm = {q: blk(bt_ref, *q) for q in chains}
        km = {q: blk(kt_ref, *q) for q in chains}
        vm = {q: blk(v_ref, *q) for q in chains}
        a_ab, a_ak, a_rb, a_rk = {}, {}, {}, {}
        for q in chains:
            ar = jnp.concatenate([am[q], rm[q]], axis=0)
            s1 = _dot_nt(ar, _bd(bm[q], nd))
            s2 = _dot_nt(ar, _bd(km[q], nd))
            a_ab[q] = jnp.where(strict, s1[:chunk], 0.0)
            a_rb[q] = jnp.where(incl, s1[chunk:], 0.0)
            a_ak[q] = jnp.where(strict, s2[:chunk], 0.0)
            a_rk[q] = jnp.where(incl, s2[chunk:], 0.0)
        tinv = {q: eye_c + a_ab[q] for q in chains}
        pw = {q: _dot(a_ab[q], _bd(a_ab[q], chunk)) for q in chains}
        for lvl in range(n_fused):
            last = lvl == n_fused - 1
            for q in chains:
                lhs = tinv[q] if last else jnp.concatenate([tinv[q], pw[q]], axis=0)
                zz = _dot(lhs, _bd(pw[q], chunk))
                tinv[q] = tinv[q] + zz[:chunk]
                if not last:
                    pw[q] = zz[chunk:]
        akv, rkv = {}, {}
        for q in chains:
            zv = _dot(jnp.concatenate([a_ak[q], a_rk[q]], axis=0), _bd(vm[q], nd))
            akv[q], rkv[q] = zv[:chunk], zv[chunk:]
        ap = {q: _dot(tinv[q], _bd(am[q], nd)) for q in chains}
        uv = {q: _dot(tinv[q], _bd(akv[q], nd)) for q in chains}
        for q in chains:
            rp[q] = rm[q] + _dot(a_rb[q], _bd(ap[q], nd))
            yv[q] = _dot(a_rb[q], _bd(uv[q], nd)) + rkv[q]
        for q in chains:
            s, c, g = q
            r_end = s * tt_seq + (c + 1) * chunk
            pc = pc_ref[r_end - 8:r_end, g * GW:(g + 1) * GW][7:8, :]
            cross = _dot_tn(jnp.concatenate([ap[q], uv[q]], axis=1), bm[q])
            vk = _dot_tn(vm[q], km[q])
            mm[q] = (eye_n + _collapse(cross[:GW])) * pc
            gm[q] = (_collapse(cross[GW:]) + _collapse(vk)) * pc

    for i0 in range(0, len(all_chains), CHAIN_BATCH):
        chunk_operators(all_chains[i0:i0 + CHAIN_BATCH])

    for s in range(n_seq):
        for g in range(N_GROUPS):
            st = s_ref[s, g]
            for c in range(n_chunk):
                q = (s, c, g)
                r0 = s * tt_seq + c * chunk
                y_ref[r0:r0 + chunk, g * GW:(g + 1) * GW] = _dot_nt(rp[q], _bd(st, nd)) + yv[q]
                st = _dot(st, _bd(mm[q], nd)) + gm[q]
            s_ref[s, g] = st

    y = y_ref[...]
    inv_n = 1.0 / RW_HEAD_DIM
    mean = _segsum(y, eseg) * inv_n
    yc = y - mean
    var = _segsum(yc * yc, eseg) * inv_n
    yn = yc * lax.rsqrt(var + GN_EPS) * gng_ref[...] + gnb_ref[...]
    bonus = _segsum(r_ref[...] * km_ref[...] * rk_ref[...], eseg) * v_ref[...]
    out_ref[...] = (yn + bonus) * g_ref[...]

    @pl.when(t == n_t - 1)
    def _():
        for s in range(n_seq):
            for h in range(RW_HEADS):
                g, hg = divmod(h, GROUP)
                sfin_ref[s, h] = s_ref[s, g, :, hg * nd:(hg + 1) * nd]


def _mixer(p_rows, u_rows, row_block0, prev, s0, hist, wts, conv_w, *, batch, n_seq, tt_seq, chunk, t_valid,
           n_t, out_rows):
    (mu, w0, wdup, a0, aup, gup, k_k, k_a, r_k, gn_g, gn_b, eseg) = wts
    dw_w, dw_b, ln_g, ln_b = conv_w
    tt = n_seq * tt_seq
    kern = functools.partial(_wkv_kernel, n_seq=n_seq, tt_seq=tt_seq, chunk=chunk, t_valid=t_valid, n_t=n_t)
    const = lambda shape: pl.BlockSpec(shape, lambda b, t: tuple(0 for _ in shape))
    row_out = lambda w: pl.BlockSpec((tt, w), lambda b, t: (b * n_t + t, 0))
    blk = pltpu.VMEM((tt, RW_WIDTH), F32)
    st_spec = pl.BlockSpec((n_seq, RW_HEADS, RW_HEAD_DIM, RW_HEAD_DIM), lambda b, t: (b, 0, 0, 0))
    return pl.pallas_call(
        kern,
        grid=(batch // n_seq, n_t),
        in_specs=[pl.BlockSpec((tt, RW_PROJ), lambda b, t: (row_block0 + b * n_t + t, 0)),
                  pl.BlockSpec((n_seq, 1, RW_PROJ), lambda b, t: (b, 0, 0)),
                  pl.BlockSpec((1, n_seq, RW_HEADS, RW_HEAD_DIM, RW_HEAD_DIM), lambda b, t: (0, b, 0, 0, 0)),
                  const((1, RW_PROJ)), const((1, RW_WIDTH)), const((DECAY_LORA, RW_WIDTH)),
                  const((1, RW_WIDTH)), const((ICLR_LORA, RW_WIDTH)), const((GATE_LORA, RW_WIDTH)),
                  const((1, RW_WIDTH)), const((1, RW_WIDTH)), const((1, RW_WIDTH)),
                  const((1, RW_WIDTH)), const((1, RW_WIDTH)), const((RW_WIDTH, RW_WIDTH)),
                  pl.BlockSpec((tt, CONV_WIDTH), lambda b, t: (row_block0 + b * n_t + t, 0)),
                  pl.BlockSpec((1, n_seq, CONV_K - 1, CONV_WIDTH), lambda b, t: (0, b, 0, 0)),
                  const((CONV_K, CONV_WIDTH)), const((1, CONV_WIDTH)),
                  const((1, CONV_WIDTH)), const((1, CONV_WIDTH))],
        out_specs=[row_out(RW_WIDTH), st_spec, row_out(CONV_WIDTH)],
        out_shape=[jax.ShapeDtypeStruct((out_rows, RW_WIDTH), F32),
                   jax.ShapeDtypeStruct((batch, RW_HEADS, RW_HEAD_DIM, RW_HEAD_DIM), F32),
                   jax.ShapeDtypeStruct((out_rows, CONV_WIDTH), F32)],
        scratch_shapes=[pltpu.VMEM((n_seq, N_GROUPS, RW_HEAD_DIM, GW), F32),
                        pltpu.VMEM((8, RW_PROJ), F32),
                        blk, blk, blk, blk, blk, blk, blk, blk, blk, blk,
                        pltpu.VMEM((n_seq, CONV_PAD + tt_seq, CONV_WIDTH), F32),
                        pltpu.VMEM((8, CONV_PAD + tt_seq, CONV_WIDTH), F32)],
        compiler_params=_cparams(("parallel", "arbitrary")),
        name="mixer",
    )(p_rows, prev, s0, mu, w0, wdup, a0, aup, gup, k_k, k_a, r_k, gn_g, gn_b, eseg,
      u_rows, hist, dw_w, dw_b, ln_g, ln_b)


def _mid1_kernel(xp_ref, xs_ref, rwp_ref, rws_ref, cvp_ref, cvs_ref, wo_ref, g_ref, wq_ref, x1_ref, q_ref,
                 *, n_pt):
    first = pl.program_id(0) < n_pt
    rw = _pick(first, rwp_ref, rws_ref)
    cv = _pick(first, cvp_ref, cvs_ref)
    mix = _dot(rw, wo_ref[:RW_WIDTH, :]) + _dot(cv, wo_ref[RW_WIDTH:, :])
    x1 = _pick(first, xp_ref, xs_ref) + mix
    x1_ref[...] = x1
    q = _dot(_rms(x1, g_ref[...]), wq_ref[...])
    q_ref[...] = (q * ATTN_SCALE).astype(BF16)


def _mid1(x_p, x_s, rw_p, rw_s, cv_p, cv_s, w_out_bf, g_x, w_q_bf):
    n = x_p.shape[0] + x_s.shape[0]
    n_pt = rw_p.shape[0] // TOK_TILE
    row = lambda w: pl.BlockSpec((TOK_TILE, w), lambda i: (i, 0))
    full = lambda a, b: pl.BlockSpec((a, b), lambda i: (0, 0))
    return pl.pallas_call(
        functools.partial(_mid1_kernel, n_pt=n_pt),
        grid=(n // TOK_TILE,),
        in_specs=[*_two_source_specs(D_MODEL, n_pt), *_two_source_specs(RW_WIDTH, n_pt),
                  *_two_source_specs(CONV_WIDTH, n_pt),
                  full(D_MODEL, D_MODEL), full(1, D_MODEL), full(D_MODEL, D_MODEL)],
        out_specs=[row(D_MODEL), row(D_MODEL)],
        out_shape=[jax.ShapeDtypeStruct((n, D_MODEL), F32), jax.ShapeDtypeStruct((n, D_MODEL), BF16)],
        compiler_params=_cparams(("parallel",)),
        name="mid1",
    )(x_p, x_s, rw_p, rw_s, cv_p, cv_s, w_out_bf, g_x, w_q_bf)


def _memkv_kernel(m_ref, g_ref, wk_ref, wv_ref, k_ref, v_ref):
    m = _rms(m_ref[...], g_ref[...])
    k_ref[...] = _dot(m, wk_ref[...])
    v_ref[...] = _dot(m, wv_ref[...])


def _memkv(mem, g_mem, w_mk_bf, w_mv_bf):
    n = mem.shape[0]
    tile = N_MEM
    row = pl.BlockSpec((tile, D_MODEL), lambda i: (i, 0))
    full = lambda a, b: pl.BlockSpec((a, b), lambda i: (0, 0))
    return pl.pallas_call(
        _memkv_kernel,
        grid=(n // tile,),
        in_specs=[row, full(1, D_MODEL), full(D_MODEL, D_MODEL), full(D_MODEL, D_MODEL)],
        out_specs=[row, row],
        out_shape=[jax.ShapeDtypeStruct((n, D_MODEL), F32), jax.ShapeDtypeStruct((n, D_MODEL), F32)],
        compiler_params=_cparams(("parallel",)),
        name="memkv",
    )(mem, g_mem, w_mk_bf, w_mv_bf)


def _attn_head(q_h, k_h, v_h):
    s = _dot_nt(q_h, k_h)
    e = jnp.exp(s - jnp.max(s, axis=-1, keepdims=True))
    pr = e / jnp.sum(e, axis=-1, keepdims=True)
    return _dot(pr, v_h).astype(BF16)


def _attn_rows_kernel(q_ref, k_ref, v_ref, o_ref):
    q = q_ref[...]
    for h in range(X_HEADS):
        hs = slice(h * X_HEAD_DIM, (h + 1) * X_HEAD_DIM)
        o_ref[:, hs] = _attn_head(q[:, hs], k_ref[0, :, hs], v_ref[0, :, hs])


def _attn_seq_kernel(q_ref, k_hbm, v_hbm, o_ref, kbuf, vbuf, sem, *, n_steps):
    i = pl.program_id(0)
    slot = lax.rem(i, 2)

    def copies(step, s):
        out = []
        for g in range(ATTN_SEQS):
            seq = step * ATTN_SEQS + g
            for h in range(X_HEADS):
                lanes = pl.ds(h * X_HEAD_DIM, X_HEAD_DIM)
                out.append(pltpu.make_async_copy(k_hbm.at[0, seq, :, h, :], kbuf.at[s, g, :, lanes], sem.at[s]))
                out.append(pltpu.make_async_copy(v_hbm.at[0, seq, :, h, :], vbuf.at[s, g, :, lanes], sem.at[s]))
        return out

    @pl.when(i == 0)
    def _():
        for cp in copies(0, 0):
            cp.start()

    @pl.when(i + 1 < n_steps)
    def _():
        for cp in copies(i + 1, 1 - slot):
            cp.start()

    for cp in copies(i, slot):
        cp.wait()
    head_of_lane = lax.broadcasted_iota(I32, (SAMPLE_PAD * X_HEADS, D_MODEL), 1) >> (X_HEAD_DIM.bit_length() - 1)
    head_of_row = lax.broadcasted_iota(I32, (SAMPLE_PAD * X_HEADS, D_MODEL), 0) >> (SAMPLE_PAD.bit_length() - 1)
    own = head_of_lane == head_of_row
    for g in range(ATTN_SEQS):
        q8 = q_ref[g].astype(F32)
        qbd = jnp.where(own, jnp.concatenate([q8] * X_HEADS, axis=0), 0.0)
        s = _dot_nt(qbd, kbuf[slot, g])
        e = jnp.exp(s - jnp.max(s, axis=-1, keepdims=True))
        pr = e / jnp.sum(e, axis=-1, keepdims=True)
        o_all = jnp.where(own, _dot(pr, vbuf[slot, g]), 0.0)
        o8 = o_all[0:SAMPLE_PAD]
        for h in range(1, X_HEADS):
            o8 = o8 + o_all[SAMPLE_PAD * h:SAMPLE_PAD * (h + 1)]
        o_ref[g] = o8.astype(BF16)


def _attn_rows(q, mk, mv, n_rows, tiles_per_seq):
    qspec = pl.BlockSpec((TOK_TILE, D_MODEL), lambda i, j: (i * tiles_per_seq + j, 0))
    kvspec = pl.BlockSpec((1, N_MEM, D_MODEL), lambda i, j: (i, 0, 0))
    return pl.pallas_call(
        _attn_rows_kernel,
        grid=(mk.shape[0], tiles_per_seq),
        in_specs=[qspec, kvspec, kvspec],
        out_specs=qspec,
        out_shape=jax.ShapeDtypeStruct((n_rows, D_MODEL), BF16),
        compiler_params=_cparams(("parallel", "parallel")),
        name="attn",
    )(q, mk, mv)


def _attn_seq(q3, mk, mv):
    b, t, _ = q3.shape
    qspec = pl.BlockSpec((ATTN_SEQS, t, D_MODEL), lambda i: (i, 0, 0))
    anyspec = pl.BlockSpec(memory_space=pl.ANY)
    head_buf = pltpu.VMEM((2, ATTN_SEQS, N_MEM, D_MODEL), F32)
    return pl.pallas_call(
        functools.partial(_attn_seq_kernel, n_steps=b // ATTN_SEQS),
        grid=(b // ATTN_SEQS,),
        in_specs=[qspec, anyspec, anyspec],
        out_specs=qspec,
        out_shape=jax.ShapeDtypeStruct((b, t, D_MODEL), BF16),
        scratch_shapes=[head_buf, head_buf, pltpu.SemaphoreType.DMA((2,))],
        compiler_params=_cparams(("arbitrary",)),
        name="attn",
    )(q3, mk, mv)


def _mid2_kernel(x1_ref, op_ref, os_ref, wo_ref, g_ref, wr_ref, br_ref, tri_ref,
                 x2_ref, h3_ref, ids_ref, prob_ref, lrank_ref, cnt_ref, base_ref, tot_ref, carry_ref, *, n_pt):
    i = pl.program_id(0)

    @pl.when(i == 0)
    def _():
        carry_ref[...] = jnp.zeros_like(carry_ref)

    x2 = x1_ref[...] + _dot(_pick(i < n_pt, op_ref, os_ref), wo_ref[...])
    x2_ref[...] = x2
    h3 = _rms(x2, g_ref[...])
    h3_ref[...] = h3.astype(BF16)
    h3_hi, h3_lo = _split2(h3)
    wr_hi, wr_lo = _split2(wr_ref[...])
    logits = (jnp.dot(h3_hi, wr_hi, preferred_element_type=F32)
              + jnp.dot(h3_hi, wr_lo, preferred_element_type=F32)
              + jnp.dot(h3_lo, wr_hi, preferred_element_type=F32)) + br_ref[...]
    n = logits.shape[0]
    lane = lax.broadcasted_iota(I32, (n, N_EXPERTS), 1)
    work = logits
    vals, ids = [], []
    for _ in range(TOP_K):
        m = jnp.max(work, axis=-1, keepdims=True)
        idx = jnp.min(jnp.where(work == m, lane, N_EXPERTS), axis=-1, keepdims=True)
        vals.append(m)
        ids.append(idx)
        work = jnp.where(lane == idx, -jnp.inf, work)
    exps = [jnp.exp(vk - vals[0]) for vk in vals]
    den = exps[0] + exps[1] + exps[2] + exps[3]
    mask = jnp.zeros((n, N_EXPERTS), F32)
    for idx in ids:
        mask = mask + jnp.where(lane == idx, 1.0, 0.0)
    lrank = jnp.dot(tri_ref[...], mask.astype(BF16), preferred_element_type=F32)
    cnt = jnp.sum(mask, axis=0, keepdims=True)
    cnt_ref[0] = cnt
    base_ref[0] = carry_ref[0:1, :]
    carry_ref[0:1, :] = carry_ref[0:1, :] + jnp.floor((cnt + (STRIP - 1)) * (1.0 / STRIP)) * STRIP
    tot_ref[...] = carry_ref[0:1, :]
    for kq in range(TOP_K):
        ids_ref[:, kq:kq + 1] = ids[kq]
        prob_ref[:, kq:kq + 1] = exps[kq] / den
        lrank_ref[:, kq:kq + 1] = jnp.sum(jnp.where(lane == ids[kq], lrank, 0.0), axis=-1, keepdims=True)


def _mid2(x1, o_p, o_s, w_o_bf, g_moe, w_router, b_router, tri):
    n = x1.shape[0]
    n_t = n // TOK_TILE
    n_pt = o_p.shape[0] // TOK_TILE
    row = lambda w: pl.BlockSpec((TOK_TILE, w), lambda i: (i, 0))
    full = lambda a, b: pl.BlockSpec((a, b), lambda i: (0, 0))
    per_tile = pl.BlockSpec((1, 1, N_EXPERTS), lambda i: (i, 0, 0))
    return pl.pallas_call(
        functools.partial(_mid2_kernel, n_pt=n_pt),
        grid=(n_t,),
        in_specs=[row(D_MODEL), *_two_source_specs(D_MODEL, n_pt), full(D_MODEL, D_MODEL), full(1, D_MODEL),
                  full(D_MODEL, N_EXPERTS), full(1, N_EXPERTS), full(TOK_TILE, TOK_TILE)],
        out_specs=[row(D_MODEL), row(D_MODEL), row(TOP_K), row(TOP_K), row(TOP_K), per_tile, per_tile,
                   full(1, N_EXPERTS)],
        out_shape=[jax.ShapeDtypeStruct((n, D_MODEL), F32), jax.ShapeDtypeStruct((n, D_MODEL), BF16),
                   jax.ShapeDtypeStruct((n, TOP_K), I32), jax.ShapeDtypeStruct((n, TOP_K), F32),
                   jax.ShapeDtypeStruct((n, TOP_K), F32),
                   jax.ShapeDtypeStruct((n_t, 1, N_EXPERTS), F32), jax.ShapeDtypeStruct((n_t, 1, N_EXPERTS), F32),
                   jax.ShapeDtypeStruct((1, N_EXPERTS), F32)],
        scratch_shapes=[pltpu.VMEM((8, N_EXPERTS), F32)],
        compiler_params=_cparams(("arbitrary",)),
        name="mid2",
    )(x1, o_p, o_s, w_o_bf, g_moe, w_router, b_router, tri)


def _slot_select(ids_ref, lrank_ref, lbase_v, weights):
    n = ids_ref.shape[0]
    lane = lax.broadcasted_iota(I32, (n, N_EXPERTS), 1)
    col = lax.broadcasted_iota(I32, (n, SLOT_ROWS), 1)
    sel = jnp.zeros((n, SLOT_ROWS), F32)
    for kq in range(TOP_K):
        base = jnp.sum(jnp.where(lane == ids_ref[:, kq:kq + 1], lbase_v, 0.0), axis=-1, keepdims=True)
        dest = (base + lrank_ref[:, kq:kq + 1]).astype(I32)
        sel = jnp.where(col == dest, weights[kq], sel)
    return sel.astype(BF16)


def _strip_copies(c8_ref, lb_ref, gb_ref, make):
    for e in range(N_EXPERTS):
        rows = c8_ref[0, 0, e]

        @pl.when(rows > 0)
        def _():
            make(pl.multiple_of(lb_ref[0, 0, e], STRIP), pl.multiple_of(gb_ref[0, 0, e], STRIP),
                 pl.multiple_of(rows, STRIP)).start()


def _dispatch_kernel(tot_ref, tailn_ref, tailo_ref, nt_ref,
                     c8_ref, lb_ref, gb_ref, h3_ref, ids_ref, lrank_ref, lbv_ref,
                     xs_hbm, cbuf, zbuf, sem, *, n_t, n_tiles):
    i = pl.program_id(0)
    slot = lax.rem(i, 2)
    sel = _slot_select(ids_ref, lrank_ref, lbv_ref[0], [1.0] * TOP_K)
    cbuf[slot] = _dot_tn(sel, h3_ref[...])
    _strip_copies(c8_ref, lb_ref, gb_ref,
                  lambda lr, gr, rows: pltpu.make_async_copy(cbuf.at[slot, pl.ds(lr, rows), :],
                                                             xs_hbm.at[pl.ds(gr, rows), :], sem.at[slot]))

    def wait_rows(s, rows):
        rows = pl.multiple_of(rows, STRIP)
        pltpu.make_async_copy(cbuf.at[s, pl.ds(0, rows), :], xs_hbm.at[pl.ds(0, rows), :], sem.at[s]).wait()

    @pl.when(i > 0)
    def _():
        wait_rows(1 - slot, tot_ref[jnp.maximum(i - 1, 0)])

    @pl.when(i == n_t - 1)
    def _():
        wait_rows(slot, tot_ref[i])
        zbuf[...] = jnp.zeros_like(zbuf)
        for e in range(N_EXPERTS):
            rows = pl.multiple_of(tailn_ref[e], STRIP)

            @pl.when(rows > 0)
            def _():
                pltpu.make_async_copy(zbuf.at[pl.ds(0, rows), :],
                                      xs_hbm.at[pl.ds(pl.multiple_of(tailo_ref[e], STRIP), rows), :],
                                      sem.at[2]).start()

        for e in range(N_EXPERTS):
            rows = pl.multiple_of(tailn_ref[e], STRIP)

            @pl.when(rows > 0)
            def _():
                pltpu.make_async_copy(zbuf.at[pl.ds(0, rows), :], xs_hbm.at[pl.ds(0, rows), :], sem.at[2]).wait()

        def tile_copy(j):
            return pltpu.make_async_copy(
                zbuf, xs_hbm.at[pl.ds(pl.multiple_of(j * FFN_TILE, FFN_TILE), FFN_TILE), :], sem.at[2])

        def start_tile(j, c):
            tile_copy(j).start()
            return c

        def wait_tile(j, c):
            tile_copy(j).wait()
            return c

        lax.fori_loop(nt_ref[0], n_tiles, start_tile, 0)
        lax.fori_loop(nt_ref[0], n_tiles, wait_tile, 0)


def _dispatch(tile_tot, tail_n, tail_off, n_used, c8, lbase, gbase, h3, ids, lrank, n_tiles):
    n_t = c8.shape[0]
    smem = pl.BlockSpec((1, 1, N_EXPERTS), lambda i, *_: (i, 0, 0), memory_space=pltpu.SMEM)
    row = lambda w: pl.BlockSpec((TOK_TILE, w), lambda i, *_: (i, 0))
    grid_spec = pltpu.PrefetchScalarGridSpec(
        num_scalar_prefetch=4,
        grid=(n_t,),
        in_specs=[smem, smem, smem, row(D_MODEL), row(TOP_K), row(TOP_K),
                  pl.BlockSpec((1, 1, N_EXPERTS), lambda i, *_: (i, 0, 0))],
        out_specs=pl.BlockSpec(memory_space=pl.ANY),
        scratch_shapes=[pltpu.VMEM((2, SLOT_ROWS, D_MODEL), F32), pltpu.VMEM((FFN_TILE, D_MODEL), F32),
                        pltpu.SemaphoreType.DMA((3,))],
    )
    return pl.pallas_call(
        functools.partial(_dispatch_kernel, n_t=n_t, n_tiles=n_tiles),
        grid_spec=grid_spec,
        out_shape=jax.ShapeDtypeStruct((n_tiles * FFN_TILE, D_MODEL), F32),
        compiler_params=_cparams(("arbitrary",)),
        name="dispatch",
    )(tile_tot, tail_n, tail_off, n_used, c8, lbase, gbase, h3, ids, lrank, lbase.astype(F32))


def _ffn_kernel(te_ref, ne_ref, nt_ref, xs_ref, wg_hbm, bg_ref, wu_hbm, bu_ref, wd_hbm, bd_ref, y_ref,
                wf32, wbf, turn, sem):
    i = pl.program_id(0)
    n_used = nt_ref[0]
    e = te_ref[i]

    def fetch(expert, s):
        return [pltpu.make_async_copy(w_hbm.at[0, expert], wf32.at[s, j], sem.at[s])
                for j, w_hbm in enumerate((wg_hbm, wu_hbm, wd_hbm))]

    @pl.when(i == 0)
    def _():
        turn[0] = 0
        for cp in fetch(e, 0):
            cp.start()

    @pl.when(jnp.logical_and(i < n_used, jnp.logical_or(i == 0, e != te_ref[jnp.maximum(i - 1, 0)])))
    def _():
        s = turn[0]
        for cp in fetch(e, s):
            cp.wait()
        nxt = i + ne_ref[e]

        @pl.when(nxt < n_used)
        def _():
            for cp in fetch(te_ref[jnp.minimum(nxt, te_ref.shape[0] - 1)], 1 - s):
                cp.start()

        for j in range(3):
            wbf[j] = wf32[s, j].astype(BF16)
        turn[0] = 1 - s

    @pl.when(i < n_used)
    def _():
        x = xs_ref[...].astype(BF16)
        gate = jnp.minimum(jnp.dot(x, wbf[0], preferred_element_type=F32) + bg_ref[0, 0], SWIGLU_LIMIT)
        up = jnp.clip(jnp.dot(x, wbf[1], preferred_element_type=F32) + bu_ref[0, 0],
                      -SWIGLU_LIMIT, SWIGLU_LIMIT)
        hid = (up + 1.0) * gate * _sigmoid(gate * SWIGLU_ALPHA)
        y_ref[...] = jnp.dot(hid.astype(BF16), wbf[2], preferred_element_type=F32) + bd_ref[0, 0]

    @pl.when(i >= n_used)
    def _():
        y_ref[...] = jnp.zeros_like(y_ref)


def _ffn(tile_expert, expert_tiles, n_used, xs, wg, bg, wu, bu, wd, bd):
    n_tiles = xs.shape[0] // FFN_TILE
    anyspec = pl.BlockSpec(memory_space=pl.ANY)
    bspec = pl.BlockSpec((1, 1, 1, D_MODEL), lambda i, te, ne, nt: (0, te[i], 0, 0))
    grid_spec = pltpu.PrefetchScalarGridSpec(
        num_scalar_prefetch=3,
        grid=(n_tiles,),
        in_specs=[pl.BlockSpec((FFN_TILE, D_MODEL), lambda i, te, ne, nt: (jnp.minimum(i, nt[0] - 1), 0)),
                  anyspec, bspec, anyspec, bspec, anyspec, bspec],
        out_specs=pl.BlockSpec((FFN_TILE, D_MODEL), lambda i, te, ne, nt: (i, 0)),
        scratch_shapes=[pltpu.VMEM((2, 3, D_MODEL, D_MODEL), F32), pltpu.VMEM((3, D_MODEL, D_MODEL), BF16),
                        pltpu.SMEM((1,), I32), pltpu.SemaphoreType.DMA((2,))],
    )
    return pl.pallas_call(
        _ffn_kernel,
        grid_spec=grid_spec,
        out_shape=jax.ShapeDtypeStruct((n_tiles * FFN_TILE, D_MODEL), F32),
        compiler_params=_cparams(("arbitrary",)),
        name="ffn",
    )(tile_expert, expert_tiles, n_used, xs, wg, bg, wu, bu, wd, bd)


def _combine_kernel(tot_ref, c8_ref, lb_ref, gb_ref, c8n_ref, lbn_ref, gbn_ref,
                    ys_hbm, x2_ref, ids_ref, lrank_ref, prob_ref, lbv_ref, g_ref,
                    op_ref, os_ref, sbuf, sem, *, n_pt, n_t):
    i = pl.program_id(0)
    slot = lax.rem(i, 2)

    def fetch(c8, lb, gb, dst_slot):
        _strip_copies(c8, lb, gb,
                      lambda lr, gr, rows: pltpu.make_async_copy(ys_hbm.at[pl.ds(gr, rows), :],
                                                                 sbuf.at[dst_slot, pl.ds(lr, rows), :],
                                                                 sem.at[dst_slot]))

    @pl.when(i == 0)
    def _():
        sbuf[...] = jnp.zeros_like(sbuf)
        fetch(c8_ref, lb_ref, gb_ref, 0)

    @pl.when(i + 1 < n_t)
    def _():
        fetch(c8n_ref, lbn_ref, gbn_ref, 1 - slot)

    rows = pl.multiple_of(tot_ref[i], STRIP)
    pltpu.make_async_copy(ys_hbm.at[pl.ds(0, rows), :], sbuf.at[slot, pl.ds(0, rows), :], sem.at[slot]).wait()
    selw = _slot_select(ids_ref, lrank_ref, lbv_ref[0], [prob_ref[:, kq:kq + 1] for kq in range(TOP_K)])
    y = jnp.dot(selw, sbuf[slot].astype(BF16), preferred_element_type=F32)
    out = _rms(x2_ref[...] + y, g_ref[...])

    @pl.when(i < n_pt)
    def _():
        op_ref[...] = out

    @pl.when(i >= n_pt)
    def _():
        os_ref[...] = out


def _combine(tile_tot, c8, lbase, gbase, ys, x2, ids, lrank, prob, g_final, n_pt):
    n_t = c8.shape[0]
    last = n_t - 1
    cur = pl.BlockSpec((1, 1, N_EXPERTS), lambda i, *_: (i, 0, 0), memory_space=pltpu.SMEM)
    nxt = pl.BlockSpec((1, 1, N_EXPERTS), lambda i, *_: (jnp.minimum(i + 1, last), 0, 0), memory_space=pltpu.SMEM)
    row = lambda w: pl.BlockSpec((TOK_TILE, w), lambda i, *_: (i, 0))
    grid_spec = pltpu.PrefetchScalarGridSpec(
        num_scalar_prefetch=1,
        grid=(n_t,),
        in_specs=[cur, cur, cur, nxt, nxt, nxt,
                  pl.BlockSpec(memory_space=pl.ANY),
                  row(D_MODEL), row(TOP_K), row(TOP_K), row(TOP_K),
                  pl.BlockSpec((1, 1, N_EXPERTS), lambda i, *_: (i, 0, 0)),
                  pl.BlockSpec((1, D_MODEL), lambda i, *_: (0, 0))],
        out_specs=[pl.BlockSpec((TOK_TILE, D_MODEL), lambda i, *_: (jnp.minimum(i, n_pt - 1), 0)),
                   pl.BlockSpec((TOK_TILE, D_MODEL), lambda i, *_: (jnp.maximum(i - n_pt, 0), 0))],
        scratch_shapes=[pltpu.VMEM((2, SLOT_ROWS, D_MODEL), F32), pltpu.SemaphoreType.DMA((2,))],
    )
    return pl.pallas_call(
        functools.partial(_combine_kernel, n_pt=n_pt, n_t=n_t),
        grid_spec=grid_spec,
        out_shape=[jax.ShapeDtypeStruct((n_pt * TOK_TILE, D_MODEL), F32),
                   jax.ShapeDtypeStruct(((n_t - n_pt) * TOK_TILE, D_MODEL), F32)],
        compiler_params=_cparams(("arbitrary",)),
        name="combine",
    )(tile_tot, c8, lbase, gbase, c8, lbase, gbase, ys, x2, ids, lrank, prob, lbase.astype(F32), g_final)


def kernel(x_prompt, x_sample, mem_prompt, state_shift, state_wkv, state_conv, cache_mem_k, cache_mem_v,
           g_mix, w_in, mu_shift, w0, w_decay_up, a0, w_iclr_up, w_glora_up, k_k, k_a, r_k, gn_g, gn_b,
           dw_w, dw_b, cln_g, cln_b, w_out, g_xattn, g_mem, w_q, w_mk, w_mv, w_o,
           g_moe, w_router, b_router, w_moe_gate, b_moe_gate, w_moe_up, b_moe_up, w_moe_down, b_moe_down,
           g_final):
    bp, tp, _ = x_prompt.shape
    bs, ts, _ = x_sample.shape
    n_p, n_s = bp * tp, bs * ts
    n_all = n_p + n_s
    assert tp % WKV_BLOCK == 0 and tp % TOK_TILE == 0 and n_s == TOK_TILE
    assert ts <= SAMPLE_PAD and bs % SAMPLE_SEQS == 0 and bs % ATTN_SEQS == 0
    row2 = lambda a: a.reshape(1, -1)

    x_p = x_prompt.reshape(n_p, D_MODEL)
    x_s = x_sample.reshape(n_s, D_MODEL)
    p_rw, u = _inproj(x_p, x_s, row2(g_mix[0]), w_in[0].astype(BF16))

    head_idx = jnp.arange(RW_WIDTH) // RW_HEAD_DIM
    eseg = (head_idx[:, None] == head_idx[None, :]).astype(BF16)
    wts = (row2(mu_shift[0]), row2(w0[0]), w_decay_up[0].astype(BF16), row2(a0[0]),
           w_iclr_up[0].astype(BF16), w_glora_up[0].astype(BF16), row2(k_k[0]), row2(k_a[0]),
           row2(r_k[0]), row2(gn_g[0]), row2(gn_b[0]), eseg)
    conv_w = (dw_w[0], row2(dw_b[0]), row2(cln_g[0]), row2(cln_b[0]))
    n_tp = tp // WKV_BLOCK
    rw_p, wkv_p, cv_p = _mixer(p_rw, u, 0, jnp.zeros((bp, 1, RW_PROJ), F32),
                               jnp.zeros((1, bp, RW_HEADS, RW_HEAD_DIM, RW_HEAD_DIM), F32),
                               jnp.zeros((1, bp, CONV_K - 1, CONV_WIDTH), F32), wts, conv_w,
                               batch=bp, n_seq=1, tt_seq=WKV_BLOCK, chunk=WKV_CHUNK, t_valid=tp, n_t=n_tp,
                               out_rows=n_p)
    pad_seq = lambda a: jnp.pad(a.reshape(bs, ts, -1), ((0, 0), (0, SAMPLE_PAD - ts), (0, 0))).reshape(
        bs * SAMPLE_PAD, -1)
    unpad_seq = lambda a: a.reshape(bs, SAMPLE_PAD, -1)[:, :ts].reshape(n_s, -1)
    p_rw_s = p_rw[n_p:].reshape(bs, ts, RW_PROJ)
    u_s = u[n_p:]
    rw_s_pad, wkv_s, cv_s_pad = _mixer(pad_seq(p_rw_s), pad_seq(u_s), 0, state_shift[0].reshape(bs, 1, RW_PROJ),
                                       state_wkv, state_conv, wts, conv_w,
                                       batch=bs, n_seq=SAMPLE_SEQS, tt_seq=SAMPLE_PAD, chunk=SAMPLE_PAD,
                                       t_valid=ts, n_t=1, out_rows=bs * SAMPLE_PAD)
    rw_s, cv_s = unpad_seq(rw_s_pad), unpad_seq(cv_s_pad)

    x1, q = _mid1(x_p, x_s, rw_p, rw_s, cv_p, cv_s, w_out[0].astype(BF16), row2(g_xattn[0]), w_q[0].astype(BF16))

    mk_p, mv_p = _memkv(mem_prompt.reshape(bp * N_MEM, D_MODEL), row2(g_mem[0]),
                        w_mk[0].astype(BF16), w_mv[0].astype(BF16))
    mk_p = mk_p.reshape(bp, N_MEM, D_MODEL)
    mv_p = mv_p.reshape(bp, N_MEM, D_MODEL)
    o_p = _attn_rows(q, mk_p, mv_p, n_p, tp // TOK_TILE)
    o_s = unpad_seq(_attn_seq(pad_seq(q[n_p:]).reshape(bs, SAMPLE_PAD, D_MODEL), cache_mem_k, cache_mem_v))

    tri = (jnp.arange(TOK_TILE)[None, :] < jnp.arange(TOK_TILE)[:, None]).astype(BF16)
    x2, h3, ids, prob, lrank, cnt3, base3, tot = _mid2(x1, o_p, o_s, w_o[0].astype(BF16), row2(g_moe[0]),
                                                       w_router[0], row2(b_router[0]), tri)

    n_t = n_all // TOK_TILE
    cnt = cnt3.astype(I32)
    c8 = ((cnt + STRIP - 1) // STRIP) * STRIP
    lbase = jnp.cumsum(c8, axis=-1) - c8
    tile_tot = jnp.sum(c8, axis=(1, 2))
    used = tot[0].astype(I32)
    padded = ((used + FFN_TILE - 1) // FFN_TILE) * FFN_TILE
    ends = jnp.cumsum(padded)
    offs = ends - padded
    gbase = offs[None, None, :] + base3.astype(I32)
    tail_n = padded - used
    n_tiles = -(-(n_all * TOP_K + n_t * N_EXPERTS * (STRIP - 1)) // FFN_TILE) + N_EXPERTS
    n_used = (ends[-1] // FFN_TILE).reshape(1)
    tile_ids = jnp.minimum(jnp.arange(n_tiles, dtype=I32), n_used - 1)
    tile_expert = jnp.sum((ends[None, :] // FFN_TILE) <= tile_ids[:, None], axis=1).astype(I32)
    tile_expert = jnp.minimum(tile_expert, N_EXPERTS - 1)

    xs = _dispatch(tile_tot, tail_n, offs + used, n_used, c8, lbase, gbase, h3, ids, lrank, n_tiles)
    bias4 = lambda b: b.reshape(1, N_EXPERTS, 1, D_MODEL)
    ys = _ffn(tile_expert, padded // FFN_TILE, n_used, xs, w_moe_gate, bias4(b_moe_gate),
              w_moe_up, bias4(b_moe_up), w_moe_down, bias4(b_moe_down))

    y_p, y_s = _combine(tile_tot, c8, lbase, gbase, ys, x2, ids, lrank, prob, row2(g_final), n_p // TOK_TILE)

    last_rows = lambda a, rows: jnp.stack([a[(b + 1) * tp - rows:(b + 1) * tp] for b in range(bp)])
    new_conv_s = jnp.concatenate([state_conv[0], u_s.reshape(bs, ts, CONV_WIDTH)], axis=1)[:, -(CONV_K - 1):]
    kv_shape = (1, bp, N_MEM, X_HEADS, X_HEAD_DIM)
    return (y_p.reshape(bp, tp, D_MODEL), y_s.reshape(bs, ts, D_MODEL),
            last_rows(p_rw, 1).reshape(1, bp, RW_PROJ), wkv_p[None],
            last_rows(u, CONV_K - 1)[None],
            mk_p.reshape(kv_shape), mv_p.reshape(kv_shape),
            p_rw_s[:, -1][None], wkv_s[None], new_conv_s[None])
```

```python
import functools

import jax
import jax.numpy as jnp
from jax import lax
from jax.experimental import pallas as pl
from jax.experimental.pallas import tpu as pltpu

F32 = jnp.float32
BF16 = jnp.bfloat16
I32 = jnp.int32

D_MODEL = 1024
RW_HEADS = 8
RW_HEAD_DIM = 64
RW_WIDTH = RW_HEADS * RW_HEAD_DIM
DECAY_LORA = 64
ICLR_LORA = 64
GATE_LORA = 128
RW_PROJ = 3 * RW_WIDTH + DECAY_LORA + ICLR_LORA + GATE_LORA
GN_EPS = 64e-5
CONV_WIDTH = D_MODEL - RW_WIDTH
CONV_K = 31
LN_EPS = 1e-5
IN_PROJ = RW_PROJ + 2 * CONV_WIDTH
N_MEM = 256
X_HEADS = 4
X_HEAD_DIM = D_MODEL // X_HEADS
ATTN_SCALE = X_HEAD_DIM ** -0.5
N_EXPERTS = 32
TOP_K = 4
SWIGLU_LIMIT = 7.0
SWIGLU_ALPHA = 1.702
NORM_EPS = 1e-5

TOK_TILE = 512
WKV_BLOCK = 256
WKV_CHUNK = 64
SAMPLE_PAD = 8
SAMPLE_SEQS = 8
ATTN_SEQS = 4
GROUP = 4
GW = GROUP * RW_HEAD_DIM
N_GROUPS = RW_HEADS // GROUP
CHAIN_BATCH = 16
FFN_TILE = 512
STRIP = 8
SLOT_ROWS = TOP_K * TOK_TILE + N_EXPERTS * STRIP
CONV_PAD = 32
VMEM_LIMIT = 56 * 1024 * 1024


def _cparams(sem):
    return pltpu.CompilerParams(dimension_semantics=sem, vmem_limit_bytes=VMEM_LIMIT)


def _dot(a, b):
    return jnp.dot(a.astype(BF16), b.astype(BF16), preferred_element_type=F32)


def _dot_nt(a, b):
    return lax.dot_general(a.astype(BF16), b.astype(BF16), (((1,), (1,)), ((), ())),
                           preferred_element_type=F32)


def _dot_tn(a, b):
    return lax.dot_general(a.astype(BF16), b.astype(BF16), (((0,), (0,)), ((), ())),
                           preferred_element_type=F32)


def _split2(x):
    hi = x.astype(BF16)
    lo = (x - hi.astype(F32)).astype(BF16)
    return hi, lo


def _split3(x):
    hi = x.astype(BF16)
    r1 = x - hi.astype(F32)
    mid = r1.astype(BF16)
    lo = (r1 - mid.astype(F32)).astype(BF16)
    return hi, mid, lo


def _segsum(x, eseg):
    hi, lo = _split2(x)
    return (jnp.dot(hi, eseg, preferred_element_type=F32)
            + jnp.dot(lo, eseg, preferred_element_type=F32))


def _rms(x, g):
    return x * lax.rsqrt(jnp.mean(x * x, axis=-1, keepdims=True) + NORM_EPS) * g


def _sigmoid(x):
    return 1.0 / (1.0 + jnp.exp(-x))


def _pick(first, a_ref, b_ref):
    return jnp.where(first, a_ref[...], b_ref[...])


def _two_source_specs(width, n_pt):
    return (pl.BlockSpec((TOK_TILE, width), lambda i: (jnp.minimum(i, n_pt - 1), 0)),
            pl.BlockSpec((TOK_TILE, width), lambda i: (jnp.maximum(i - n_pt, 0), 0)))


def _inproj_kernel(xp_ref, xs_ref, g_ref, w_ref, prw_ref, u_ref, *, n_pt):
    h = _rms(_pick(pl.program_id(0) < n_pt, xp_ref, xs_ref), g_ref[...])
    p = _dot(h, w_ref[...])
    prw_ref[...] = p[:, :RW_PROJ]
    u_ref[...] = p[:, RW_PROJ:RW_PROJ + CONV_WIDTH] * _sigmoid(p[:, RW_PROJ + CONV_WIDTH:])


def _inproj(x_p, x_s, g_mix, w_in_bf):
    n = x_p.shape[0] + x_s.shape[0]
    n_pt = x_p.shape[0] // TOK_TILE
    return pl.pallas_call(
        functools.partial(_inproj_kernel, n_pt=n_pt),
        grid=(n // TOK_TILE,),
        in_specs=[*_two_source_specs(D_MODEL, n_pt),
                  pl.BlockSpec((1, D_MODEL), lambda i: (0, 0)),
                  pl.BlockSpec((D_MODEL, IN_PROJ), lambda i: (0, 0))],
        out_specs=[pl.BlockSpec((TOK_TILE, RW_PROJ), lambda i: (i, 0)),
                   pl.BlockSpec((TOK_TILE, CONV_WIDTH), lambda i: (i, 0))],
        out_shape=[jax.ShapeDtypeStruct((n, RW_PROJ), F32),
                   jax.ShapeDtypeStruct((n, CONV_WIDTH), F32)],
        compiler_params=_cparams(("parallel",)),
        name="inproj",
    )(x_p, x_s, g_mix, w_in_bf)


def _bd(x, w, masks):
    r, lanes = x.shape
    key = (r, lanes, w)
    if key not in masks:
        head = lax.broadcasted_iota(I32, (r, lanes), 1) >> (w.bit_length() - 1)
        masks[key] = [head == h for h in range(GROUP)]
    return jnp.concatenate([jnp.where(m, x, 0.0) for m in masks[key]], axis=0).astype(BF16)


def _collapse(x):
    head = lax.broadcasted_iota(I32, (RW_HEAD_DIM, GW), 1) >> (RW_HEAD_DIM.bit_length() - 1)
    out = jnp.zeros((RW_HEAD_DIM, GW), F32)
    for h in range(GROUP):
        out = out + jnp.where(head == h, x[h * RW_HEAD_DIM:(h + 1) * RW_HEAD_DIM, :], 0.0)
    return out


def _conv_module(t, u_ref, hist_ref, cw_ref, cb_ref, cg_ref, cbeta_ref, cv_ref, ext_ref, sh_ref,
                 *, n_seq, tt_seq, n_t):
    first = CONV_PAD - (CONV_K - 1)
    span = tt_seq + CONV_PAD - 8
    for s in range(n_seq):
        rows = slice(s * tt_seq, (s + 1) * tt_seq)

        @pl.when(t == 0)
        def _():
            ext_ref[s, 0:8, :] = jnp.zeros((8, CONV_WIDTH), F32)
            ext_ref[s, first:CONV_PAD, :] = hist_ref[0, s]

        ext_ref[s, CONV_PAD:CONV_PAD + tt_seq, :] = u_ref[rows, :]
        for b in range(1, 8):
            sh_ref[b, 0:span, :] = ext_ref[s, pl.ds(b, span), :]
        acc = jnp.zeros((tt_seq, CONV_WIDTH), F32) + cb_ref[...]
        for j in range(CONV_K):
            a8, b = divmod(first + j, 8)
            win = ext_ref[s, pl.ds(8 * a8, tt_seq), :] if b == 0 else sh_ref[b, pl.ds(8 * a8, tt_seq), :]
            acc = acc + win * cw_ref[j:j + 1, :]
        mean = jnp.mean(acc, axis=-1, keepdims=True)
        cen = acc - mean
        var = jnp.mean(cen * cen, axis=-1, keepdims=True)
        cn = cen * lax.rsqrt(var + LN_EPS) * cg_ref[...] + cbeta_ref[...]
        cv_ref[rows, :] = cn * _sigmoid(cn)
        if n_t > 1:
            ext_ref[s, 0:CONV_PAD, :] = ext_ref[s, tt_seq:tt_seq + CONV_PAD, :]


def _wkv_kernel(p_ref, prev_ref, s0_ref, mu_ref, w0_ref, wdup_ref, a0_ref, aup_ref, gup_ref,
                kk_ref, ka_ref, rk_ref, gng_ref, gnb_ref, eseg_ref,
                u_ref, hist_ref, cw_ref, cb_ref, cg_ref, cbeta_ref,
                out_ref, sfin_ref, cv_ref,
                s_ref, last_ref, rt_ref, at_ref, bt_ref, kt_ref, v_ref, pc_ref, y_ref,
                r_ref, km_ref, g_ref, ext_ref, sh_ref, *, n_seq, tt_seq, chunk, t_valid, n_t):
    t = pl.program_id(1)
    tt = n_seq * tt_seq
    n_chunk = tt_seq // chunk
    n_fused = chunk.bit_length() - 2
    nd = RW_HEAD_DIM
    c1, c2, c3 = RW_WIDTH, 2 * RW_WIDTH, 3 * RW_WIDTH
    c4 = c3 + DECAY_LORA
    c5 = c4 + ICLR_LORA

    @pl.when(t == 0)
    def _():
        for s in range(n_seq):
            for h in range(RW_HEADS):
                g, hg = divmod(h, GROUP)
                s_ref[s, g, :, hg * nd:(hg + 1) * nd] = s0_ref[0, s, h]
        if n_seq == 1:
            last_ref[0:1, :] = prev_ref[0]

    _conv_module(t, u_ref, hist_ref, cw_ref, cb_ref, cg_ref, cbeta_ref, cv_ref, ext_ref, sh_ref,
                 n_seq=n_seq, tt_seq=tt_seq, n_t=n_t)

    p = p_ref[...]
    row = lax.broadcasted_iota(I32, (tt, 1), 0)
    seq_row = row & (tt_seq - 1)
    rolled = pltpu.roll(p, shift=1, axis=0)
    if n_seq == 1:
        ps = jnp.where(seq_row == 0, last_ref[0:1, :], rolled)
        last_ref[0:1, :] = p[tt - 1:tt, :]
    else:
        prev_rows = jnp.concatenate(
            [jnp.broadcast_to(prev_ref[q], (tt_seq, RW_PROJ)) for q in range(n_seq)], axis=0)
        ps = jnp.where(seq_row == 0, prev_rows, rolled)
    z = p + mu_ref[...] * (ps - p)
    r, k, v = z[:, :c1], z[:, c1:c2], z[:, c2:c3]
    wd, ad, gd = z[:, c3:c4], z[:, c4:c5], z[:, c5:]
    wpre = w0_ref[...] + _dot(jnp.tanh(wd), wdup_ref[...])
    neg = -wpre
    w = -(jnp.maximum(neg, 0.0) + jnp.log(1.0 + jnp.exp(-jnp.abs(neg)))) - 0.5
    a = _sigmoid(a0_ref[...] + _dot(ad, aup_ref[...]))
    g_ref[...] = _dot(_sigmoid(gd), gup_ref[...])
    eseg = eseg_ref[...]
    kk = k * kk_ref[...]
    kk = kk / jnp.maximum(jnp.sqrt(_segsum(kk * kk, eseg)), 1e-12)
    kmod = k * (1.0 + (a - 1.0) * ka_ref[...])
    logd = -jnp.exp(w)
    if t_valid < tt_seq * n_t:
        valid = (t * tt_seq + seq_row) < t_valid
        logd = jnp.where(valid, logd, 0.0)
        kk = jnp.where(valid, kk, 0.0)
        kmod = jnp.where(valid, kmod, 0.0)
        v = jnp.where(valid, v, 0.0)
    ri = lax.broadcasted_iota(I32, (tt, tt), 0)
    ci = lax.broadcasted_iota(I32, (tt, tt), 1)
    shift = chunk.bit_length() - 1
    ltri = jnp.where(ci <= ri, jnp.where((ri >> shift) == (ci >> shift), 1.0, 0.0), 0.0).astype(BF16)
    hi, mid, lo = _split3(logd)
    logp = (jnp.dot(ltri, hi, preferred_element_type=F32)
            + jnp.dot(ltri, mid, preferred_element_type=F32)
            + jnp.dot(ltri, lo, preferred_element_type=F32))
    pcum = jnp.exp(logp)
    pinv = jnp.exp(-logp)
    r_ref[...] = r
    km_ref[...] = kmod
    v_ref[...] = v
    pc_ref[...] = pcum
    rt_ref[...] = r * pcum
    at_ref[...] = -kk * jnp.exp(logp - logd)
    bt_ref[...] = kk * a * pinv
    kt_ref[...] = kmod * pinv

    cw = GROUP * chunk
    ti = lax.broadcasted_iota(I32, (chunk, cw), 0)
    si = lax.broadcasted_iota(I32, (chunk, cw), 1) & (chunk - 1)
    strict = si < ti
    incl = si <= ti
    eye_c = jnp.where(si == ti, 1.0, 0.0)
    ji = lax.broadcasted_iota(I32, (nd, GW), 0)
    jj = lax.broadcasted_iota(I32, (nd, GW), 1) & (nd - 1)
    eye_n = jnp.where(ji == jj, 1.0, 0.0)

    all_chains = [(s, c, g) for s in range(n_seq) for c in range(n_chunk) for g in range(N_GROUPS)]
    masks = {}

    def blk(ref, s, c, g):
        r0 = s * tt_seq + c * chunk
        return ref[r0:r0 + chunk, g * GW:(g + 1) * GW]

    rp, yv, mm, gm = {}, {}, {}, {}

    def chunk_operators(chains):
        am = {q: blk(at_ref, *q) for q in chains}
        rm = {q: blk(rt_ref, *q) for q in chains}
        bm = {q: blk(bt_ref, *q) for q in chains}
        km = {q: blk(kt_ref, *q) for q in chains}
        vm = {q: blk(v_ref, *q) for q in chains}
        a_ab, a_ak, a_rb, a_rk = {}, {}, {}, {}
        for q in chains:
            ar = jnp.concatenate([am[q], rm[q]], axis=0)
            s1 = _dot_nt(ar, _bd(bm[q], nd, masks))
            s2 = _dot_nt(ar, _bd(km[q], nd, masks))
            a_ab[q] = jnp.where(strict, s1[:chunk], 0.0)
            a_rb[q] = jnp.where(incl, s1[chunk:], 0.0)
            a_ak[q] = jnp.where(strict, s2[:chunk], 0.0)
            a_rk[q] = jnp.where(incl, s2[chunk:], 0.0)
        tinv = {q: eye_c + a_ab[q] for q in chains}
        pw = {q: _dot(a_ab[q], _bd(a_ab[q], chunk, masks)) for q in chains}
        for lvl in range(n_fused):
            last = lvl == n_fused - 1
            for q in chains:
                lhs = tinv[q] if last else jnp.concatenate([tinv[q], pw[q]], axis=0)
                zz = _dot(lhs, _bd(pw[q], chunk, masks))
                tinv[q] = tinv[q] + zz[:chunk]
                if not last:
                    pw[q] = zz[chunk:]
        akv, rkv = {}, {}
        for q in chains:
            zv = _dot(jnp.concatenate([a_ak[q], a_rk[q]], axis=0), _bd(vm[q], nd, masks))
            akv[q], rkv[q] = zv[:chunk], zv[chunk:]
        ap = {q: _dot(tinv[q], _bd(am[q], nd, masks)) for q in chains}
        uv = {q: _dot(tinv[q], _bd(akv[q], nd, masks)) for q in chains}
        for q in chains:
            rp[q] = rm[q] + _dot(a_rb[q], _bd(ap[q], nd, masks))
            yv[q] = _dot(a_rb[q], _bd(uv[q], nd, masks)) + rkv[q]
        for q in chains:
            s, c, g = q
            r_end = s * tt_seq + (c + 1) * chunk
            pc = pc_ref[r_end - 8:r_end, g * GW:(g + 1) * GW][7:8, :]
            cross = _dot_tn(jnp.concatenate([ap[q], uv[q]], axis=1), bm[q])
            vk = _dot_tn(vm[q], km[q])
            mm[q] = (eye_n + _collapse(cross[:GW])) * pc
            gm[q] = (_collapse(cross[GW:]) + _collapse(vk)) * pc

    for i0 in range(0, len(all_chains), CHAIN_BATCH):
        chunk_operators(all_chains[i0:i0 + CHAIN_BATCH])

    for s in range(n_seq):
        for g in range(N_GROUPS):
            st = s_ref[s, g]
            for c in range(n_chunk):
                q = (s, c, g)
                r0 = s * tt_seq + c * chunk
                y_ref[r0:r0 + chunk, g * GW:(g + 1) * GW] = _dot_nt(rp[q], _bd(st, nd, masks)) + yv[q]
                st = _dot(st, _bd(mm[q], nd, masks)) + gm[q]
            s_ref[s, g] = st

    y = y_ref[...]
    inv_n = 1.0 / RW_HEAD_DIM
    mean = _segsum(y, eseg) * inv_n
    yc = y - mean
    var = _segsum(yc * yc, eseg) * inv_n
    yn = yc * lax.rsqrt(var + GN_EPS) * gng_ref[...] + gnb_ref[...]
    bonus = _segsum(r_ref[...] * km_ref[...] * rk_ref[...], eseg) * v_ref[...]
    out_ref[...] = (yn + bonus) * g_ref[...]

    @pl.when(t == n_t - 1)
    def _():
        for s in range(n_seq):
            for h in range(RW_HEADS):
                g, hg = divmod(h, GROUP)
                sfin_ref[s, h] = s_ref[s, g, :, hg * nd:(hg + 1) * nd]


def _mixer(p_rows, u_rows, row_block0, prev, s0, hist, wts, conv_w, *, batch, n_seq, tt_seq, chunk, t_valid,
           n_t, out_rows):
    (mu, w0, wdup, a0, aup, gup, k_k, k_a, r_k, gn_g, gn_b, eseg) = wts
    dw_w, dw_b, ln_g, ln_b = conv_w
    tt = n_seq * tt_seq
    kern = functools.partial(_wkv_kernel, n_seq=n_seq, tt_seq=tt_seq, chunk=chunk, t_valid=t_valid, n_t=n_t)
    const = lambda shape: pl.BlockSpec(shape, lambda b, t: tuple(0 for _ in shape))
    row_out = lambda w: pl.BlockSpec((tt, w), lambda b, t: (b * n_t + t, 0))
    blk = pltpu.VMEM((tt, RW_WIDTH), F32)
    st_spec = pl.BlockSpec((n_seq, RW_HEADS, RW_HEAD_DIM, RW_HEAD_DIM), lambda b, t: (b, 0, 0, 0))
    return pl.pallas_call(
        kern,
        grid=(batch // n_seq, n_t),
        in_specs=[pl.BlockSpec((tt, RW_PROJ), lambda b, t: (row_block0 + b * n_t + t, 0)),
                  pl.BlockSpec((n_seq, 1, RW_PROJ), lambda b, t: (b, 0, 0)),
                  pl.BlockSpec((1, n_seq, RW_HEADS, RW_HEAD_DIM, RW_HEAD_DIM), lambda b, t: (0, b, 0, 0, 0)),
                  const((1, RW_PROJ)), const((1, RW_WIDTH)), const((DECAY_LORA, RW_WIDTH)),
                  const((1, RW_WIDTH)), const((ICLR_LORA, RW_WIDTH)), const((GATE_LORA, RW_WIDTH)),
                  const((1, RW_WIDTH)), const((1, RW_WIDTH)), const((1, RW_WIDTH)),
                  const((1, RW_WIDTH)), const((1, RW_WIDTH)), const((RW_WIDTH, RW_WIDTH)),
                  pl.BlockSpec((tt, CONV_WIDTH), lambda b, t: (row_block0 + b * n_t + t, 0)),
                  pl.BlockSpec((1, n_seq, CONV_K - 1, CONV_WIDTH), lambda b, t: (0, b, 0, 0)),
                  const((CONV_K, CONV_WIDTH)), const((1, CONV_WIDTH)),
                  const((1, CONV_WIDTH)), const((1, CONV_WIDTH))],
        out_specs=[row_out(RW_WIDTH), st_spec, row_out(CONV_WIDTH)],
        out_shape=[jax.ShapeDtypeStruct((out_rows, RW_WIDTH), F32),
                   jax.ShapeDtypeStruct((batch, RW_HEADS, RW_HEAD_DIM, RW_HEAD_DIM), F32),
                   jax.ShapeDtypeStruct((out_rows, CONV_WIDTH), F32)],
        scratch_shapes=[pltpu.VMEM((n_seq, N_GROUPS, RW_HEAD_DIM, GW), F32),
                        pltpu.VMEM((8, RW_PROJ), F32),
                        blk, blk, blk, blk, blk, blk, blk, blk, blk, blk,
                        pltpu.VMEM((n_seq, CONV_PAD + tt_seq, CONV_WIDTH), F32),
                        pltpu.VMEM((8, CONV_PAD + tt_seq, CONV_WIDTH), F32)],
        compiler_params=_cparams(("parallel", "arbitrary")),
        name="mixer",
    )(p_rows, prev, s0, mu, w0, wdup, a0, aup, gup, k_k, k_a, r_k, gn_g, gn_b, eseg,
      u_rows, hist, dw_w, dw_b, ln_g, ln_b)


def _mid1_kernel(xp_ref, xs_ref, rwp_ref, rws_ref, cvp_ref, cvs_ref, wo_ref, g_ref, wq_ref, x1_ref, q_ref,
                 *, n_pt):
    first = pl.program_id(0) < n_pt
    rw = _pick(first, rwp_ref, rws_ref)
    cv = _pick(first, cvp_ref, cvs_ref)
    mix = _dot(rw, wo_ref[:RW_WIDTH, :]) + _dot(cv, wo_ref[RW_WIDTH:, :])
    x1 = _pick(first, xp_ref, xs_ref) + mix
    x1_ref[...] = x1
    q = _dot(_rms(x1, g_ref[...]), wq_ref[...])
    q_ref[...] = (q * ATTN_SCALE).astype(BF16)


def _mid1(x_p, x_s, rw_p, rw_s, cv_p, cv_s, w_out_bf, g_x, w_q_bf):
    n = x_p.shape[0] + x_s.shape[0]
    n_pt = rw_p.shape[0] // TOK_TILE
    row = lambda w: pl.BlockSpec((TOK_TILE, w), lambda i: (i, 0))
    full = lambda a, b: pl.BlockSpec((a, b), lambda i: (0, 0))
    return pl.pallas_call(
        functools.partial(_mid1_kernel, n_pt=n_pt),
        grid=(n // TOK_TILE,),
        in_specs=[*_two_source_specs(D_MODEL, n_pt), *_two_source_specs(RW_WIDTH, n_pt),
                  *_two_source_specs(CONV_WIDTH, n_pt),
                  full(D_MODEL, D_MODEL), full(1, D_MODEL), full(D_MODEL, D_MODEL)],
        out_specs=[row(D_MODEL), row(D_MODEL)],
        out_shape=[jax.ShapeDtypeStruct((n, D_MODEL), F32), jax.ShapeDtypeStruct((n, D_MODEL), BF16)],
        compiler_params=_cparams(("parallel",)),
        name="mid1",
    )(x_p, x_s, rw_p, rw_s, cv_p, cv_s, w_out_bf, g_x, w_q_bf)


def _memkv_kernel(m_ref, g_ref, wk_ref, wv_ref, k_ref, v_ref):
    m = _rms(m_ref[...], g_ref[...])
    k_ref[...] = _dot(m, wk_ref[...])
    v_ref[...] = _dot(m, wv_ref[...])


def _memkv(mem, g_mem, w_mk_bf, w_mv_bf):
    n = mem.shape[0]
    tile = N_MEM
    row = pl.BlockSpec((tile, D_MODEL), lambda i: (i, 0))
    full = lambda a, b: pl.BlockSpec((a, b), lambda i: (0, 0))
    return pl.pallas_call(
        _memkv_kernel,
        grid=(n // tile,),
        in_specs=[row, full(1, D_MODEL), full(D_MODEL, D_MODEL), full(D_MODEL, D_MODEL)],
        out_specs=[row, row],
        out_shape=[jax.ShapeDtypeStruct((n, D_MODEL), F32), jax.ShapeDtypeStruct((n, D_MODEL), F32)],
        compiler_params=_cparams(("parallel",)),
        name="memkv",
    )(mem, g_mem, w_mk_bf, w_mv_bf)


def _attn_head(q_h, k_h, v_h):
    s = _dot_nt(q_h, k_h)
    e = jnp.exp(s - jnp.max(s, axis=-1, keepdims=True))
    pr = e / jnp.sum(e, axis=-1, keepdims=True)
    return _dot(pr, v_h).astype(BF16)


def _attn_rows_kernel(q_ref, k_ref, v_ref, o_ref):
    q = q_ref[...]
    for h in range(X_HEADS):
        hs = slice(h * X_HEAD_DIM, (h + 1) * X_HEAD_DIM)
        o_ref[:, hs] = _attn_head(q[:, hs], k_ref[0, :, hs], v_ref[0, :, hs])


def _attn_seq_kernel(q_ref, k_hbm, v_hbm, o_ref, kbuf, vbuf, sem, *, n_steps):
    i = pl.program_id(0)
    slot = lax.rem(i, 2)

    def copies(step, s):
        out = []
        for g in range(ATTN_SEQS):
            seq = step * ATTN_SEQS + g
            for h in range(X_HEADS):
                lanes = pl.ds(h * X_HEAD_DIM, X_HEAD_DIM)
                out.append(pltpu.make_async_copy(k_hbm.at[0, seq, :, h, :], kbuf.at[s, g, :, lanes], sem.at[s]))
                out.append(pltpu.make_async_copy(v_hbm.at[0, seq, :, h, :], vbuf.at[s, g, :, lanes], sem.at[s]))
        return out

    @pl.when(i == 0)
    def _():
        for cp in copies(0, 0):
            cp.start()

    @pl.when(i + 1 < n_steps)
    def _():
        for cp in copies(i + 1, 1 - slot):
            cp.start()

    for cp in copies(i, slot):
        cp.wait()
    head_of_lane = lax.broadcasted_iota(I32, (SAMPLE_PAD * X_HEADS, D_MODEL), 1) >> (X_HEAD_DIM.bit_length() - 1)
    head_of_row = lax.broadcasted_iota(I32, (SAMPLE_PAD * X_HEADS, D_MODEL), 0) >> (SAMPLE_PAD.bit_length() - 1)
    own = head_of_lane == head_of_row
    for g in range(ATTN_SEQS):
        q8 = q_ref[g].astype(F32)
        qbd = jnp.where(own, jnp.concatenate([q8] * X_HEADS, axis=0), 0.0)
        s = _dot_nt(qbd, kbuf[slot, g])
        e = jnp.exp(s - jnp.max(s, axis=-1, keepdims=True))
        pr = e / jnp.sum(e, axis=-1, keepdims=True)
        o_all = jnp.where(own, _dot(pr, vbuf[slot, g]), 0.0)
        o8 = o_all[0:SAMPLE_PAD]
        for h in range(1, X_HEADS):
            o8 = o8 + o_all[SAMPLE_PAD * h:SAMPLE_PAD * (h + 1)]
        o_ref[g] = o8.astype(BF16)


def _attn_rows(q, mk, mv, n_rows, tiles_per_seq):
    qspec = pl.BlockSpec((TOK_TILE, D_MODEL), lambda i, j: (i * tiles_per_seq + j, 0))
    kvspec = pl.BlockSpec((1, N_MEM, D_MODEL), lambda i, j: (i, 0, 0))
    return pl.pallas_call(
        _attn_rows_kernel,
        grid=(mk.shape[0], tiles_per_seq),
        in_specs=[qspec, kvspec, kvspec],
        out_specs=qspec,
        out_shape=jax.ShapeDtypeStruct((n_rows, D_MODEL), BF16),
        compiler_params=_cparams(("parallel", "parallel")),
        name="attn",
    )(q, mk, mv)


def _attn_seq(q3, mk, mv):
    b, t, _ = q3.shape
    qspec = pl.BlockSpec((ATTN_SEQS, t, D_MODEL), lambda i: (i, 0, 0))
    anyspec = pl.BlockSpec(memory_space=pl.ANY)
    head_buf = pltpu.VMEM((2, ATTN_SEQS, N_MEM, D_MODEL), F32)
    return pl.pallas_call(
        functools.partial(_attn_seq_kernel, n_steps=b // ATTN_SEQS),
        grid=(b // ATTN_SEQS,),
        in_specs=[qspec, anyspec, anyspec],
        out_specs=qspec,
        out_shape=jax.ShapeDtypeStruct((b, t, D_MODEL), BF16),
        scratch_shapes=[head_buf, head_buf, pltpu.SemaphoreType.DMA((2,))],
        compiler_params=_cparams(("arbitrary",)),
        name="attn",
    )(q3, mk, mv)


def _mid2_kernel(x1_ref, op_ref, os_ref, wo_ref, g_ref, wr_ref, br_ref, tri_ref,
                 x2_ref, h3_ref, ids_ref, prob_ref, lrank_ref, cnt_ref, base_ref, tot_ref, carry_ref, *, n_pt):
    i = pl.program_id(0)

    @pl.when(i == 0)
    def _():
        carry_ref[...] = jnp.zeros_like(carry_ref)

    x2 = x1_ref[...] + _dot(_pick(i < n_pt, op_ref, os_ref), wo_ref[...])
    x2_ref[...] = x2
    h3 = _rms(x2, g_ref[...])
    h3_ref[...] = h3.astype(BF16)
    h3_hi, h3_lo = _split2(h3)
    wr_hi, wr_lo = _split2(wr_ref[...])
    logits = (jnp.dot(h3_hi, wr_hi, preferred_element_type=F32)
              + jnp.dot(h3_hi, wr_lo, preferred_element_type=F32)
              + jnp.dot(h3_lo, wr_hi, preferred_element_type=F32)) + br_ref[...]
    n = logits.shape[0]
    lane = lax.broadcasted_iota(I32, (n, N_EXPERTS), 1)
    work = logits
    vals, ids = [], []
    for _ in range(TOP_K):
        m = jnp.max(work, axis=-1, keepdims=True)
        idx = jnp.min(jnp.where(work == m, lane, N_EXPERTS), axis=-1, keepdims=True)
        vals.append(m)
        ids.append(idx)
        work = jnp.where(lane == idx, -jnp.inf, work)
    exps = [jnp.exp(vk - vals[0]) for vk in vals]
    den = exps[0] + exps[1] + exps[2] + exps[3]
    mask = jnp.zeros((n, N_EXPERTS), F32)
    for idx in ids:
        mask = mask + jnp.where(lane == idx, 1.0, 0.0)
    lrank = jnp.dot(tri_ref[...], mask.astype(BF16), preferred_element_type=F32)
    cnt = jnp.sum(mask, axis=0, keepdims=True)
    cnt_ref[0] = cnt
    base_ref[0] = carry_ref[0:1, :]
    carry_ref[0:1, :] = carry_ref[0:1, :] + jnp.floor((cnt + (STRIP - 1)) * (1.0 / STRIP)) * STRIP
    tot_ref[...] = carry_ref[0:1, :]
    for kq in range(TOP_K):
        ids_ref[:, kq:kq + 1] = ids[kq]
        prob_ref[:, kq:kq + 1] = exps[kq] / den
        lrank_ref[:, kq:kq + 1] = jnp.sum(jnp.where(lane == ids[kq], lrank, 0.0), axis=-1, keepdims=True)


def _mid2(x1, o_p, o_s, w_o_bf, g_moe, w_router, b_router, tri):
    n = x1.shape[0]
    n_t = n // TOK_TILE
    n_pt = o_p.shape[0] // TOK_TILE
    row = lambda w: pl.BlockSpec((TOK_TILE, w), lambda i: (i, 0))
    full = lambda a, b: pl.BlockSpec((a, b), lambda i: (0, 0))
    per_tile = pl.BlockSpec((1, 1, N_EXPERTS), lambda i: (i, 0, 0))
    return pl.pallas_call(
        functools.partial(_mid2_kernel, n_pt=n_pt),
        grid=(n_t,),
        in_specs=[row(D_MODEL), *_two_source_specs(D_MODEL, n_pt), full(D_MODEL, D_MODEL), full(1, D_MODEL),
                  full(D_MODEL, N_EXPERTS), full(1, N_EXPERTS), full(TOK_TILE, TOK_TILE)],
        out_specs=[row(D_MODEL), row(D_MODEL), row(TOP_K), row(TOP_K), row(TOP_K), per_tile, per_tile,
                   full(1, N_EXPERTS)],
        out_shape=[jax.ShapeDtypeStruct((n, D_MODEL), F32), jax.ShapeDtypeStruct((n, D_MODEL), BF16),
                   jax.ShapeDtypeStruct((n, TOP_K), I32), jax.ShapeDtypeStruct((n, TOP_K), F32),
                   jax.ShapeDtypeStruct((n, TOP_K), F32),
                   jax.ShapeDtypeStruct((n_t, 1, N_EXPERTS), F32), jax.ShapeDtypeStruct((n_t, 1, N_EXPERTS), F32),
                   jax.ShapeDtypeStruct((1, N_EXPERTS), F32)],
        scratch_shapes=[pltpu.VMEM((8, N_EXPERTS), F32)],
        compiler_params=_cparams(("arbitrary",)),
        name="mid2",
    )(x1, o_p, o_s, w_o_bf, g_moe, w_router, b_router, tri)


def _slot_select(ids_ref, lrank_ref, lbase_v, weights):
    n = ids_ref.shape[0]
    lane = lax.broadcasted_iota(I32, (n, N_EXPERTS), 1)
    col = lax.broadcasted_iota(I32, (n, SLOT_ROWS), 1)
    sel = jnp.zeros((n, SLOT_ROWS), F32)
    for kq in range(TOP_K):
        base = jnp.sum(jnp.where(lane == ids_ref[:, kq:kq + 1], lbase_v, 0.0), axis=-1, keepdims=True)
        dest = (base + lrank_ref[:, kq:kq + 1]).astype(I32)
        sel = jnp.where(col == dest, weights[kq], sel)
    return sel.astype(BF16)


def _strip_copies(c8_ref, lb_ref, gb_ref, make):
    for e in range(N_EXPERTS):
        rows = c8_ref[0, 0, e]

        @pl.when(rows > 0)
        def _():
            make(pl.multiple_of(lb_ref[0, 0, e], STRIP), pl.multiple_of(gb_ref[0, 0, e], STRIP),
                 pl.multiple_of(rows, STRIP)).start()


def _dispatch_kernel(tot_ref, tailn_ref, tailo_ref, nt_ref,
                     c8_ref, lb_ref, gb_ref, h3_ref, ids_ref, lrank_ref, lbv_ref,
                     xs_hbm, cbuf, zbuf, sem, *, n_t, n_tiles):
    i = pl.program_id(0)
    slot = lax.rem(i, 2)
    sel = _slot_select(ids_ref, lrank_ref, lbv_ref[0], [1.0] * TOP_K)
    cbuf[slot] = _dot_tn(sel, h3_ref[...])
    _strip_copies(c8_ref, lb_ref, gb_ref,
                  lambda lr, gr, rows: pltpu.make_async_copy(cbuf.at[slot, pl.ds(lr, rows), :],
                                                             xs_hbm.at[pl.ds(gr, rows), :], sem.at[slot]))

    def wait_rows(s, rows):
        rows = pl.multiple_of(rows, STRIP)
        pltpu.make_async_copy(cbuf.at[s, pl.ds(0, rows), :], xs_hbm.at[pl.ds(0, rows), :], sem.at[s]).wait()

    @pl.when(i > 0)
    def _():
        wait_rows(1 - slot, tot_ref[jnp.maximum(i - 1, 0)])

    @pl.when(i == n_t - 1)
    def _():
        wait_rows(slot, tot_ref[i])
        zbuf[...] = jnp.zeros_like(zbuf)
        for e in range(N_EXPERTS):
            rows = pl.multiple_of(tailn_ref[e], STRIP)

            @pl.when(rows > 0)
            def _():
                pltpu.make_async_copy(zbuf.at[pl.ds(0, rows), :],
                                      xs_hbm.at[pl.ds(pl.multiple_of(tailo_ref[e], STRIP), rows), :],
                                      sem.at[2]).start()

        for e in range(N_EXPERTS):
            rows = pl.multiple_of(tailn_ref[e], STRIP)

            @pl.when(rows > 0)
            def _():
                pltpu.make_async_copy(zbuf.at[pl.ds(0, rows), :], xs_hbm.at[pl.ds(0, rows), :], sem.at[2]).wait()

        def tile_copy(j):
            return pltpu.make_async_copy(
                zbuf, xs_hbm.at[pl.ds(pl.multiple_of(j * FFN_TILE, FFN_TILE), FFN_TILE), :], sem.at[2])

        def start_tile(j, c):
            tile_copy(j).start()
            return c

        def wait_tile(j, c):
            tile_copy(j).wait()
            return c

        lax.fori_loop(nt_ref[0], n_tiles, start_tile, 0)
        lax.fori_loop(nt_ref[0], n_tiles, wait_tile, 0)


def _dispatch(tile_tot, tail_n, tail_off, n_used, c8, lbase, gbase, h3, ids, lrank, n_tiles):
    n_t = c8.shape[0]
    smem = pl.BlockSpec((1, 1, N_EXPERTS), lambda i, *_: (i, 0, 0), memory_space=pltpu.SMEM)
    row = lambda w: pl.BlockSpec((TOK_TILE, w), lambda i, *_: (i, 0))
    grid_spec = pltpu.PrefetchScalarGridSpec(
        num_scalar_prefetch=4,
        grid=(n_t,),
        in_specs=[smem, smem, smem, row(D_MODEL), row(TOP_K), row(TOP_K),
                  pl.BlockSpec((1, 1, N_EXPERTS), lambda i, *_: (i, 0, 0))],
        out_specs=pl.BlockSpec(memory_space=pl.ANY),
        scratch_shapes=[pltpu.VMEM((2, SLOT_ROWS, D_MODEL), F32), pltpu.VMEM((FFN_TILE, D_MODEL), F32),
                        pltpu.SemaphoreType.DMA((3,))],
    )
    return pl.pallas_call(
        functools.partial(_dispatch_kernel, n_t=n_t, n_tiles=n_tiles),
        grid_spec=grid_spec,
        out_shape=jax.ShapeDtypeStruct((n_tiles * FFN_TILE, D_MODEL), F32),
        compiler_params=_cparams(("arbitrary",)),
        name="dispatch",
    )(tile_tot, tail_n, tail_off, n_used, c8, lbase, gbase, h3, ids, lrank, lbase.astype(F32))


def _ffn_kernel(te_ref, ne_ref, lim_ref, nt_ref, xs_ref, wg_hbm, bg_ref, wu_hbm, bu_ref, wd_hbm, bd_ref, y_ref,
                wf32, wbf, turn, sem):
    i = pl.program_id(0)
    n_used = nt_ref[0]
    e = te_ref[i]

    def fetch(expert, s):
        return [pltpu.make_async_copy(w_hbm.at[0, expert], wf32.at[s, j], sem.at[s])
                for j, w_hbm in enumerate((wg_hbm, wu_hbm, wd_hbm))]

    @pl.when(i == 0)
    def _():
        turn[0] = 0
        for cp in fetch(e, 0):
            cp.start()

    @pl.when(jnp.logical_and(i < n_used, jnp.logical_or(i == 0, e != te_ref[jnp.maximum(i - 1, 0)])))
    def _():
        s = turn[0]
        for cp in fetch(e, s):
            cp.wait()
        nxt = i + ne_ref[e]

        @pl.when(nxt < n_used)
        def _():
            for cp in fetch(te_ref[jnp.minimum(nxt, te_ref.shape[0] - 1)], 1 - s):
                cp.start()

        for j in range(3):
            wbf[j] = wf32[s, j].astype(BF16)
        turn[0] = 1 - s

    def expert(rows):
        x = xs_ref[0:rows, :].astype(BF16)
        gate = jnp.minimum(jnp.dot(x, wbf[0], preferred_element_type=F32) + bg_ref[0, 0], SWIGLU_LIMIT)
        up = jnp.clip(jnp.dot(x, wbf[1], preferred_element_type=F32) + bu_ref[0, 0],
                      -SWIGLU_LIMIT, SWIGLU_LIMIT)
        hid = (up + 1.0) * gate * _sigmoid(gate * SWIGLU_ALPHA)
        y_ref[0:rows, :] = jnp.dot(hid.astype(BF16), wbf[2], preferred_element_type=F32) + bd_ref[0, 0]

    half = FFN_TILE // 2
    valid = lim_ref[e] - i * FFN_TILE

    @pl.when(jnp.logical_and(i < n_used, valid > half))
    def _():
        expert(FFN_TILE)

    @pl.when(jnp.logical_and(i < n_used, valid <= half))
    def _():
        expert(half)
        y_ref[half:, :] = jnp.zeros((FFN_TILE - half, D_MODEL), F32)

    @pl.when(i >= n_used)
    def _():
        y_ref[...] = jnp.zeros_like(y_ref)


def _ffn(tile_expert, expert_tiles, expert_end, n_used, xs, wg, bg, wu, bu, wd, bd):
    n_tiles = xs.shape[0] // FFN_TILE
    anyspec = pl.BlockSpec(memory_space=pl.ANY)
    bspec = pl.BlockSpec((1, 1, 1, D_MODEL), lambda i, te, ne, lim, nt: (0, te[i], 0, 0))
    grid_spec = pltpu.PrefetchScalarGridSpec(
        num_scalar_prefetch=4,
        grid=(n_tiles,),
        in_specs=[pl.BlockSpec((FFN_TILE, D_MODEL), lambda i, te, ne, lim, nt: (jnp.minimum(i, nt[0] - 1), 0)),
                  anyspec, bspec, anyspec, bspec, anyspec, bspec],
        out_specs=pl.BlockSpec((FFN_TILE, D_MODEL), lambda i, te, ne, lim, nt: (i, 0)),
        scratch_shapes=[pltpu.VMEM((2, 3, D_MODEL, D_MODEL), F32), pltpu.VMEM((3, D_MODEL, D_MODEL), BF16),
                        pltpu.SMEM((1,), I32), pltpu.SemaphoreType.DMA((2,))],
    )
    return pl.pallas_call(
        _ffn_kernel,
        grid_spec=grid_spec,
        out_shape=jax.ShapeDtypeStruct((n_tiles * FFN_TILE, D_MODEL), F32),
        compiler_params=_cparams(("arbitrary",)),
        name="ffn",
    )(tile_expert, expert_tiles, expert_end, n_used, xs, wg, bg, wu, bu, wd, bd)


def _combine_kernel(tot_ref, c8_ref, lb_ref, gb_ref, c8n_ref, lbn_ref, gbn_ref,
                    ys_hbm, x2_ref, ids_ref, lrank_ref, prob_ref, lbv_ref, g_ref,
                    op_ref, os_ref, sbuf, sem, *, n_pt, n_t):
    i = pl.program_id(0)
    slot = lax.rem(i, 2)

    def fetch(c8, lb, gb, dst_slot):
        _strip_copies(c8, lb, gb,
                      lambda lr, gr, rows: pltpu.make_async_copy(ys_hbm.at[pl.ds(gr, rows), :],
                                                                 sbuf.at[dst_slot, pl.ds(lr, rows), :],
                                                                 sem.at[dst_slot]))

    @pl.when(i == 0)
    def _():
        sbuf[...] = jnp.zeros_like(sbuf)
        fetch(c8_ref, lb_ref, gb_ref, 0)

    @pl.when(i + 1 < n_t)
    def _():
        fetch(c8n_ref, lbn_ref, gbn_ref, 1 - slot)

    rows = pl.multiple_of(tot_ref[i], STRIP)
    pltpu.make_async_copy(ys_hbm.at[pl.ds(0, rows), :], sbuf.at[slot, pl.ds(0, rows), :], sem.at[slot]).wait()
    selw = _slot_select(ids_ref, lrank_ref, lbv_ref[0], [prob_ref[:, kq:kq + 1] for kq in range(TOP_K)])
    y = jnp.dot(selw, sbuf[slot].astype(BF16), preferred_element_type=F32)
    out = _rms(x2_ref[...] + y, g_ref[...])

    @pl.when(i < n_pt)
    def _():
        op_ref[...] = out

    @pl.when(i >= n_pt)
    def _():
        os_ref[...] = out


def _combine(tile_tot, c8, lbase, gbase, ys, x2, ids, lrank, prob, g_final, n_pt):
    n_t = c8.shape[0]
    last = n_t - 1
    cur = pl.BlockSpec((1, 1, N_EXPERTS), lambda i, *_: (i, 0, 0), memory_space=pltpu.SMEM)
    nxt = pl.BlockSpec((1, 1, N_EXPERTS), lambda i, *_: (jnp.minimum(i + 1, last), 0, 0), memory_space=pltpu.SMEM)
    row = lambda w: pl.BlockSpec((TOK_TILE, w), lambda i, *_: (i, 0))
    grid_spec = pltpu.PrefetchScalarGridSpec(
        num_scalar_prefetch=1,
        grid=(n_t,),
        in_specs=[cur, cur, cur, nxt, nxt, nxt,
                  pl.BlockSpec(memory_space=pl.ANY),
                  row(D_MODEL), row(TOP_K), row(TOP_K), row(TOP_K),
                  pl.BlockSpec((1, 1, N_EXPERTS), lambda i, *_: (i, 0, 0)),
                  pl.BlockSpec((1, D_MODEL), lambda i, *_: (0, 0))],
        out_specs=[pl.BlockSpec((TOK_TILE, D_MODEL), lambda i, *_: (jnp.minimum(i, n_pt - 1), 0)),
                   pl.BlockSpec((TOK_TILE, D_MODEL), lambda i, *_: (jnp.maximum(i - n_pt, 0), 0))],
        scratch_shapes=[pltpu.VMEM((2, SLOT_ROWS, D_MODEL), F32), pltpu.SemaphoreType.DMA((2,))],
    )
    return pl.pallas_call(
        functools.partial(_combine_kernel, n_pt=n_pt, n_t=n_t),
        grid_spec=grid_spec,
        out_shape=[jax.ShapeDtypeStruct((n_pt * TOK_TILE, D_MODEL), F32),
                   jax.ShapeDtypeStruct(((n_t - n_pt) * TOK_TILE, D_MODEL), F32)],
        compiler_params=_cparams(("arbitrary",)),
        name="combine",
    )(tile_tot, c8, lbase, gbase, c8, lbase, gbase, ys, x2, ids, lrank, prob, lbase.astype(F32), g_final)


def kernel(x_prompt, x_sample, mem_prompt, state_shift, state_wkv, state_conv, cache_mem_k, cache_mem_v,
           g_mix, w_in, mu_shift, w0, w_decay_up, a0, w_iclr_up, w_glora_up, k_k, k_a, r_k, gn_g, gn_b,
           dw_w, dw_b, cln_g, cln_b, w_out, g_xattn, g_mem, w_q, w_mk, w_mv, w_o,
           g_moe, w_router, b_router, w_moe_gate, b_moe_gate, w_moe_up, b_moe_up, w_moe_down, b_moe_down,
           g_final):
    bp, tp, _ = x_prompt.shape
    bs, ts, _ = x_sample.shape
    n_p, n_s = bp * tp, bs * ts
    n_all = n_p + n_s
    assert tp % WKV_BLOCK == 0 and tp % TOK_TILE == 0 and n_s == TOK_TILE
    assert ts <= SAMPLE_PAD and bs % SAMPLE_SEQS == 0 and bs % ATTN_SEQS == 0
    row2 = lambda a: a.reshape(1, -1)

    x_p = x_prompt.reshape(n_p, D_MODEL)
    x_s = x_sample.reshape(n_s, D_MODEL)
    p_rw, u = _inproj(x_p, x_s, row2(g_mix[0]), w_in[0].astype(BF16))

    head_idx = jnp.arange(RW_WIDTH) // RW_HEAD_DIM
    eseg = (head_idx[:, None] == head_idx[None, :]).astype(BF16)
    wts = (row2(mu_shift[0]), row2(w0[0]), w_decay_up[0].astype(BF16), row2(a0[0]),
           w_iclr_up[0].astype(BF16), w_glora_up[0].astype(BF16), row2(k_k[0]), row2(k_a[0]),
           row2(r_k[0]), row2(gn_g[0]), row2(gn_b[0]), eseg)
    conv_w = (dw_w[0], row2(dw_b[0]), row2(cln_g[0]), row2(cln_b[0]))
    n_tp = tp // WKV_BLOCK
    rw_p, wkv_p, cv_p = _mixer(p_rw, u, 0, jnp.zeros((bp, 1, RW_PROJ), F32),
                               jnp.zeros((1, bp, RW_HEADS, RW_HEAD_DIM, RW_HEAD_DIM), F32),
                               jnp.zeros((1, bp, CONV_K - 1, CONV_WIDTH), F32), wts, conv_w,
                               batch=bp, n_seq=1, tt_seq=WKV_BLOCK, chunk=WKV_CHUNK, t_valid=tp, n_t=n_tp,
                               out_rows=n_p)
    pad_seq = lambda a: jnp.pad(a.reshape(bs, ts, -1), ((0, 0), (0, SAMPLE_PAD - ts), (0, 0))).reshape(
        bs * SAMPLE_PAD, -1)
    unpad_seq = lambda a: a.reshape(bs, SAMPLE_PAD, -1)[:, :ts].reshape(n_s, -1)
    p_rw_s = p_rw[n_p:].reshape(bs, ts, RW_PROJ)
    u_s = u[n_p:]
    rw_s_pad, wkv_s, cv_s_pad = _mixer(pad_seq(p_rw_s), pad_seq(u_s), 0, state_shift[0].reshape(bs, 1, RW_PROJ),
                                       state_wkv, state_conv, wts, conv_w,
                                       batch=bs, n_seq=SAMPLE_SEQS, tt_seq=SAMPLE_PAD, chunk=SAMPLE_PAD,
                                       t_valid=ts, n_t=1, out_rows=bs * SAMPLE_PAD)
    rw_s, cv_s = unpad_seq(rw_s_pad), unpad_seq(cv_s_pad)

    x1, q = _mid1(x_p, x_s, rw_p, rw_s, cv_p, cv_s, w_out[0].astype(BF16), row2(g_xattn[0]), w_q[0].astype(BF16))

    mk_p, mv_p = _memkv(mem_prompt.reshape(bp * N_MEM, D_MODEL), row2(g_mem[0]),
                        w_mk[0].astype(BF16), w_mv[0].astype(BF16))
    mk_p = mk_p.reshape(bp, N_MEM, D_MODEL)
    mv_p = mv_p.reshape(bp, N_MEM, D_MODEL)
    o_p = _attn_rows(q, mk_p, mv_p, n_p, tp // TOK_TILE)
    o_s = unpad_seq(_attn_seq(pad_seq(q[n_p:]).reshape(bs, SAMPLE_PAD, D_MODEL), cache_mem_k, cache_mem_v))

    tri = (jnp.arange(TOK_TILE)[None, :] < jnp.arange(TOK_TILE)[:, None]).astype(BF16)
    x2, h3, ids, prob, lrank, cnt3, base3, tot = _mid2(x1, o_p, o_s, w_o[0].astype(BF16), row2(g_moe[0]),
                                                       w_router[0], row2(b_router[0]), tri)

    n_t = n_all // TOK_TILE
    cnt = cnt3.astype(I32)
    c8 = ((cnt + STRIP - 1) // STRIP) * STRIP
    lbase = jnp.cumsum(c8, axis=-1) - c8
    tile_tot = jnp.sum(c8, axis=(1, 2))
    used = tot[0].astype(I32)
    padded = ((used + FFN_TILE - 1) // FFN_TILE) * FFN_TILE
    ends = jnp.cumsum(padded)
    offs = ends - padded
    gbase = offs[None, None, :] + base3.astype(I32)
    tail_n = padded - used
    n_tiles = -(-(n_all * TOP_K + n_t * N_EXPERTS * (STRIP - 1)) // FFN_TILE) + N_EXPERTS
    n_used = (ends[-1] // FFN_TILE).reshape(1)
    tile_ids = jnp.minimum(jnp.arange(n_tiles, dtype=I32), n_used - 1)
    tile_expert = jnp.sum((ends[None, :] // FFN_TILE) <= tile_ids[:, None], axis=1).astype(I32)
    tile_expert = jnp.minimum(tile_expert, N_EXPERTS - 1)

    xs = _dispatch(tile_tot, tail_n, offs + used, n_used, c8, lbase, gbase, h3, ids, lrank, n_tiles)
    bias4 = lambda b: b.reshape(1, N_EXPERTS, 1, D_MODEL)
    ys = _ffn(tile_expert, padded // FFN_TILE, offs + used, n_used, xs, w_moe_gate, bias4(b_moe_gate),
              w_moe_up, bias4(b_moe_up), w_moe_down, bias4(b_moe_down))

    y_p, y_s = _combine(tile_tot, c8, lbase, gbase, ys, x2, ids, lrank, prob, row2(g_final), n_p // TOK_TILE)

    last_rows = lambda a, rows: jnp.stack([a[(b + 1) * tp - rows:(b + 1) * tp] for b in range(bp)])
    new_conv_s = jnp.concatenate([state_conv[0], u_s.reshape(bs, ts, CONV_WIDTH)], axis=1)[:, -(CONV_K - 1):]
    kv_shape = (1, bp, N_MEM, X_HEADS, X_HEAD_DIM)
    return (y_p.reshape(bp, tp, D_MODEL), y_s.reshape(bs, ts, D_MODEL),
            last_rows(p_rw, 1).reshape(1, bp, RW_PROJ), wkv_p[None],
            last_rows(u, CONV_K - 1)[None],
            mk_p.reshape(kv_shape), mv_p.reshape(kv_shape),
            p_rw_s[:, -1][None], wkv_s[None], new_conv_s[None])
```

```python
import functools

import jax
import jax.numpy as jnp
from jax import lax
from jax.experimental import pallas as pl
from jax.experimental.pallas import tpu as pltpu

F32 = jnp.float32
BF16 = jnp.bfloat16
I32 = jnp.int32

D_MODEL = 1024
RW_HEADS = 8
RW_HEAD_DIM = 64
RW_WIDTH = RW_HEADS * RW_HEAD_DIM
DECAY_LORA = 64
ICLR_LORA = 64
GATE_LORA = 128
RW_PROJ = 3 * RW_WIDTH + DECAY_LORA + ICLR_LORA + GATE_LORA
GN_EPS = 64e-5
CONV_WIDTH = D_MODEL - RW_WIDTH
CONV_K = 31
LN_EPS = 1e-5
IN_PROJ = RW_PROJ + 2 * CONV_WIDTH
N_MEM = 256
X_HEADS = 4
X_HEAD_DIM = D_MODEL // X_HEADS
ATTN_SCALE = X_HEAD_DIM ** -0.5
N_EXPERTS = 32
TOP_K = 4
SWIGLU_LIMIT = 7.0
SWIGLU_ALPHA = 1.702
NORM_EPS = 1e-5

TOK_TILE = 512
WKV_BLOCK = 256
WKV_CHUNK = 64
SAMPLE_PAD = 8
SAMPLE_SEQS = 8
ATTN_SEQS = 4
GROUP = 4
GW = GROUP * RW_HEAD_DIM
N_GROUPS = RW_HEADS // GROUP
CHAIN_BATCH = 16
FFN_TILE = 512
STRIP = 8
SLOT_ROWS = TOP_K * TOK_TILE + N_EXPERTS * STRIP
COMBINE_PARTS = 4
CONV_PAD = 32
VMEM_LIMIT = 56 * 1024 * 1024


def _cparams(sem):
    return pltpu.CompilerParams(dimension_semantics=sem, vmem_limit_bytes=VMEM_LIMIT)


def _dot(a, b):
    return jnp.dot(a.astype(BF16), b.astype(BF16), preferred_element_type=F32)


def _dot_nt(a, b):
    return lax.dot_general(a.astype(BF16), b.astype(BF16), (((1,), (1,)), ((), ())),
                           preferred_element_type=F32)


def _dot_tn(a, b):
    return lax.dot_general(a.astype(BF16), b.astype(BF16), (((0,), (0,)), ((), ())),
                           preferred_element_type=F32)


def _split2(x):
    hi = x.astype(BF16)
    lo = (x - hi.astype(F32)).astype(BF16)
    return hi, lo


def _split3(x):
    hi = x.astype(BF16)
    r1 = x - hi.astype(F32)
    mid = r1.astype(BF16)
    lo = (r1 - mid.astype(F32)).astype(BF16)
    return hi, mid, lo


def _segsum(x, eseg):
    hi, lo = _split2(x)
    return (jnp.dot(hi, eseg, preferred_element_type=F32)
            + jnp.dot(lo, eseg, preferred_element_type=F32))


def _rms(x, g):
    return x * lax.rsqrt(jnp.mean(x * x, axis=-1, keepdims=True) + NORM_EPS) * g


def _sigmoid(x):
    return 1.0 / (1.0 + jnp.exp(-x))


def _pick(first, a_ref, b_ref):
    return jnp.where(first, a_ref[...], b_ref[...])


def _two_source_specs(width, n_pt):
    return (pl.BlockSpec((TOK_TILE, width), lambda i: (jnp.minimum(i, n_pt - 1), 0)),
            pl.BlockSpec((TOK_TILE, width), lambda i: (jnp.maximum(i - n_pt, 0), 0)))


def _inproj_kernel(xp_ref, xs_ref, g_ref, w_ref, prw_ref, u_ref, *, n_pt):
    h = _rms(_pick(pl.program_id(0) < n_pt, xp_ref, xs_ref), g_ref[...])
    p = _dot(h, w_ref[...])
    prw_ref[...] = p[:, :RW_PROJ]
    u_ref[...] = p[:, RW_PROJ:RW_PROJ + CONV_WIDTH] * _sigmoid(p[:, RW_PROJ + CONV_WIDTH:])


def _inproj(x_p, x_s, g_mix, w_in_bf):
    n = x_p.shape[0] + x_s.shape[0]
    n_pt = x_p.shape[0] // TOK_TILE
    return pl.pallas_call(
        functools.partial(_inproj_kernel, n_pt=n_pt),
        grid=(n // TOK_TILE,),
        in_specs=[*_two_source_specs(D_MODEL, n_pt),
                  pl.BlockSpec((1, D_MODEL), lambda i: (0, 0)),
                  pl.BlockSpec((D_MODEL, IN_PROJ), lambda i: (0, 0))],
        out_specs=[pl.BlockSpec((TOK_TILE, RW_PROJ), lambda i: (i, 0)),
                   pl.BlockSpec((TOK_TILE, CONV_WIDTH), lambda i: (i, 0))],
        out_shape=[jax.ShapeDtypeStruct((n, RW_PROJ), F32),
                   jax.ShapeDtypeStruct((n, CONV_WIDTH), F32)],
        compiler_params=_cparams(("parallel",)),
        name="inproj",
    )(x_p, x_s, g_mix, w_in_bf)


def _bd(x, w, masks):
    r, lanes = x.shape
    key = (r, lanes, w)
    if key not in masks:
        head = lax.broadcasted_iota(I32, (r, lanes), 1) >> (w.bit_length() - 1)
        masks[key] = [head == h for h in range(GROUP)]
    return jnp.concatenate([jnp.where(m, x, 0.0) for m in masks[key]], axis=0).astype(BF16)


def _collapse(x):
    head = lax.broadcasted_iota(I32, (RW_HEAD_DIM, GW), 1) >> (RW_HEAD_DIM.bit_length() - 1)
    out = jnp.zeros((RW_HEAD_DIM, GW), F32)
    for h in range(GROUP):
        out = out + jnp.where(head == h, x[h * RW_HEAD_DIM:(h + 1) * RW_HEAD_DIM, :], 0.0)
    return out


def _conv_module(t, u_ref, hist_ref, cw_ref, cb_ref, cg_ref, cbeta_ref, cv_ref, ext_ref, sh_ref,
                 *, n_seq, tt_seq, n_t):
    first = CONV_PAD - (CONV_K - 1)
    span = tt_seq + CONV_PAD - 8
    for s in range(n_seq):
        rows = slice(s * tt_seq, (s + 1) * tt_seq)

        @pl.when(t == 0)
        def _():
            ext_ref[s, 0:8, :] = jnp.zeros((8, CONV_WIDTH), F32)
            ext_ref[s, first:CONV_PAD, :] = hist_ref[0, s]

        ext_ref[s, CONV_PAD:CONV_PAD + tt_seq, :] = u_ref[rows, :]
        for b in range(1, 8):
            sh_ref[b, 0:span, :] = ext_ref[s, pl.ds(b, span), :]
        acc = jnp.zeros((tt_seq, CONV_WIDTH), F32) + cb_ref[...]
        for j in range(CONV_K):
            a8, b = divmod(first + j, 8)
            win = ext_ref[s, pl.ds(8 * a8, tt_seq), :] if b == 0 else sh_ref[b, pl.ds(8 * a8, tt_seq), :]
            acc = acc + win * cw_ref[j:j + 1, :]
        mean = jnp.mean(acc, axis=-1, keepdims=True)
        cen = acc - mean
        var = jnp.mean(cen * cen, axis=-1, keepdims=True)
        cn = cen * lax.rsqrt(var + LN_EPS) * cg_ref[...] + cbeta_ref[...]
        cv_ref[rows, :] = cn * _sigmoid(cn)
        if n_t > 1:
            ext_ref[s, 0:CONV_PAD, :] = ext_ref[s, tt_seq:tt_seq + CONV_PAD, :]


def _wkv_kernel(p_ref, prev_ref, s0_ref, mu_ref, w0_ref, wdup_ref, a0_ref, aup_ref, gup_ref,
                kk_ref, ka_ref, rk_ref, gng_ref, gnb_ref, eseg_ref,
                u_ref, hist_ref, cw_ref, cb_ref, cg_ref, cbeta_ref,
                out_ref, sfin_ref, cv_ref,
                s_ref, last_ref, rt_ref, at_ref, bt_ref, kt_ref, v_ref, pc_ref, y_ref,
                r_ref, km_ref, g_ref, ext_ref, sh_ref, *, n_seq, tt_seq, chunk, t_valid, n_t):
    t = pl.program_id(1)
    tt = n_seq * tt_seq
    n_chunk = tt_seq // chunk
    n_fused = chunk.bit_length() - 2
    nd = RW_HEAD_DIM
    c1, c2, c3 = RW_WIDTH, 2 * RW_WIDTH, 3 * RW_WIDTH
    c4 = c3 + DECAY_LORA
    c5 = c4 + ICLR_LORA

    @pl.when(t == 0)
    def _():
        for s in range(n_seq):
            for h in range(RW_HEADS):
                g, hg = divmod(h, GROUP)
                s_ref[s, g, :, hg * nd:(hg + 1) * nd] = s0_ref[0, s, h]
        if n_seq == 1:
            last_ref[0:1, :] = prev_ref[0]

    _conv_module(t, u_ref, hist_ref, cw_ref, cb_ref, cg_ref, cbeta_ref, cv_ref, ext_ref, sh_ref,
                 n_seq=n_seq, tt_seq=tt_seq, n_t=n_t)

    p = p_ref[...]
    row = lax.broadcasted_iota(I32, (tt, 1), 0)
    seq_row = row & (tt_seq - 1)
    rolled = pltpu.roll(p, shift=1, axis=0)
    if n_seq == 1:
        ps = jnp.where(seq_row == 0, last_ref[0:1, :], rolled)
        last_ref[0:1, :] = p[tt - 1:tt, :]
    else:
        prev_rows = jnp.concatenate(
            [jnp.broadcast_to(prev_ref[q], (tt_seq, RW_PROJ)) for q in range(n_seq)], axis=0)
        ps = jnp.where(seq_row == 0, prev_rows, rolled)
    z = p + mu_ref[...] * (ps - p)
    r, k, v = z[:, :c1], z[:, c1:c2], z[:, c2:c3]
    wd, ad, gd = z[:, c3:c4], z[:, c4:c5], z[:, c5:]
    wpre = w0_ref[...] + _dot(jnp.tanh(wd), wdup_ref[...])
    neg = -wpre
    w = -(jnp.maximum(neg, 0.0) + jnp.log(1.0 + jnp.exp(-jnp.abs(neg)))) - 0.5
    a = _sigmoid(a0_ref[...] + _dot(ad, aup_ref[...]))
    g_ref[...] = _dot(_sigmoid(gd), gup_ref[...])
    eseg = eseg_ref[...]
    kk = k * kk_ref[...]
    kk = kk / jnp.maximum(jnp.sqrt(_segsum(kk * kk, eseg)), 1e-12)
    kmod = k * (1.0 + (a - 1.0) * ka_ref[...])
    logd = -jnp.exp(w)
    if t_valid < tt_seq * n_t:
        valid = (t * tt_seq + seq_row) < t_valid
        logd = jnp.where(valid, logd, 0.0)
        kk = jnp.where(valid, kk, 0.0)
        kmod = jnp.where(valid, kmod, 0.0)
        v = jnp.where(valid, v, 0.0)
    ri = lax.broadcasted_iota(I32, (tt, tt), 0)
    ci = lax.broadcasted_iota(I32, (tt, tt), 1)
    shift = chunk.bit_length() - 1
    ltri = jnp.where(ci <= ri, jnp.where((ri >> shift) == (ci >> shift), 1.0, 0.0), 0.0).astype(BF16)
    hi, mid, lo = _split3(logd)
    logp = (jnp.dot(ltri, hi, preferred_element_type=F32)
            + jnp.dot(ltri, mid, preferred_element_type=F32)
            + jnp.dot(ltri, lo, preferred_element_type=F32))
    pcum = jnp.exp(logp)
    pinv = jnp.exp(-logp)
    r_ref[...] = r
    km_ref[...] = kmod
    v_ref[...] = v
    pc_ref[...] = pcum
    rt_ref[...] = r * pcum
    at_ref[...] = -kk * jnp.exp(logp - logd)
    bt_ref[...] = kk * a * pinv
    kt_ref[...] = kmod * pinv

    cw = GROUP * chunk
    ti = lax.broadcasted_iota(I32, (chunk, cw), 0)
    si = lax.broadcasted_iota(I32, (chunk, cw), 1) & (chunk - 1)
    strict = si < ti
    incl = si <= ti
    eye_c = jnp.where(si == ti, 1.0, 0.0)
    ji = lax.broadcasted_iota(I32, (nd, GW), 0)
    jj = lax.broadcasted_iota(I32, (nd, GW), 1) & (nd - 1)
    eye_n = jnp.where(ji == jj, 1.0, 0.0)

    all_chains = [(s, c, g) for s in range(n_seq) for c in range(n_chunk) for g in range(N_GROUPS)]
    masks = {}

    def blk(ref, s, c, g):
        r0 = s * tt_seq + c * chunk
        return ref[r0:r0 + chunk, g * GW:(g + 1) * GW]

    rp, yv, mm, gm = {}, {}, {}, {}

    def chunk_operators(chains):
        am = {q: blk(at_ref, *q) for q in chains}
        rm = {q: blk(rt_ref, *q) for q in chains}
        bm = {q: blk(bt_ref, *q) for q in chains}
        km = {q: blk(kt_ref, *q) for q in chains}
        vm = {q: blk(v_ref, *q) for q in chains}
        a_ab, a_ak, a_rb, a_rk = {}, {}, {}, {}
        for q in chains:
            ar = jnp.concatenate([am[q], rm[q]], axis=0)
            s1 = _dot_nt(ar, _bd(bm[q], nd, masks))
            s2 = _dot_nt(ar, _bd(km[q], nd, masks))
            a_ab[q] = jnp.where(strict, s1[:chunk], 0.0)
            a_rb[q] = jnp.where(incl, s1[chunk:], 0.0)
            a_ak[q] = jnp.where(strict, s2[:chunk], 0.0)
            a_rk[q] = jnp.where(incl, s2[chunk:], 0.0)
        tinv = {q: eye_c + a_ab[q] for q in chains}
        pw = {q: _dot(a_ab[q], _bd(a_ab[q], chunk, masks)) for q in chains}
        for lvl in range(n_fused):
            last = lvl == n_fused - 1
            for q in chains:
                lhs = tinv[q] if last else jnp.concatenate([tinv[q], pw[q]], axis=0)
                zz = _dot(lhs, _bd(pw[q], chunk, masks))
                tinv[q] = tinv[q] + zz[:chunk]
                if not last:
                    pw[q] = zz[chunk:]
        akv, rkv = {}, {}
        for q in chains:
            zv = _dot(jnp.concatenate([a_ak[q], a_rk[q]], axis=0), _bd(vm[q], nd, masks))
            akv[q], rkv[q] = zv[:chunk], zv[chunk:]
        ap = {q: _dot(tinv[q], _bd(am[q], nd, masks)) for q in chains}
        uv = {q: _dot(tinv[q], _bd(akv[q], nd, masks)) for q in chains}
        for q in chains:
            rp[q] = rm[q] + _dot(a_rb[q], _bd(ap[q], nd, masks))
            yv[q] = _dot(a_rb[q], _bd(uv[q], nd, masks)) + rkv[q]
        for q in chains:
            s, c, g = q
            r_end = s * tt_seq + (c + 1) * chunk
            pc = pc_ref[r_end - 8:r_end, g * GW:(g + 1) * GW][7:8, :]
            cross = _dot_tn(jnp.concatenate([ap[q], uv[q]], axis=1), bm[q])
            vk = _dot_tn(vm[q], km[q])
            mm[q] = (eye_n + _collapse(cross[:GW])) * pc
            gm[q] = (_collapse(cross[GW:]) + _collapse(vk)) * pc

    for i0 in range(0, len(all_chains), CHAIN_BATCH):
        chunk_operators(all_chains[i0:i0 + CHAIN_BATCH])

    for s in range(n_seq):
        for g in range(N_GROUPS):
            st = s_ref[s, g]
            for c in range(n_chunk):
                q = (s, c, g)
                r0 = s * tt_seq + c * chunk
                y_ref[r0:r0 + chunk, g * GW:(g + 1) * GW] = _dot_nt(rp[q], _bd(st, nd, masks)) + yv[q]
                st = _dot(st, _bd(mm[q], nd, masks)) + gm[q]
            s_ref[s, g] = st

    y = y_ref[...]
    inv_n = 1.0 / RW_HEAD_DIM
    mean = _segsum(y, eseg) * inv_n
    yc = y - mean
    var = _segsum(yc * yc, eseg) * inv_n
    yn = yc * lax.rsqrt(var + GN_EPS) * gng_ref[...] + gnb_ref[...]
    bonus = _segsum(r_ref[...] * km_ref[...] * rk_ref[...], eseg) * v_ref[...]
    out_ref[...] = (yn + bonus) * g_ref[...]

    @pl.when(t == n_t - 1)
    def _():
        for s in range(n_seq):
            for h in range(RW_HEADS):
                g, hg = divmod(h, GROUP)
                sfin_ref[s, h] = s_ref[s, g, :, hg * nd:(hg + 1) * nd]


def _mixer(p_rows, u_rows, row_block0, prev, s0, hist, wts, conv_w, *, batch, n_seq, tt_seq, chunk, t_valid,
           n_t, out_rows):
    (mu, w0, wdup, a0, aup, gup, k_k, k_a, r_k, gn_g, gn_b, eseg) = wts
    dw_w, dw_b, ln_g, ln_b = conv_w
    tt = n_seq * tt_seq
    kern = functools.partial(_wkv_kernel, n_seq=n_seq, tt_seq=tt_seq, chunk=chunk, t_valid=t_valid, n_t=n_t)
    const = lambda shape: pl.BlockSpec(shape, lambda b, t: tuple(0 for _ in shape))
    row_out = lambda w: pl.BlockSpec((tt, w), lambda b, t: (b * n_t + t, 0))
    blk = pltpu.VMEM((tt, RW_WIDTH), F32)
    st_spec = pl.BlockSpec((n_seq, RW_HEADS, RW_HEAD_DIM, RW_HEAD_DIM), lambda b, t: (b, 0, 0, 0))
    return pl.pallas_call(
        kern,
        grid=(batch // n_seq, n_t),
        in_specs=[pl.BlockSpec((tt, RW_PROJ), lambda b, t: (row_block0 + b * n_t + t, 0)),
                  pl.BlockSpec((n_seq, 1, RW_PROJ), lambda b, t: (b, 0, 0)),
                  pl.BlockSpec((1, n_seq, RW_HEADS, RW_HEAD_DIM, RW_HEAD_DIM), lambda b, t: (0, b, 0, 0, 0)),
                  const((1, RW_PROJ)), const((1, RW_WIDTH)), const((DECAY_LORA, RW_WIDTH)),
                  const((1, RW_WIDTH)), const((ICLR_LORA, RW_WIDTH)), const((GATE_LORA, RW_WIDTH)),
                  const((1, RW_WIDTH)), const((1, RW_WIDTH)), const((1, RW_WIDTH)),
                  const((1, RW_WIDTH)), const((1, RW_WIDTH)), const((RW_WIDTH, RW_WIDTH)),
                  pl.BlockSpec((tt, CONV_WIDTH), lambda b, t: (row_block0 + b * n_t + t, 0)),
                  pl.BlockSpec((1, n_seq, CONV_K - 1, CONV_WIDTH), lambda b, t: (0, b, 0, 0)),
                  const((CONV_K, CONV_WIDTH)), const((1, CONV_WIDTH)),
                  const((1, CONV_WIDTH)), const((1, CONV_WIDTH))],
        out_specs=[row_out(RW_WIDTH), st_spec, row_out(CONV_WIDTH)],
        out_shape=[jax.ShapeDtypeStruct((out_rows, RW_WIDTH), F32),
                   jax.ShapeDtypeStruct((batch, RW_HEADS, RW_HEAD_DIM, RW_HEAD_DIM), F32),
                   jax.ShapeDtypeStruct((out_rows, CONV_WIDTH), F32)],
        scratch_shapes=[pltpu.VMEM((n_seq, N_GROUPS, RW_HEAD_DIM, GW), F32),
                        pltpu.VMEM((8, RW_PROJ), F32),
                        blk, blk, blk, blk, blk, blk, blk, blk, blk, blk,
                        pltpu.VMEM((n_seq, CONV_PAD + tt_seq, CONV_WIDTH), F32),
                        pltpu.VMEM((8, CONV_PAD + tt_seq, CONV_WIDTH), F32)],
        compiler_params=_cparams(("parallel", "arbitrary")),
        name="mixer",
    )(p_rows, prev, s0, mu, w0, wdup, a0, aup, gup, k_k, k_a, r_k, gn_g, gn_b, eseg,
      u_rows, hist, dw_w, dw_b, ln_g, ln_b)


def _mid1_kernel(xp_ref, xs_ref, rwp_ref, rws_ref, cvp_ref, cvs_ref, wo_ref, g_ref, wq_ref, x1_ref, q_ref,
                 *, n_pt):
    first = pl.program_id(0) < n_pt
    rw = _pick(first, rwp_ref, rws_ref)
    cv = _pick(first, cvp_ref, cvs_ref)
    mix = _dot(rw, wo_ref[:RW_WIDTH, :]) + _dot(cv, wo_ref[RW_WIDTH:, :])
    x1 = _pick(first, xp_ref, xs_ref) + mix
    x1_ref[...] = x1
    q = _dot(_rms(x1, g_ref[...]), wq_ref[...])
    q_ref[...] = (q * ATTN_SCALE).astype(BF16)


def _mid1(x_p, x_s, rw_p, rw_s, cv_p, cv_s, w_out_bf, g_x, w_q_bf):
    n = x_p.shape[0] + x_s.shape[0]
    n_pt = rw_p.shape[0] // TOK_TILE
    row = lambda w: pl.BlockSpec((TOK_TILE, w), lambda i: (i, 0))
    full = lambda a, b: pl.BlockSpec((a, b), lambda i: (0, 0))
    return pl.pallas_call(
        functools.partial(_mid1_kernel, n_pt=n_pt),
        grid=(n // TOK_TILE,),
        in_specs=[*_two_source_specs(D_MODEL, n_pt), *_two_source_specs(RW_WIDTH, n_pt),
                  *_two_source_specs(CONV_WIDTH, n_pt),
                  full(D_MODEL, D_MODEL), full(1, D_MODEL), full(D_MODEL, D_MODEL)],
        out_specs=[row(D_MODEL), row(D_MODEL)],
        out_shape=[jax.ShapeDtypeStruct((n, D_MODEL), F32), jax.ShapeDtypeStruct((n, D_MODEL), BF16)],
        compiler_params=_cparams(("parallel",)),
        name="mid1",
    )(x_p, x_s, rw_p, rw_s, cv_p, cv_s, w_out_bf, g_x, w_q_bf)


def _memkv_kernel(m_ref, g_ref, wk_ref, wv_ref, k_ref, v_ref):
    m = _rms(m_ref[...], g_ref[...])
    k_ref[...] = _dot(m, wk_ref[...])
    v_ref[...] = _dot(m, wv_ref[...])


def _memkv(mem, g_mem, w_mk_bf, w_mv_bf):
    n = mem.shape[0]
    tile = N_MEM
    row = pl.BlockSpec((tile, D_MODEL), lambda i: (i, 0))
    full = lambda a, b: pl.BlockSpec((a, b), lambda i: (0, 0))
    return pl.pallas_call(
        _memkv_kernel,
        grid=(n // tile,),
        in_specs=[row, full(1, D_MODEL), full(D_MODEL, D_MODEL), full(D_MODEL, D_MODEL)],
        out_specs=[row, row],
        out_shape=[jax.ShapeDtypeStruct((n, D_MODEL), F32), jax.ShapeDtypeStruct((n, D_MODEL), F32)],
        compiler_params=_cparams(("parallel",)),
        name="memkv",
    )(mem, g_mem, w_mk_bf, w_mv_bf)


def _attn_head(q_h, k_h, v_h):
    s = _dot_nt(q_h, k_h)
    e = jnp.exp(s - jnp.max(s, axis=-1, keepdims=True))
    pr = e / jnp.sum(e, axis=-1, keepdims=True)
    return _dot(pr, v_h).astype(BF16)


def _attn_rows_kernel(q_ref, k_ref, v_ref, o_ref):
    q = q_ref[...]
    for h in range(X_HEADS):
        hs = slice(h * X_HEAD_DIM, (h + 1) * X_HEAD_DIM)
        o_ref[:, hs] = _attn_head(q[:, hs], k_ref[0, :, hs], v_ref[0, :, hs])


def _attn_seq_kernel(q_ref, k_hbm, v_hbm, o_ref, kbuf, vbuf, sem, *, n_steps):
    i = pl.program_id(0)
    slot = lax.rem(i, 2)

    def copies(step, s):
        out = []
        for g in range(ATTN_SEQS):
            seq = step * ATTN_SEQS + g
            for h in range(X_HEADS):
                lanes = pl.ds(h * X_HEAD_DIM, X_HEAD_DIM)
                out.append(pltpu.make_async_copy(k_hbm.at[0, seq, :, h, :], kbuf.at[s, g, :, lanes], sem.at[s]))
                out.append(pltpu.make_async_copy(v_hbm.at[0, seq, :, h, :], vbuf.at[s, g, :, lanes], sem.at[s]))
        return out

    @pl.when(i == 0)
    def _():
        for cp in copies(0, 0):
            cp.start()

    @pl.when(i + 1 < n_steps)
    def _():
        for cp in copies(i + 1, 1 - slot):
            cp.start()

    for cp in copies(i, slot):
        cp.wait()
    head_of_lane = lax.broadcasted_iota(I32, (SAMPLE_PAD * X_HEADS, D_MODEL), 1) >> (X_HEAD_DIM.bit_length() - 1)
    head_of_row = lax.broadcasted_iota(I32, (SAMPLE_PAD * X_HEADS, D_MODEL), 0) >> (SAMPLE_PAD.bit_length() - 1)
    own = head_of_lane == head_of_row
    for g in range(ATTN_SEQS):
        q8 = q_ref[g].astype(F32)
        qbd = jnp.where(own, jnp.concatenate([q8] * X_HEADS, axis=0), 0.0)
        s = _dot_nt(qbd, kbuf[slot, g])
        e = jnp.exp(s - jnp.max(s, axis=-1, keepdims=True))
        pr = e / jnp.sum(e, axis=-1, keepdims=True)
        o_all = jnp.where(own, _dot(pr, vbuf[slot, g]), 0.0)
        o8 = o_all[0:SAMPLE_PAD]
        for h in range(1, X_HEADS):
            o8 = o8 + o_all[SAMPLE_PAD * h:SAMPLE_PAD * (h + 1)]
        o_ref[g] = o8.astype(BF16)


def _attn_rows(q, mk, mv, n_rows, tiles_per_seq):
    qspec = pl.BlockSpec((TOK_TILE, D_MODEL), lambda i, j: (i * tiles_per_seq + j, 0))
    kvspec = pl.BlockSpec((1, N_MEM, D_MODEL), lambda i, j: (i, 0, 0))
    return pl.pallas_call(
        _attn_rows_kernel,
        grid=(mk.shape[0], tiles_per_seq),
        in_specs=[qspec, kvspec, kvspec],
        out_specs=qspec,
        out_shape=jax.ShapeDtypeStruct((n_rows, D_MODEL), BF16),
        compiler_params=_cparams(("parallel", "parallel")),
        name="attn",
    )(q, mk, mv)


def _attn_seq(q3, mk, mv):
    b, t, _ = q3.shape
    qspec = pl.BlockSpec((ATTN_SEQS, t, D_MODEL), lambda i: (i, 0, 0))
    anyspec = pl.BlockSpec(memory_space=pl.ANY)
    head_buf = pltpu.VMEM((2, ATTN_SEQS, N_MEM, D_MODEL), F32)
    return pl.pallas_call(
        functools.partial(_attn_seq_kernel, n_steps=b // ATTN_SEQS),
        grid=(b // ATTN_SEQS,),
        in_specs=[qspec, anyspec, anyspec],
        out_specs=qspec,
        out_shape=jax.ShapeDtypeStruct((b, t, D_MODEL), BF16),
        scratch_shapes=[head_buf, head_buf, pltpu.SemaphoreType.DMA((2,))],
        compiler_params=_cparams(("arbitrary",)),
        name="attn",
    )(q3, mk, mv)


def _mid2_kernel(x1_ref, op_ref, os_ref, wo_ref, g_ref, wr_ref, br_ref, tri_ref,
                 x2_ref, h3_ref, ids_ref, prob_ref, lrank_ref, cnt_ref, base_ref, tot_ref, carry_ref, *, n_pt):
    i = pl.program_id(0)

    @pl.when(i == 0)
    def _():
        carry_ref[...] = jnp.zeros_like(carry_ref)

    x2 = x1_ref[...] + _dot(_pick(i < n_pt, op_ref, os_ref), wo_ref[...])
    x2_ref[...] = x2
    h3 = _rms(x2, g_ref[...])
    h3_ref[...] = h3.astype(BF16)
    h3_hi, h3_lo = _split2(h3)
    wr_hi, wr_lo = _split2(wr_ref[...])
    logits = (jnp.dot(h3_hi, wr_hi, preferred_element_type=F32)
              + jnp.dot(h3_hi, wr_lo, preferred_element_type=F32)
              + jnp.dot(h3_lo, wr_hi, preferred_element_type=F32)) + br_ref[...]
    n = logits.shape[0]
    lane = lax.broadcasted_iota(I32, (n, N_EXPERTS), 1)
    work = logits
    vals, ids = [], []
    for _ in range(TOP_K):
        m = jnp.max(work, axis=-1, keepdims=True)
        idx = jnp.min(jnp.where(work == m, lane, N_EXPERTS), axis=-1, keepdims=True)
        vals.append(m)
        ids.append(idx)
        work = jnp.where(lane == idx, -jnp.inf, work)
    exps = [jnp.exp(vk - vals[0]) for vk in vals]
    den = exps[0] + exps[1] + exps[2] + exps[3]
    mask = jnp.zeros((n, N_EXPERTS), F32)
    for idx in ids:
        mask = mask + jnp.where(lane == idx, 1.0, 0.0)
    lrank = jnp.dot(tri_ref[...], mask.astype(BF16), preferred_element_type=F32)
    cnt = jnp.sum(mask, axis=0, keepdims=True)
    cnt_ref[0] = cnt
    base_ref[0] = carry_ref[0:1, :]
    carry_ref[0:1, :] = carry_ref[0:1, :] + jnp.floor((cnt + (STRIP - 1)) * (1.0 / STRIP)) * STRIP
    tot_ref[...] = carry_ref[0:1, :]
    for kq in range(TOP_K):
        ids_ref[:, kq:kq + 1] = ids[kq]
        prob_ref[:, kq:kq + 1] = exps[kq] / den
        lrank_ref[:, kq:kq + 1] = jnp.sum(jnp.where(lane == ids[kq], lrank, 0.0), axis=-1, keepdims=True)


def _mid2(x1, o_p, o_s, w_o_bf, g_moe, w_router, b_router, tri):
    n = x1.shape[0]
    n_t = n // TOK_TILE
    n_pt = o_p.shape[0] // TOK_TILE
    row = lambda w: pl.BlockSpec((TOK_TILE, w), lambda i: (i, 0))
    full = lambda a, b: pl.BlockSpec((a, b), lambda i: (0, 0))
    per_tile = pl.BlockSpec((1, 1, N_EXPERTS), lambda i: (i, 0, 0))
    return pl.pallas_call(
        functools.partial(_mid2_kernel, n_pt=n_pt),
        grid=(n_t,),
        in_specs=[row(D_MODEL), *_two_source_specs(D_MODEL, n_pt), full(D_MODEL, D_MODEL), full(1, D_MODEL),
                  full(D_MODEL, N_EXPERTS), full(1, N_EXPERTS), full(TOK_TILE, TOK_TILE)],
        out_specs=[row(D_MODEL), row(D_MODEL), row(TOP_K), row(TOP_K), row(TOP_K), per_tile, per_tile,
                   full(1, N_EXPERTS)],
        out_shape=[jax.ShapeDtypeStruct((n, D_MODEL), F32), jax.ShapeDtypeStruct((n, D_MODEL), BF16),
                   jax.ShapeDtypeStruct((n, TOP_K), I32), jax.ShapeDtypeStruct((n, TOP_K), F32),
                   jax.ShapeDtypeStruct((n, TOP_K), F32),
                   jax.ShapeDtypeStruct((n_t, 1, N_EXPERTS), F32), jax.ShapeDtypeStruct((n_t, 1, N_EXPERTS), F32),
                   jax.ShapeDtypeStruct((1, N_EXPERTS), F32)],
        scratch_shapes=[pltpu.VMEM((8, N_EXPERTS), F32)],
        compiler_params=_cparams(("arbitrary",)),
        name="mid2",
    )(x1, o_p, o_s, w_o_bf, g_moe, w_router, b_router, tri)


def _slot_select(ids_ref, lrank_ref, lbase_v, weights):
    n = ids_ref.shape[0]
    lane = lax.broadcasted_iota(I32, (n, N_EXPERTS), 1)
    col = lax.broadcasted_iota(I32, (n, SLOT_ROWS), 1)
    sel = jnp.zeros((n, SLOT_ROWS), F32)
    for kq in range(TOP_K):
        base = jnp.sum(jnp.where(lane == ids_ref[:, kq:kq + 1], lbase_v, 0.0), axis=-1, keepdims=True)
        dest = (base + lrank_ref[:, kq:kq + 1]).astype(I32)
        sel = jnp.where(col == dest, weights[kq], sel)
    return sel.astype(BF16)


def _strip_copies(c8_ref, lb_ref, gb_ref, make):
    for e in range(N_EXPERTS):
        rows = c8_ref[0, 0, e]

        @pl.when(rows > 0)
        def _():
            make(pl.multiple_of(lb_ref[0, 0, e], STRIP), pl.multiple_of(gb_ref[0, 0, e], STRIP),
                 pl.multiple_of(rows, STRIP)).start()


def _dispatch_kernel(tot_ref, tailn_ref, tailo_ref, nt_ref,
                     c8_ref, lb_ref, gb_ref, h3_ref, ids_ref, lrank_ref, lbv_ref,
                     xs_hbm, cbuf, zbuf, sem, *, n_t, n_tiles):
    i = pl.program_id(0)
    slot = lax.rem(i, 2)
    sel = _slot_select(ids_ref, lrank_ref, lbv_ref[0], [1.0] * TOP_K)
    cbuf[slot] = _dot_tn(sel, h3_ref[...])
    _strip_copies(c8_ref, lb_ref, gb_ref,
                  lambda lr, gr, rows: pltpu.make_async_copy(cbuf.at[slot, pl.ds(lr, rows), :],
                                                             xs_hbm.at[pl.ds(gr, rows), :], sem.at[slot]))

    def wait_rows(s, rows):
        rows = pl.multiple_of(rows, STRIP)
        pltpu.make_async_copy(cbuf.at[s, pl.ds(0, rows), :], xs_hbm.at[pl.ds(0, rows), :], sem.at[s]).wait()

    @pl.when(i > 0)
    def _():
        wait_rows(1 - slot, tot_ref[jnp.maximum(i - 1, 0)])

    @pl.when(i == n_t - 1)
    def _():
        wait_rows(slot, tot_ref[i])
        zbuf[...] = jnp.zeros_like(zbuf)
        for e in range(N_EXPERTS):
            rows = pl.multiple_of(tailn_ref[e], STRIP)

            @pl.when(rows > 0)
            def _():
                pltpu.make_async_copy(zbuf.at[pl.ds(0, rows), :],
                                      xs_hbm.at[pl.ds(pl.multiple_of(tailo_ref[e], STRIP), rows), :],
                                      sem.at[2]).start()

        for e in range(N_EXPERTS):
            rows = pl.multiple_of(tailn_ref[e], STRIP)

            @pl.when(rows > 0)
            def _():
                pltpu.make_async_copy(zbuf.at[pl.ds(0, rows), :], xs_hbm.at[pl.ds(0, rows), :], sem.at[2]).wait()

        def tile_copy(j):
            return pltpu.make_async_copy(
                zbuf, xs_hbm.at[pl.ds(pl.multiple_of(j * FFN_TILE, FFN_TILE), FFN_TILE), :], sem.at[2])

        def start_tile(j, c):
            tile_copy(j).start()
            return c

        def wait_tile(j, c):
            tile_copy(j).wait()
            return c

        lax.fori_loop(nt_ref[0], n_tiles, start_tile, 0)
        lax.fori_loop(nt_ref[0], n_tiles, wait_tile, 0)


def _dispatch(tile_tot, tail_n, tail_off, n_used, c8, lbase, gbase, h3, ids, lrank, n_tiles):
    n_t = c8.shape[0]
    smem = pl.BlockSpec((1, 1, N_EXPERTS), lambda i, *_: (i, 0, 0), memory_space=pltpu.SMEM)
    row = lambda w: pl.BlockSpec((TOK_TILE, w), lambda i, *_: (i, 0))
    grid_spec = pltpu.PrefetchScalarGridSpec(
        num_scalar_prefetch=4,
        grid=(n_t,),
        in_specs=[smem, smem, smem, row(D_MODEL), row(TOP_K), row(TOP_K),
                  pl.BlockSpec((1, 1, N_EXPERTS), lambda i, *_: (i, 0, 0))],
        out_specs=pl.BlockSpec(memory_space=pl.ANY),
        scratch_shapes=[pltpu.VMEM((2, SLOT_ROWS, D_MODEL), F32), pltpu.VMEM((FFN_TILE, D_MODEL), F32),
                        pltpu.SemaphoreType.DMA((3,))],
    )
    return pl.pallas_call(
        functools.partial(_dispatch_kernel, n_t=n_t, n_tiles=n_tiles),
        grid_spec=grid_spec,
        out_shape=jax.ShapeDtypeStruct((n_tiles * FFN_TILE, D_MODEL), F32),
        compiler_params=_cparams(("arbitrary",)),
        name="dispatch",
    )(tile_tot, tail_n, tail_off, n_used, c8, lbase, gbase, h3, ids, lrank, lbase.astype(F32))


def _ffn_kernel(te_ref, ne_ref, lim_ref, nt_ref, xs_ref, wg_hbm, bg_ref, wu_hbm, bu_ref, wd_hbm, bd_ref, y_ref,
                wf32, wbf, turn, sem):
    i = pl.program_id(0)
    n_used = nt_ref[0]
    e = te_ref[i]

    def fetch(expert, s):
        return [pltpu.make_async_copy(w_hbm.at[0, expert], wf32.at[s, j], sem.at[s])
                for j, w_hbm in enumerate((wg_hbm, wu_hbm, wd_hbm))]

    @pl.when(i == 0)
    def _():
        turn[0] = 0
        for cp in fetch(e, 0):
            cp.start()

    @pl.when(jnp.logical_and(i < n_used, jnp.logical_or(i == 0, e != te_ref[jnp.maximum(i - 1, 0)])))
    def _():
        s = turn[0]
        for cp in fetch(e, s):
            cp.wait()
        nxt = i + ne_ref[e]

        @pl.when(nxt < n_used)
        def _():
            for cp in fetch(te_ref[jnp.minimum(nxt, te_ref.shape[0] - 1)], 1 - s):
                cp.start()

        for j in range(3):
            wbf[j] = wf32[s, j].astype(BF16)
        turn[0] = 1 - s

    def expert(rows):
        x = xs_ref[0:rows, :].astype(BF16)
        gate = jnp.minimum(jnp.dot(x, wbf[0], preferred_element_type=F32) + bg_ref[0, 0], SWIGLU_LIMIT)
        up = jnp.clip(jnp.dot(x, wbf[1], preferred_element_type=F32) + bu_ref[0, 0],
                      -SWIGLU_LIMIT, SWIGLU_LIMIT)
        hid = (up + 1.0) * gate * _sigmoid(gate * SWIGLU_ALPHA)
        y_ref[0:rows, :] = jnp.dot(hid.astype(BF16), wbf[2], preferred_element_type=F32) + bd_ref[0, 0]

    half = FFN_TILE // 2
    valid = lim_ref[e] - i * FFN_TILE

    @pl.when(jnp.logical_and(i < n_used, valid > half))
    def _():
        expert(FFN_TILE)

    @pl.when(jnp.logical_and(i < n_used, valid <= half))
    def _():
        expert(half)
        y_ref[half:, :] = jnp.zeros((FFN_TILE - half, D_MODEL), F32)

    @pl.when(i >= n_used)
    def _():
        y_ref[...] = jnp.zeros_like(y_ref)


def _ffn(tile_expert, expert_tiles, expert_end, n_used, xs, wg, bg, wu, bu, wd, bd):
    n_tiles = xs.shape[0] // FFN_TILE
    anyspec = pl.BlockSpec(memory_space=pl.ANY)
    bspec = pl.BlockSpec((1, 1, 1, D_MODEL), lambda i, te, ne, lim, nt: (0, te[i], 0, 0))
    grid_spec = pltpu.PrefetchScalarGridSpec(
        num_scalar_prefetch=4,
        grid=(n_tiles,),
        in_specs=[pl.BlockSpec((FFN_TILE, D_MODEL), lambda i, te, ne, lim, nt: (jnp.minimum(i, nt[0] - 1), 0)),
                  anyspec, bspec, anyspec, bspec, anyspec, bspec],
        out_specs=pl.BlockSpec((FFN_TILE, D_MODEL), lambda i, te, ne, lim, nt: (i, 0)),
        scratch_shapes=[pltpu.VMEM((2, 3, D_MODEL, D_MODEL), F32), pltpu.VMEM((3, D_MODEL, D_MODEL), BF16),
                        pltpu.SMEM((1,), I32), pltpu.SemaphoreType.DMA((2,))],
    )
    return pl.pallas_call(
        _ffn_kernel,
        grid_spec=grid_spec,
        out_shape=jax.ShapeDtypeStruct((n_tiles * FFN_TILE, D_MODEL), F32),
        compiler_params=_cparams(("arbitrary",)),
        name="ffn",
    )(tile_expert, expert_tiles, expert_end, n_used, xs, wg, bg, wu, bu, wd, bd)


def _combine_kernel(tot_ref, c8_ref, lb_ref, gb_ref, c8n_ref, lbn_ref, gbn_ref,
                    ys_hbm, x2_ref, ids_ref, lrank_ref, prob_ref, lbv_ref, g_ref,
                    op_ref, os_ref, sbuf, sem, *, n_pt, n_t):
    i = pl.program_id(0)
    slot = lax.rem(i, 2)

    def fetch(c8, lb, gb, dst_slot):
        _strip_copies(c8, lb, gb,
                      lambda lr, gr, rows: pltpu.make_async_copy(ys_hbm.at[pl.ds(gr, rows), :],
                                                                 sbuf.at[dst_slot, pl.ds(lr, rows), :],
                                                                 sem.at[dst_slot]))

    @pl.when(i == 0)
    def _():
        sbuf[...] = jnp.zeros_like(sbuf)
        fetch(c8_ref, lb_ref, gb_ref, 0)

    @pl.when(i + 1 < n_t)
    def _():
        fetch(c8n_ref, lbn_ref, gbn_ref, 1 - slot)

    rows = pl.multiple_of(tot_ref[i], STRIP)
    pltpu.make_async_copy(ys_hbm.at[pl.ds(0, rows), :], sbuf.at[slot, pl.ds(0, rows), :], sem.at[slot]).wait()
    strips = sbuf[slot].astype(BF16)
    half = TOK_TILE // COMBINE_PARTS
    outs = []
    for r0 in range(0, TOK_TILE, half):
        rs = slice(r0, r0 + half)
        prob = prob_ref[rs, :]
        selw = _slot_select(ids_ref[rs, :], lrank_ref[rs, :], lbv_ref[0],
                            [prob[:, kq:kq + 1] for kq in range(TOP_K)])
        y = jnp.dot(selw, strips, preferred_element_type=F32)
        outs.append(_rms(x2_ref[rs, :] + y, g_ref[...]))

    @pl.when(i < n_pt)
    def _():
        for j, part in enumerate(outs):
            op_ref[j * half:(j + 1) * half, :] = part

    @pl.when(i >= n_pt)
    def _():
        for j, part in enumerate(outs):
            os_ref[j * half:(j + 1) * half, :] = part


def _combine(tile_tot, c8, lbase, gbase, ys, x2, ids, lrank, prob, g_final, n_pt):
    n_t = c8.shape[0]
    last = n_t - 1
    cur = pl.BlockSpec((1, 1, N_EXPERTS), lambda i, *_: (i, 0, 0), memory_space=pltpu.SMEM)
    nxt = pl.BlockSpec((1, 1, N_EXPERTS), lambda i, *_: (jnp.minimum(i + 1, last), 0, 0), memory_space=pltpu.SMEM)
    row = lambda w: pl.BlockSpec((TOK_TILE, w), lambda i, *_: (i, 0))
    grid_spec = pltpu.PrefetchScalarGridSpec(
        num_scalar_prefetch=1,
        grid=(n_t,),
        in_specs=[cur, cur, cur, nxt, nxt, nxt,
                  pl.BlockSpec(memory_space=pl.ANY),
                  row(D_MODEL), row(TOP_K), row(TOP_K), row(TOP_K),
                  pl.BlockSpec((1, 1, N_EXPERTS), lambda i, *_: (i, 0, 0)),
                  pl.BlockSpec((1, D_MODEL), lambda i, *_: (0, 0))],
        out_specs=[pl.BlockSpec((TOK_TILE, D_MODEL), lambda i, *_: (jnp.minimum(i, n_pt - 1), 0)),
                   pl.BlockSpec((TOK_TILE, D_MODEL), lambda i, *_: (jnp.maximum(i - n_pt, 0), 0))],
        scratch_shapes=[pltpu.VMEM((2, SLOT_ROWS, D_MODEL), F32), pltpu.SemaphoreType.DMA((2,))],
    )
    return pl.pallas_call(
        functools.partial(_combine_kernel, n_pt=n_pt, n_t=n_t),
        grid_spec=grid_spec,
        out_shape=[jax.ShapeDtypeStruct((n_pt * TOK_TILE, D_MODEL), F32),
                   jax.ShapeDtypeStruct(((n_t - n_pt) * TOK_TILE, D_MODEL), F32)],
        compiler_params=_cparams(("arbitrary",)),
        name="combine",
    )(tile_tot, c8, lbase, gbase, c8, lbase, gbase, ys, x2, ids, lrank, prob, lbase.astype(F32), g_final)


def kernel(x_prompt, x_sample, mem_prompt, state_shift, state_wkv, state_conv, cache_mem_k, cache_mem_v,
           g_mix, w_in, mu_shift, w0, w_decay_up, a0, w_iclr_up, w_glora_up, k_k, k_a, r_k, gn_g, gn_b,
           dw_w, dw_b, cln_g, cln_b, w_out, g_xattn, g_mem, w_q, w_mk, w_mv, w_o,
           g_moe, w_router, b_router, w_moe_gate, b_moe_gate, w_moe_up, b_moe_up, w_moe_down, b_moe_down,
           g_final):
    bp, tp, _ = x_prompt.shape
    bs, ts, _ = x_sample.shape
    n_p, n_s = bp * tp, bs * ts
    n_all = n_p + n_s
    assert tp % WKV_BLOCK == 0 and tp % TOK_TILE == 0 and n_s == TOK_TILE
    assert ts <= SAMPLE_PAD and bs % SAMPLE_SEQS == 0 and bs % ATTN_SEQS == 0
    row2 = lambda a: a.reshape(1, -1)

    x_p = x_prompt.reshape(n_p, D_MODEL)
    x_s = x_sample.reshape(n_s, D_MODEL)
    p_rw, u = _inproj(x_p, x_s, row2(g_mix[0]), w_in[0].astype(BF16))

    head_idx = jnp.arange(RW_WIDTH) // RW_HEAD_DIM
    eseg = (head_idx[:, None] == head_idx[None, :]).astype(BF16)
    wts = (row2(mu_shift[0]), row2(w0[0]), w_decay_up[0].astype(BF16), row2(a0[0]),
           w_iclr_up[0].astype(BF16), w_glora_up[0].astype(BF16), row2(k_k[0]), row2(k_a[0]),
           row2(r_k[0]), row2(gn_g[0]), row2(gn_b[0]), eseg)
    conv_w = (dw_w[0], row2(dw_b[0]), row2(cln_g[0]), row2(cln_b[0]))
    n_tp = tp // WKV_BLOCK
    rw_p, wkv_p, cv_p = _mixer(p_rw, u, 0, jnp.zeros((bp, 1, RW_PROJ), F32),
                               jnp.zeros((1, bp, RW_HEADS, RW_HEAD_DIM, RW_HEAD_DIM), F32),
                               jnp.zeros((1, bp, CONV_K - 1, CONV_WIDTH), F32), wts, conv_w,
                               batch=bp, n_seq=1, tt_seq=WKV_BLOCK, chunk=WKV_CHUNK, t_valid=tp, n_t=n_tp,
                               out_rows=n_p)
    pad_seq = lambda a: jnp.pad(a.reshape(bs, ts, -1), ((0, 0), (0, SAMPLE_PAD - ts), (0, 0))).reshape(
        bs * SAMPLE_PAD, -1)
    unpad_seq = lambda a: a.reshape(bs, SAMPLE_PAD, -1)[:, :ts].reshape(n_s, -1)
    p_rw_s = p_rw[n_p:].reshape(bs, ts, RW_PROJ)
    u_s = u[n_p:]
    rw_s_pad, wkv_s, cv_s_pad = _mixer(pad_seq(p_rw_s), pad_seq(u_s), 0, state_shift[0].reshape(bs, 1, RW_PROJ),
                                       state_wkv, state_conv, wts, conv_w,
                                       batch=bs, n_seq=SAMPLE_SEQS, tt_seq=SAMPLE_PAD, chunk=SAMPLE_PAD,
                                       t_valid=ts, n_t=1, out_rows=bs * SAMPLE_PAD)
    rw_s, cv_s = unpad_seq(rw_s_pad), unpad_seq(cv_s_pad)

    x1, q = _mid1(x_p, x_s, rw_p, rw_s, cv_p, cv_s, w_out[0].astype(BF16), row2(g_xattn[0]), w_q[0].astype(BF16))

    mk_p, mv_p = _memkv(mem_prompt.reshape(bp * N_MEM, D_MODEL), row2(g_mem[0]),
                        w_mk[0].astype(BF16), w_mv[0].astype(BF16))
    mk_p = mk_p.reshape(bp, N_MEM, D_MODEL)
    mv_p = mv_p.reshape(bp, N_MEM, D_MODEL)
    o_p = _attn_rows(q, mk_p, mv_p, n_p, tp // TOK_TILE)
    o_s = unpad_seq(_attn_seq(pad_seq(q[n_p:]).reshape(bs, SAMPLE_PAD, D_MODEL), cache_mem_k, cache_mem_v))

    tri = (jnp.arange(TOK_TILE)[None, :] < jnp.arange(TOK_TILE)[:, None]).astype(BF16)
    x2, h3, ids, prob, lrank, cnt3, base3, tot = _mid2(x1, o_p, o_s, w_o[0].astype(BF16), row2(g_moe[0]),
                                                       w_router[0], row2(b_router[0]), tri)

    n_t = n_all // TOK_TILE
    cnt = cnt3.astype(I32)
    c8 = ((cnt + STRIP - 1) // STRIP) * STRIP
    lbase = jnp.cumsum(c8, axis=-1) - c8
    tile_tot = jnp.sum(c8, axis=(1, 2))
    used = tot[0].astype(I32)
    padded = ((used + FFN_TILE - 1) // FFN_TILE) * FFN_TILE
    ends = jnp.cumsum(padded)
    offs = ends - padded
    gbase = offs[None, None, :] + base3.astype(I32)
    tail_n = padded - used
    n_tiles = -(-(n_all * TOP_K + n_t * N_EXPERTS * (STRIP - 1)) // FFN_TILE) + N_EXPERTS
    n_used = (ends[-1] // FFN_TILE).reshape(1)
    tile_ids = jnp.minimum(jnp.arange(n_tiles, dtype=I32), n_used - 1)
    tile_expert = jnp.sum((ends[None, :] // FFN_TILE) <= tile_ids[:, None], axis=1).astype(I32)
    tile_expert = jnp.minimum(tile_expert, N_EXPERTS - 1)

    xs = _dispatch(tile_tot, tail_n, offs + used, n_used, c8, lbase, gbase, h3, ids, lrank, n_tiles)
    bias4 = lambda b: b.reshape(1, N_EXPERTS, 1, D_MODEL)
    ys = _ffn(tile_expert, padded // FFN_TILE, offs + used, n_used, xs, w_moe_gate, bias4(b_moe_gate),
              w_moe_up, bias4(b_moe_up), w_moe_down, bias4(b_moe_down))

    y_p, y_s = _combine(tile_tot, c8, lbase, gbase, ys, x2, ids, lrank, prob, row2(g_final), n_p // TOK_TILE)

    last_rows = lambda a, rows: jnp.stack([a[(b + 1) * tp - rows:(b + 1) * tp] for b in range(bp)])
    new_conv_s = jnp.concatenate([state_conv[0], u_s.reshape(bs, ts, CONV_WIDTH)], axis=1)[:, -(CONV_K - 1):]
    kv_shape = (1, bp, N_MEM, X_HEADS, X_HEAD_DIM)
    return (y_p.reshape(bp, tp, D_MODEL), y_s.reshape(bs, ts, D_MODEL),
            last_rows(p_rw, 1).reshape(1, bp, RW_PROJ), wkv_p[None],
            last_rows(u, CONV_K - 1)[None],
            mk_p.reshape(kv_shape), mv_p.reshape(kv_shape),
            p_rw_s[:, -1][None], wkv_s[None], new_conv_s[None])
```

```python
import functools

import jax
import jax.numpy as jnp
from jax import lax
from jax.experimental import pallas as pl
from jax.experimental.pallas import tpu as pltpu

F32 = jnp.float32
BF16 = jnp.bfloat16
I32 = jnp.int32

D_MODEL = 1024
RW_HEADS = 8
RW_HEAD_DIM = 64
RW_WIDTH = RW_HEADS * RW_HEAD_DIM
DECAY_LORA = 64
ICLR_LORA = 64
GATE_LORA = 128
RW_PROJ = 3 * RW_WIDTH + DECAY_LORA + ICLR_LORA + GATE_LORA
GN_EPS = 64e-5
CONV_WIDTH = D_MODEL - RW_WIDTH
CONV_K = 31
LN_EPS = 1e-5
IN_PROJ = RW_PROJ + 2 * CONV_WIDTH
N_MEM = 256
X_HEADS = 4
X_HEAD_DIM = D_MODEL // X_HEADS
ATTN_SCALE = X_HEAD_DIM ** -0.5
N_EXPERTS = 32
TOP_K = 4
SWIGLU_LIMIT = 7.0
SWIGLU_ALPHA = 1.702
NORM_EPS = 1e-5

TOK_TILE = 512
WKV_BLOCK = 256
WKV_CHUNK = 64
SAMPLE_PAD = 8
SAMPLE_SEQS = 8
ATTN_SEQS = 4
GROUP = 4
GW = GROUP * RW_HEAD_DIM
N_GROUPS = RW_HEADS // GROUP
CHAIN_BATCH = 16
FFN_TILE = 512
STRIP = 8
SLOT_ROWS = TOP_K * TOK_TILE + N_EXPERTS * STRIP
COMBINE_PARTS = 4
CONV_PAD = 32
VMEM_LIMIT = 56 * 1024 * 1024


def _cparams(sem):
    return pltpu.CompilerParams(dimension_semantics=sem, vmem_limit_bytes=VMEM_LIMIT)


def _dot(a, b):
    return jnp.dot(a.astype(BF16), b.astype(BF16), preferred_element_type=F32)


def _dot_nt(a, b):
    return lax.dot_general(a.astype(BF16), b.astype(BF16), (((1,), (1,)), ((), ())),
                           preferred_element_type=F32)


def _dot_tn(a, b):
    return lax.dot_general(a.astype(BF16), b.astype(BF16), (((0,), (0,)), ((), ())),
                           preferred_element_type=F32)


def _split2(x):
    hi = x.astype(BF16)
    lo = (x - hi.astype(F32)).astype(BF16)
    return hi, lo


def _split3(x):
    hi = x.astype(BF16)
    r1 = x - hi.astype(F32)
    mid = r1.astype(BF16)
    lo = (r1 - mid.astype(F32)).astype(BF16)
    return hi, mid, lo


def _segsum(x, eseg):
    hi, lo = _split2(x)
    return (jnp.dot(hi, eseg, preferred_element_type=F32)
            + jnp.dot(lo, eseg, preferred_element_type=F32))


def _rms(x, g):
    return x * lax.rsqrt(jnp.mean(x * x, axis=-1, keepdims=True) + NORM_EPS) * g


def _sigmoid(x):
    return 1.0 / (1.0 + jnp.exp(-x))


def _pick(first, a_ref, b_ref):
    return jnp.where(first, a_ref[...], b_ref[...])


def _two_source_specs(width, n_pt):
    return (pl.BlockSpec((TOK_TILE, width), lambda i: (jnp.minimum(i, n_pt - 1), 0)),
            pl.BlockSpec((TOK_TILE, width), lambda i: (jnp.maximum(i - n_pt, 0), 0)))


def _inproj_kernel(xp_ref, xs_ref, g_ref, w_ref, prw_ref, u_ref, *, n_pt):
    h = _rms(_pick(pl.program_id(0) < n_pt, xp_ref, xs_ref), g_ref[...])
    p = _dot(h, w_ref[...])
    prw_ref[...] = p[:, :RW_PROJ]
    u_ref[...] = p[:, RW_PROJ:RW_PROJ + CONV_WIDTH] * _sigmoid(p[:, RW_PROJ + CONV_WIDTH:])


def _inproj(x_p, x_s, g_mix, w_in_bf):
    n = x_p.shape[0] + x_s.shape[0]
    n_pt = x_p.shape[0] // TOK_TILE
    return pl.pallas_call(
        functools.partial(_inproj_kernel, n_pt=n_pt),
        grid=(n // TOK_TILE,),
        in_specs=[*_two_source_specs(D_MODEL, n_pt),
                  pl.BlockSpec((1, D_MODEL), lambda i: (0, 0)),
                  pl.BlockSpec((D_MODEL, IN_PROJ), lambda i: (0, 0))],
        out_specs=[pl.BlockSpec((TOK_TILE, RW_PROJ), lambda i: (i, 0)),
                   pl.BlockSpec((TOK_TILE, CONV_WIDTH), lambda i: (i, 0))],
        out_shape=[jax.ShapeDtypeStruct((n, RW_PROJ), F32),
                   jax.ShapeDtypeStruct((n, CONV_WIDTH), F32)],
        compiler_params=_cparams(("parallel",)),
        name="inproj",
    )(x_p, x_s, g_mix, w_in_bf)


def _bd(x, w, masks):
    r, lanes = x.shape
    key = (r, lanes, w)
    if key not in masks:
        head = lax.broadcasted_iota(I32, (r, lanes), 1) >> (w.bit_length() - 1)
        masks[key] = [head == h for h in range(GROUP)]
    return jnp.concatenate([jnp.where(m, x, 0.0) for m in masks[key]], axis=0).astype(BF16)


def _collapse(x):
    head = lax.broadcasted_iota(I32, (RW_HEAD_DIM, GW), 1) >> (RW_HEAD_DIM.bit_length() - 1)
    out = jnp.zeros((RW_HEAD_DIM, GW), F32)
    for h in range(GROUP):
        out = out + jnp.where(head == h, x[h * RW_HEAD_DIM:(h + 1) * RW_HEAD_DIM, :], 0.0)
    return out


def _conv_module(t, u_ref, hist_ref, cw_ref, cb_ref, cg_ref, cbeta_ref, cv_ref, ext_ref, sh_ref,
                 *, n_seq, tt_seq, n_t):
    first = CONV_PAD - (CONV_K - 1)
    span = tt_seq + CONV_PAD - 8
    for s in range(n_seq):
        rows = slice(s * tt_seq, (s + 1) * tt_seq)

        @pl.when(t == 0)
        def _():
            ext_ref[s, 0:8, :] = jnp.zeros((8, CONV_WIDTH), F32)
            ext_ref[s, first:CONV_PAD, :] = hist_ref[0, s]

        ext_ref[s, CONV_PAD:CONV_PAD + tt_seq, :] = u_ref[rows, :]
        for b in range(1, 8):
            sh_ref[b, 0:span, :] = ext_ref[s, pl.ds(b, span), :]
        acc = jnp.zeros((tt_seq, CONV_WIDTH), F32) + cb_ref[...]
        for j in range(CONV_K):
            a8, b = divmod(first + j, 8)
            win = ext_ref[s, pl.ds(8 * a8, tt_seq), :] if b == 0 else sh_ref[b, pl.ds(8 * a8, tt_seq), :]
            acc = acc + win * cw_ref[j:j + 1, :]
        mean = jnp.mean(acc, axis=-1, keepdims=True)
        cen = acc - mean
        var = jnp.mean(cen * cen, axis=-1, keepdims=True)
        cn = cen * lax.rsqrt(var + LN_EPS) * cg_ref[...] + cbeta_ref[...]
        cv_ref[rows, :] = cn * _sigmoid(cn)
        if n_t > 1:
            ext_ref[s, 0:CONV_PAD, :] = ext_ref[s, tt_seq:tt_seq + CONV_PAD, :]


def _wkv_kernel(p_ref, prev_ref, s0_ref, mu_ref, w0_ref, wdup_ref, a0_ref, aup_ref, gup_ref,
                kk_ref, ka_ref, rk_ref, gng_ref, gnb_ref, eseg_ref,
                u_ref, hist_ref, cw_ref, cb_ref, cg_ref, cbeta_ref,
                out_ref, sfin_ref, cv_ref,
                s_ref, last_ref, rt_ref, at_ref, bt_ref, kt_ref, v_ref, pc_ref, y_ref,
                r_ref, km_ref, g_ref, ext_ref, sh_ref, *, n_seq, tt_seq, chunk, t_valid, n_t):
    t = pl.program_id(1)
    tt = n_seq * tt_seq
    n_chunk = tt_seq // chunk
    n_fused = chunk.bit_length() - 2
    nd = RW_HEAD_DIM
    c1, c2, c3 = RW_WIDTH, 2 * RW_WIDTH, 3 * RW_WIDTH
    c4 = c3 + DECAY_LORA
    c5 = c4 + ICLR_LORA

    @pl.when(t == 0)
    def _():
        for s in range(n_seq):
            for h in range(RW_HEADS):
                g, hg = divmod(h, GROUP)
                s_ref[s, g, :, hg * nd:(hg + 1) * nd] = s0_ref[0, s, h]
        if n_seq == 1:
            last_ref[0:1, :] = prev_ref[0]

    _conv_module(t, u_ref, hist_ref, cw_ref, cb_ref, cg_ref, cbeta_ref, cv_ref, ext_ref, sh_ref,
                 n_seq=n_seq, tt_seq=tt_seq, n_t=n_t)

    p = p_ref[...]
    row = lax.broadcasted_iota(I32, (tt, 1), 0)
    seq_row = row & (tt_seq - 1)
    rolled = pltpu.roll(p, shift=1, axis=0)
    if n_seq == 1:
        ps = jnp.where(seq_row == 0, last_ref[0:1, :], rolled)
        last_ref[0:1, :] = p[tt - 1:tt, :]
    else:
        prev_rows = jnp.concatenate(
            [jnp.broadcast_to(prev_ref[q], (tt_seq, RW_PROJ)) for q in range(n_seq)], axis=0)
        ps = jnp.where(seq_row == 0, prev_rows, rolled)
    z = p + mu_ref[...] * (ps - p)
    r, k, v = z[:, :c1], z[:, c1:c2], z[:, c2:c3]
    wd, ad, gd = z[:, c3:c4], z[:, c4:c5], z[:, c5:]
    wpre = w0_ref[...] + _dot(jnp.tanh(wd), wdup_ref[...])
    neg = -wpre
    w = -(jnp.maximum(neg, 0.0) + jnp.log(1.0 + jnp.exp(-jnp.abs(neg)))) - 0.5
    a = _sigmoid(a0_ref[...] + _dot(ad, aup_ref[...]))
    g_ref[...] = _dot(_sigmoid(gd), gup_ref[...])
    eseg = eseg_ref[...]
    kk = k * kk_ref[...]
    kk = kk / jnp.maximum(jnp.sqrt(_segsum(kk * kk, eseg)), 1e-12)
    kmod = k * (1.0 + (a - 1.0) * ka_ref[...])
    logd = -jnp.exp(w)
    if t_valid < tt_seq * n_t:
        valid = (t * tt_seq + seq_row) < t_valid
        logd = jnp.where(valid, logd, 0.0)
        kk = jnp.where(valid, kk, 0.0)
        kmod = jnp.where(valid, kmod, 0.0)
        v = jnp.where(valid, v, 0.0)
    ri = lax.broadcasted_iota(I32, (tt, tt), 0)
    ci = lax.broadcasted_iota(I32, (tt, tt), 1)
    shift = chunk.bit_length() - 1
    ltri = jnp.where(ci <= ri, jnp.where((ri >> shift) == (ci >> shift), 1.0, 0.0), 0.0).astype(BF16)
    hi, mid, lo = _split3(logd)
    logp = (jnp.dot(ltri, hi, preferred_element_type=F32)
            + jnp.dot(ltri, mid, preferred_element_type=F32)
            + jnp.dot(ltri, lo, preferred_element_type=F32))
    pcum = jnp.exp(logp)
    pinv = jnp.exp(-logp)
    r_ref[...] = r
    km_ref[...] = kmod
    v_ref[...] = v
    pc_ref[...] = pcum
    rt_ref[...] = r * pcum
    at_ref[...] = -kk * jnp.exp(logp - logd)
    bt_ref[...] = kk * a * pinv
    kt_ref[...] = kmod * pinv

    cw = GROUP * chunk
    ti = lax.broadcasted_iota(I32, (chunk, cw), 0)
    si = lax.broadcasted_iota(I32, (chunk, cw), 1) & (chunk - 1)
    strict = si < ti
    incl = si <= ti
    eye_c = jnp.where(si == ti, 1.0, 0.0)
    ji = lax.broadcasted_iota(I32, (nd, GW), 0)
    jj = lax.broadcasted_iota(I32, (nd, GW), 1) & (nd - 1)
    eye_n = jnp.where(ji == jj, 1.0, 0.0)

    all_chains = [(s, c, g) for s in range(n_seq) for c in range(n_chunk) for g in range(N_GROUPS)]
    masks = {}

    def blk(ref, s, c, g):
        r0 = s * tt_seq + c * chunk
        return ref[r0:r0 + chunk, g * GW:(g + 1) * GW]

    rp, yv, mm, gm = {}, {}, {}, {}

    def chunk_operators(chains):
        am = {q: blk(at_ref, *q) for q in chains}
        rm = {q: blk(rt_ref, *q) for q in chains}
        bm = {q: blk(bt_ref, *q) for q in chains}
        km = {q: blk(kt_ref, *q) for q in chains}
        vm = {q: blk(v_ref, *q) for q in chains}
        a_ab, a_ak, a_rb, a_rk = {}, {}, {}, {}
        for q in chains:
            ar = jnp.concatenate([am[q], rm[q]], axis=0)
            s1 = _dot_nt(ar, _bd(bm[q], nd, masks))
            s2 = _dot_nt(ar, _bd(km[q], nd, masks))
            a_ab[q] = jnp.where(strict, s1[:chunk], 0.0)
            a_rb[q] = jnp.where(incl, s1[chunk:], 0.0)
            a_ak[q] = jnp.where(strict, s2[:chunk], 0.0)
            a_rk[q] = jnp.where(incl, s2[chunk:], 0.0)
        tinv = {q: eye_c + a_ab[q] for q in chains}
        pw = {q: _dot(a_ab[q], _bd(a_ab[q], chunk, masks)) for q in chains}
        for lvl in range(n_fused):
            last = lvl == n_fused - 1
            for q in chains:
                lhs = tinv[q] if last else jnp.concatenate([tinv[q], pw[q]], axis=0)
                zz = _dot(lhs, _bd(pw[q], chunk, masks))
                tinv[q] = tinv[q] + zz[:chunk]
                if not last:
                    pw[q] = zz[chunk:]
        akv, rkv = {}, {}
        for q in chains:
            zv = _dot(jnp.concatenate([a_ak[q], a_rk[q]], axis=0), _bd(vm[q], nd, masks))
            akv[q], rkv[q] = zv[:chunk], zv[chunk:]
        ap = {q: _dot(tinv[q], _bd(am[q], nd, masks)) for q in chains}
        uv = {q: _dot(tinv[q], _bd(akv[q], nd, masks)) for q in chains}
        for q in chains:
            rp[q] = rm[q] + _dot(a_rb[q], _bd(ap[q], nd, masks))
            yv[q] = _dot(a_rb[q], _bd(uv[q], nd, masks)) + rkv[q]
        for q in chains:
            s, c, g = q
            r_end = s * tt_seq + (c + 1) * chunk
            pc = pc_ref[r_end - 8:r_end, g * GW:(g + 1) * GW][7:8, :]
            cross = _dot_tn(jnp.concatenate([ap[q], uv[q]], axis=1), bm[q])
            vk = _dot_tn(vm[q], km[q])
            mm[q] = (eye_n + _collapse(cross[:GW])) * pc
            gm[q] = (_collapse(cross[GW:]) + _collapse(vk)) * pc

    for i0 in range(0, len(all_chains), CHAIN_BATCH):
        chunk_operators(all_chains[i0:i0 + CHAIN_BATCH])

    for s in range(n_seq):
        for g in range(N_GROUPS):
            st = s_ref[s, g]
            for c in range(n_chunk):
                q = (s, c, g)
                r0 = s * tt_seq + c * chunk
                y_ref[r0:r0 + chunk, g * GW:(g + 1) * GW] = _dot_nt(rp[q], _bd(st, nd, masks)) + yv[q]
                st = _dot(st, _bd(mm[q], nd, masks)) + gm[q]
            s_ref[s, g] = st

    y = y_ref[...]
    inv_n = 1.0 / RW_HEAD_DIM
    mean = _segsum(y, eseg) * inv_n
    yc = y - mean
    var = _segsum(yc * yc, eseg) * inv_n
    yn = yc * lax.rsqrt(var + GN_EPS) * gng_ref[...] + gnb_ref[...]
    bonus = _segsum(r_ref[...] * km_ref[...] * rk_ref[...], eseg) * v_ref[...]
    out_ref[...] = (yn + bonus) * g_ref[...]

    @pl.when(t == n_t - 1)
    def _():
        for s in range(n_seq):
            for h in range(RW_HEADS):
                g, hg = divmod(h, GROUP)
                sfin_ref[s, h] = s_ref[s, g, :, hg * nd:(hg + 1) * nd]


def _mixer(p_rows, u_rows, row_block0, prev, s0, hist, wts, conv_w, *, batch, n_seq, tt_seq, chunk, t_valid,
           n_t, out_rows):
    (mu, w0, wdup, a0, aup, gup, k_k, k_a, r_k, gn_g, gn_b, eseg) = wts
    dw_w, dw_b, ln_g, ln_b = conv_w
    tt = n_seq * tt_seq
    kern = functools.partial(_wkv_kernel, n_seq=n_seq, tt_seq=tt_seq, chunk=chunk, t_valid=t_valid, n_t=n_t)
    const = lambda shape: pl.BlockSpec(shape, lambda b, t: tuple(0 for _ in shape))
    row_out = lambda w: pl.BlockSpec((tt, w), lambda b, t: (b * n_t + t, 0))
    blk = pltpu.VMEM((tt, RW_WIDTH), F32)
    st_spec = pl.BlockSpec((n_seq, RW_HEADS, RW_HEAD_DIM, RW_HEAD_DIM), lambda b, t: (b, 0, 0, 0))
    return pl.pallas_call(
        kern,
        grid=(batch // n_seq, n_t),
        in_specs=[pl.BlockSpec((tt, RW_PROJ), lambda b, t: (row_block0 + b * n_t + t, 0)),
                  pl.BlockSpec((n_seq, 1, RW_PROJ), lambda b, t: (b, 0, 0)),
                  pl.BlockSpec((1, n_seq, RW_HEADS, RW_HEAD_DIM, RW_HEAD_DIM), lambda b, t: (0, b, 0, 0, 0)),
                  const((1, RW_PROJ)), const((1, RW_WIDTH)), const((DECAY_LORA, RW_WIDTH)),
                  const((1, RW_WIDTH)), const((ICLR_LORA, RW_WIDTH)), const((GATE_LORA, RW_WIDTH)),
                  const((1, RW_WIDTH)), const((1, RW_WIDTH)), const((1, RW_WIDTH)),
                  const((1, RW_WIDTH)), const((1, RW_WIDTH)), const((RW_WIDTH, RW_WIDTH)),
                  pl.BlockSpec((tt, CONV_WIDTH), lambda b, t: (row_block0 + b * n_t + t, 0)),
                  pl.BlockSpec((1, n_seq, CONV_K - 1, CONV_WIDTH), lambda b, t: (0, b, 0, 0)),
                  const((CONV_K, CONV_WIDTH)), const((1, CONV_WIDTH)),
                  const((1, CONV_WIDTH)), const((1, CONV_WIDTH))],
        out_specs=[row_out(RW_WIDTH), st_spec, row_out(CONV_WIDTH)],
        out_shape=[jax.ShapeDtypeStruct((out_rows, RW_WIDTH), F32),
                   jax.ShapeDtypeStruct((batch, RW_HEADS, RW_HEAD_DIM, RW_HEAD_DIM), F32),
                   jax.ShapeDtypeStruct((out_rows, CONV_WIDTH), F32)],
        scratch_shapes=[pltpu.VMEM((n_seq, N_GROUPS, RW_HEAD_DIM, GW), F32),
                        pltpu.VMEM((8, RW_PROJ), F32),
                        blk, blk, blk, blk, blk, blk, blk, blk, blk, blk,
                        pltpu.VMEM((n_seq, CONV_PAD + tt_seq, CONV_WIDTH), F32),
                        pltpu.VMEM((8, CONV_PAD + tt_seq, CONV_WIDTH), F32)],
        compiler_params=_cparams(("parallel", "arbitrary")),
        name="mixer",
    )(p_rows, prev, s0, mu, w0, wdup, a0, aup, gup, k_k, k_a, r_k, gn_g, gn_b, eseg,
      u_rows, hist, dw_w, dw_b, ln_g, ln_b)


def _mid1_kernel(xp_ref, xs_ref, rwp_ref, rws_ref, cvp_ref, cvs_ref, wo_ref, g_ref, wq_ref, x1_ref, q_ref,
                 *, n_pt):
    first = pl.program_id(0) < n_pt
    rw = _pick(first, rwp_ref, rws_ref)
    cv = _pick(first, cvp_ref, cvs_ref)
    mix = _dot(rw, wo_ref[:RW_WIDTH, :]) + _dot(cv, wo_ref[RW_WIDTH:, :])
    x1 = _pick(first, xp_ref, xs_ref) + mix
    x1_ref[...] = x1
    q = _dot(_rms(x1, g_ref[...]), wq_ref[...])
    q_ref[...] = (q * ATTN_SCALE).astype(BF16)


def _mid1(x_p, x_s, rw_p, rw_s, cv_p, cv_s, w_out_bf, g_x, w_q_bf):
    n = x_p.shape[0] + x_s.shape[0]
    n_pt = rw_p.shape[0] // TOK_TILE
    row = lambda w: pl.BlockSpec((TOK_TILE, w), lambda i: (i, 0))
    full = lambda a, b: pl.BlockSpec((a, b), lambda i: (0, 0))
    return pl.pallas_call(
        functools.partial(_mid1_kernel, n_pt=n_pt),
        grid=(n // TOK_TILE,),
        in_specs=[*_two_source_specs(D_MODEL, n_pt), *_two_source_specs(RW_WIDTH, n_pt),
                  *_two_source_specs(CONV_WIDTH, n_pt),
                  full(D_MODEL, D_MODEL), full(1, D_MODEL), full(D_MODEL, D_MODEL)],
        out_specs=[row(D_MODEL), row(D_MODEL)],
        out_shape=[jax.ShapeDtypeStruct((n, D_MODEL), F32), jax.ShapeDtypeStruct((n, D_MODEL), BF16)],
        compiler_params=_cparams(("parallel",)),
        name="mid1",
    )(x_p, x_s, rw_p, rw_s, cv_p, cv_s, w_out_bf, g_x, w_q_bf)


def _memkv_kernel(m_ref, g_ref, wk_ref, wv_ref, k_ref, v_ref):
    m = _rms(m_ref[...], g_ref[...])
    k_ref[...] = _dot(m, wk_ref[...])
    v_ref[...] = _dot(m, wv_ref[...])


def _memkv(mem, g_mem, w_mk_bf, w_mv_bf):
    n = mem.shape[0]
    tile = N_MEM
    row = pl.BlockSpec((tile, D_MODEL), lambda i: (i, 0))
    full = lambda a, b: pl.BlockSpec((a, b), lambda i: (0, 0))
    return pl.pallas_call(
        _memkv_kernel,
        grid=(n // tile,),
        in_specs=[row, full(1, D_MODEL), full(D_MODEL, D_MODEL), full(D_MODEL, D_MODEL)],
        out_specs=[row, row],
        out_shape=[jax.ShapeDtypeStruct((n, D_MODEL), F32), jax.ShapeDtypeStruct((n, D_MODEL), F32)],
        compiler_params=_cparams(("parallel",)),
        name="memkv",
    )(mem, g_mem, w_mk_bf, w_mv_bf)


def _attn_head(q_h, k_h, v_h):
    s = _dot_nt(q_h, k_h)
    e = jnp.exp(s - jnp.max(s, axis=-1, keepdims=True))
    pr = e / jnp.sum(e, axis=-1, keepdims=True)
    return _dot(pr, v_h).astype(BF16)


def _attn_rows_kernel(q_ref, k_ref, v_ref, o_ref):
    q = q_ref[...]
    for h in range(X_HEADS):
        hs = slice(h * X_HEAD_DIM, (h + 1) * X_HEAD_DIM)
        o_ref[:, hs] = _attn_head(q[:, hs], k_ref[0, :, hs], v_ref[0, :, hs])


def _attn_seq_kernel(q_ref, k_hbm, v_hbm, o_ref, kbuf, vbuf, sem, *, n_steps):
    i = pl.program_id(0)
    slot = lax.rem(i, 2)

    def copies(step, s):
        out = []
        for g in range(ATTN_SEQS):
            seq = step * ATTN_SEQS + g
            for h in range(X_HEADS):
                lanes = pl.ds(h * X_HEAD_DIM, X_HEAD_DIM)
                out.append(pltpu.make_async_copy(k_hbm.at[0, seq, :, h, :], kbuf.at[s, g, :, lanes], sem.at[s]))
                out.append(pltpu.make_async_copy(v_hbm.at[0, seq, :, h, :], vbuf.at[s, g, :, lanes], sem.at[s]))
        return out

    def start_all(cps):
        for j, cp in enumerate(cps):
            cp.start(priority=j % 2)

    @pl.when(i == 0)
    def _():
        start_all(copies(0, 0))

    @pl.when(i + 1 < n_steps)
    def _():
        start_all(copies(i + 1, 1 - slot))

    for cp in copies(i, slot):
        cp.wait()
    head_of_lane = lax.broadcasted_iota(I32, (SAMPLE_PAD * X_HEADS, D_MODEL), 1) >> (X_HEAD_DIM.bit_length() - 1)
    head_of_row = lax.broadcasted_iota(I32, (SAMPLE_PAD * X_HEADS, D_MODEL), 0) >> (SAMPLE_PAD.bit_length() - 1)
    own = head_of_lane == head_of_row
    for g in range(ATTN_SEQS):
        q8 = q_ref[g].astype(F32)
        qbd = jnp.where(own, jnp.concatenate([q8] * X_HEADS, axis=0), 0.0)
        s = _dot_nt(qbd, kbuf[slot, g])
        e = jnp.exp(s - jnp.max(s, axis=-1, keepdims=True))
        pr = e / jnp.sum(e, axis=-1, keepdims=True)
        o_all = jnp.where(own, _dot(pr, vbuf[slot, g]), 0.0)
        o8 = o_all[0:SAMPLE_PAD]
        for h in range(1, X_HEADS):
            o8 = o8 + o_all[SAMPLE_PAD * h:SAMPLE_PAD * (h + 1)]
        o_ref[g] = o8.astype(BF16)


def _attn_rows(q, mk, mv, n_rows, tiles_per_seq):
    qspec = pl.BlockSpec((TOK_TILE, D_MODEL), lambda i, j: (i * tiles_per_seq + j, 0))
    kvspec = pl.BlockSpec((1, N_MEM, D_MODEL), lambda i, j: (i, 0, 0))
    return pl.pallas_call(
        _attn_rows_kernel,
        grid=(mk.shape[0], tiles_per_seq),
        in_specs=[qspec, kvspec, kvspec],
        out_specs=qspec,
        out_shape=jax.ShapeDtypeStruct((n_rows, D_MODEL), BF16),
        compiler_params=_cparams(("parallel", "parallel")),
        name="attn",
    )(q, mk, mv)


def _attn_seq(q3, mk, mv):
    b, t, _ = q3.shape
    qspec = pl.BlockSpec((ATTN_SEQS, t, D_MODEL), lambda i: (i, 0, 0))
    anyspec = pl.BlockSpec(memory_space=pl.ANY)
    head_buf = pltpu.VMEM((2, ATTN_SEQS, N_MEM, D_MODEL), F32)
    return pl.pallas_call(
        functools.partial(_attn_seq_kernel, n_steps=b // ATTN_SEQS),
        grid=(b // ATTN_SEQS,),
        in_specs=[qspec, anyspec, anyspec],
        out_specs=qspec,
        out_shape=jax.ShapeDtypeStruct((b, t, D_MODEL), BF16),
        scratch_shapes=[head_buf, head_buf, pltpu.SemaphoreType.DMA((2,))],
        compiler_params=_cparams(("arbitrary",)),
        name="attn",
    )(q3, mk, mv)


def _mid2_kernel(x1_ref, op_ref, os_ref, wo_ref, g_ref, wr_ref, br_ref, tri_ref,
                 x2_ref, h3_ref, ids_ref, prob_ref, lrank_ref, cnt_ref, base_ref, tot_ref, carry_ref, *, n_pt):
    i = pl.program_id(0)

    @pl.when(i == 0)
    def _():
        carry_ref[...] = jnp.zeros_like(carry_ref)

    x2 = x1_ref[...] + _dot(_pick(i < n_pt, op_ref, os_ref), wo_ref[...])
    x2_ref[...] = x2
    h3 = _rms(x2, g_ref[...])
    h3_ref[...] = h3.astype(BF16)
    h3_hi, h3_lo = _split2(h3)
    wr_hi, wr_lo = _split2(wr_ref[...])
    logits = (jnp.dot(h3_hi, wr_hi, preferred_element_type=F32)
              + jnp.dot(h3_hi, wr_lo, preferred_element_type=F32)
              + jnp.dot(h3_lo, wr_hi, preferred_element_type=F32)) + br_ref[...]
    n = logits.shape[0]
    lane = lax.broadcasted_iota(I32, (n, N_EXPERTS), 1)
    work = logits
    vals, ids = [], []
    for _ in range(TOP_K):
        m = jnp.max(work, axis=-1, keepdims=True)
        idx = jnp.min(jnp.where(work == m, lane, N_EXPERTS), axis=-1, keepdims=True)
        vals.append(m)
        ids.append(idx)
        work = jnp.where(lane == idx, -jnp.inf, work)
    exps = [jnp.exp(vk - vals[0]) for vk in vals]
    den = exps[0] + exps[1] + exps[2] + exps[3]
    mask = jnp.zeros((n, N_EXPERTS), F32)
    for idx in ids:
        mask = mask + jnp.where(lane == idx, 1.0, 0.0)
    lrank = jnp.dot(tri_ref[...], mask.astype(BF16), preferred_element_type=F32)
    cnt = jnp.sum(mask, axis=0, keepdims=True)
    cnt_ref[0] = cnt
    base_ref[0] = carry_ref[0:1, :]
    carry_ref[0:1, :] = carry_ref[0:1, :] + jnp.floor((cnt + (STRIP - 1)) * (1.0 / STRIP)) * STRIP
    tot_ref[...] = carry_ref[0:1, :]
    for kq in range(TOP_K):
        ids_ref[:, kq:kq + 1] = ids[kq]
        prob_ref[:, kq:kq + 1] = exps[kq] / den
        lrank_ref[:, kq:kq + 1] = jnp.sum(jnp.where(lane == ids[kq], lrank, 0.0), axis=-1, keepdims=True)


def _mid2(x1, o_p, o_s, w_o_bf, g_moe, w_router, b_router, tri):
    n = x1.shape[0]
    n_t = n // TOK_TILE
    n_pt = o_p.shape[0] // TOK_TILE
    row = lambda w: pl.BlockSpec((TOK_TILE, w), lambda i: (i, 0))
    full = lambda a, b: pl.BlockSpec((a, b), lambda i: (0, 0))
    per_tile = pl.BlockSpec((1, 1, N_EXPERTS), lambda i: (i, 0, 0))
    return pl.pallas_call(
        functools.partial(_mid2_kernel, n_pt=n_pt),
        grid=(n_t,),
        in_specs=[row(D_MODEL), *_two_source_specs(D_MODEL, n_pt), full(D_MODEL, D_MODEL), full(1, D_MODEL),
                  full(D_MODEL, N_EXPERTS), full(1, N_EXPERTS), full(TOK_TILE, TOK_TILE)],
        out_specs=[row(D_MODEL), row(D_MODEL), row(TOP_K), row(TOP_K), row(TOP_K), per_tile, per_tile,
                   full(1, N_EXPERTS)],
        out_shape=[jax.ShapeDtypeStruct((n, D_MODEL), F32), jax.ShapeDtypeStruct((n, D_MODEL), BF16),
                   jax.ShapeDtypeStruct((n, TOP_K), I32), jax.ShapeDtypeStruct((n, TOP_K), F32),
                   jax.ShapeDtypeStruct((n, TOP_K), F32),
                   jax.ShapeDtypeStruct((n_t, 1, N_EXPERTS), F32), jax.ShapeDtypeStruct((n_t, 1, N_EXPERTS), F32),
                   jax.ShapeDtypeStruct((1, N_EXPERTS), F32)],
        scratch_shapes=[pltpu.VMEM((8, N_EXPERTS), F32)],
        compiler_params=_cparams(("arbitrary",)),
        name="mid2",
    )(x1, o_p, o_s, w_o_bf, g_moe, w_router, b_router, tri)


def _slot_select(ids_ref, lrank_ref, lbase_v, weights):
    n = ids_ref.shape[0]
    lane = lax.broadcasted_iota(I32, (n, N_EXPERTS), 1)
    col = lax.broadcasted_iota(I32, (n, SLOT_ROWS), 1)
    sel = jnp.zeros((n, SLOT_ROWS), F32)
    for kq in range(TOP_K):
        base = jnp.sum(jnp.where(lane == ids_ref[:, kq:kq + 1], lbase_v, 0.0), axis=-1, keepdims=True)
        dest = (base + lrank_ref[:, kq:kq + 1]).astype(I32)
        sel = jnp.where(col == dest, weights[kq], sel)
    return sel.astype(BF16)


def _strip_copies(c8_ref, lb_ref, gb_ref, make):
    for e in range(N_EXPERTS):
        rows = c8_ref[0, 0, e]

        @pl.when(rows > 0)
        def _():
            make(pl.multiple_of(lb_ref[0, 0, e], STRIP), pl.multiple_of(gb_ref[0, 0, e], STRIP),
                 pl.multiple_of(rows, STRIP)).start(priority=e % 2)


def _dispatch_kernel(tot_ref, tailn_ref, tailo_ref, nt_ref,
                     c8_ref, lb_ref, gb_ref, h3_ref, ids_ref, lrank_ref, lbv_ref,
                     xs_hbm, cbuf, zbuf, sem, *, n_t, n_tiles):
    i = pl.program_id(0)
    slot = lax.rem(i, 2)
    sel = _slot_select(ids_ref, lrank_ref, lbv_ref[0], [1.0] * TOP_K)
    cbuf[slot] = _dot_tn(sel, h3_ref[...])
    _strip_copies(c8_ref, lb_ref, gb_ref,
                  lambda lr, gr, rows: pltpu.make_async_copy(cbuf.at[slot, pl.ds(lr, rows), :],
                                                             xs_hbm.at[pl.ds(gr, rows), :], sem.at[slot]))

    def wait_rows(s, rows):
        rows = pl.multiple_of(rows, STRIP)
        pltpu.make_async_copy(cbuf.at[s, pl.ds(0, rows), :], xs_hbm.at[pl.ds(0, rows), :], sem.at[s]).wait()

    @pl.when(i > 0)
    def _():
        wait_rows(1 - slot, tot_ref[jnp.maximum(i - 1, 0)])

    @pl.when(i == n_t - 1)
    def _():
        wait_rows(slot, tot_ref[i])
        zbuf[...] = jnp.zeros_like(zbuf)
        for e in range(N_EXPERTS):
            rows = pl.multiple_of(tailn_ref[e], STRIP)

            @pl.when(rows > 0)
            def _():
                pltpu.make_async_copy(zbuf.at[pl.ds(0, rows), :],
                                      xs_hbm.at[pl.ds(pl.multiple_of(tailo_ref[e], STRIP), rows), :],
                                      sem.at[2]).start()

        for e in range(N_EXPERTS):
            rows = pl.multiple_of(tailn_ref[e], STRIP)

            @pl.when(rows > 0)
            def _():
                pltpu.make_async_copy(zbuf.at[pl.ds(0, rows), :], xs_hbm.at[pl.ds(0, rows), :], sem.at[2]).wait()

        def tile_copy(j):
            return pltpu.make_async_copy(
                zbuf, xs_hbm.at[pl.ds(pl.multiple_of(j * FFN_TILE, FFN_TILE), FFN_TILE), :], sem.at[2])

        def start_tile(j, c):
            tile_copy(j).start()
            return c

        def wait_tile(j, c):
            tile_copy(j).wait()
            return c

        lax.fori_loop(nt_ref[0], n_tiles, start_tile, 0)
        lax.fori_loop(nt_ref[0], n_tiles, wait_tile, 0)


def _dispatch(tile_tot, tail_n, tail_off, n_used, c8, lbase, gbase, h3, ids, lrank, n_tiles):
    n_t = c8.shape[0]
    smem = pl.BlockSpec((1, 1, N_EXPERTS), lambda i, *_: (i, 0, 0), memory_space=pltpu.SMEM)
    row = lambda w: pl.BlockSpec((TOK_TILE, w), lambda i, *_: (i, 0))
    grid_spec = pltpu.PrefetchScalarGridSpec(
        num_scalar_prefetch=4,
        grid=(n_t,),
        in_specs=[smem, smem, smem, row(D_MODEL), row(TOP_K), row(TOP_K),
                  pl.BlockSpec((1, 1, N_EXPERTS), lambda i, *_: (i, 0, 0))],
        out_specs=pl.BlockSpec(memory_space=pl.ANY),
        scratch_shapes=[pltpu.VMEM((2, SLOT_ROWS, D_MODEL), F32), pltpu.VMEM((FFN_TILE, D_MODEL), F32),
                        pltpu.SemaphoreType.DMA((3,))],
    )
    return pl.pallas_call(
        functools.partial(_dispatch_kernel, n_t=n_t, n_tiles=n_tiles),
        grid_spec=grid_spec,
        out_shape=jax.ShapeDtypeStruct((n_tiles * FFN_TILE, D_MODEL), F32),
        compiler_params=_cparams(("arbitrary",)),
        name="dispatch",
    )(tile_tot, tail_n, tail_off, n_used, c8, lbase, gbase, h3, ids, lrank, lbase.astype(F32))


def _ffn_kernel(te_ref, ne_ref, lim_ref, nt_ref, xs_ref, wg_hbm, bg_ref, wu_hbm, bu_ref, wd_hbm, bd_ref, y_ref,
                wf32, wbf, turn, sem):
    i = pl.program_id(0)
    n_used = nt_ref[0]
    e = te_ref[i]

    def fetch(expert, s):
        return [pltpu.make_async_copy(w_hbm.at[0, expert], wf32.at[s, j], sem.at[s])
                for j, w_hbm in enumerate((wg_hbm, wu_hbm, wd_hbm))]

    @pl.when(i == 0)
    def _():
        turn[0] = 0
        for cp in fetch(e, 0):
            cp.start()

    @pl.when(jnp.logical_and(i < n_used, jnp.logical_or(i == 0, e != te_ref[jnp.maximum(i - 1, 0)])))
    def _():
        s = turn[0]
        for cp in fetch(e, s):
            cp.wait()
        nxt = i + ne_ref[e]

        @pl.when(nxt < n_used)
        def _():
            for cp in fetch(te_ref[jnp.minimum(nxt, te_ref.shape[0] - 1)], 1 - s):
                cp.start()

        for j in range(3):
            wbf[j] = wf32[s, j].astype(BF16)
        turn[0] = 1 - s

    def expert(rows):
        x = xs_ref[0:rows, :].astype(BF16)
        gate = jnp.minimum(jnp.dot(x, wbf[0], preferred_element_type=F32) + bg_ref[0, 0], SWIGLU_LIMIT)
        up = jnp.clip(jnp.dot(x, wbf[1], preferred_element_type=F32) + bu_ref[0, 0],
                      -SWIGLU_LIMIT, SWIGLU_LIMIT)
        hid = (up + 1.0) * gate * _sigmoid(gate * SWIGLU_ALPHA)
        y_ref[0:rows, :] = jnp.dot(hid.astype(BF16), wbf[2], preferred_element_type=F32) + bd_ref[0, 0]

    half = FFN_TILE // 2
    valid = lim_ref[e] - i * FFN_TILE

    @pl.when(jnp.logical_and(i < n_used, valid > half))
    def _():
        expert(FFN_TILE)

    @pl.when(jnp.logical_and(i < n_used, valid <= half))
    def _():
        expert(half)
        y_ref[half:, :] = jnp.zeros((FFN_TILE - half, D_MODEL), F32)

    @pl.when(i >= n_used)
    def _():
        y_ref[...] = jnp.zeros_like(y_ref)


def _ffn(tile_expert, expert_tiles, expert_end, n_used, xs, wg, bg, wu, bu, wd, bd):
    n_tiles = xs.shape[0] // FFN_TILE
    anyspec = pl.BlockSpec(memory_space=pl.ANY)
    bspec = pl.BlockSpec((1, 1, 1, D_MODEL), lambda i, te, ne, lim, nt: (0, te[i], 0, 0))
    grid_spec = pltpu.PrefetchScalarGridSpec(
        num_scalar_prefetch=4,
        grid=(n_tiles,),
        in_specs=[pl.BlockSpec((FFN_TILE, D_MODEL), lambda i, te, ne, lim, nt: (jnp.minimum(i, nt[0] - 1), 0)),
                  anyspec, bspec, anyspec, bspec, anyspec, bspec],
        out_specs=pl.BlockSpec((FFN_TILE, D_MODEL), lambda i, te, ne, lim, nt: (i, 0)),
        scratch_shapes=[pltpu.VMEM((2, 3, D_MODEL, D_MODEL), F32), pltpu.VMEM((3, D_MODEL, D_MODEL), BF16),
                        pltpu.SMEM((1,), I32), pltpu.SemaphoreType.DMA((2,))],
    )
    return pl.pallas_call(
        _ffn_kernel,
        grid_spec=grid_spec,
        out_shape=jax.ShapeDtypeStruct((n_tiles * FFN_TILE, D_MODEL), F32),
        compiler_params=_cparams(("arbitrary",)),
        name="ffn",
    )(tile_expert, expert_tiles, expert_end, n_used, xs, wg, bg, wu, bu, wd, bd)


def _combine_kernel(tot_ref, c8_ref, lb_ref, gb_ref, c8n_ref, lbn_ref, gbn_ref,
                    ys_hbm, x2_ref, ids_ref, lrank_ref, prob_ref, lbv_ref, g_ref,
                    op_ref, os_ref, sbuf, sem, *, n_pt, n_t):
    i = pl.program_id(0)
    slot = lax.rem(i, 2)

    def fetch(c8, lb, gb, dst_slot):
        _strip_copies(c8, lb, gb,
                      lambda lr, gr, rows: pltpu.make_async_copy(ys_hbm.at[pl.ds(gr, rows), :],
                                                                 sbuf.at[dst_slot, pl.ds(lr, rows), :],
                                                                 sem.at[dst_slot]))

    @pl.when(i == 0)
    def _():
        sbuf[...] = jnp.zeros_like(sbuf)
        fetch(c8_ref, lb_ref, gb_ref, 0)

    @pl.when(i + 1 < n_t)
    def _():
        fetch(c8n_ref, lbn_ref, gbn_ref, 1 - slot)

    rows = pl.multiple_of(tot_ref[i], STRIP)
    pltpu.make_async_copy(ys_hbm.at[pl.ds(0, rows), :], sbuf.at[slot, pl.ds(0, rows), :], sem.at[slot]).wait()
    strips = sbuf[slot].astype(BF16)
    half = TOK_TILE // COMBINE_PARTS
    outs = []
    for r0 in range(0, TOK_TILE, half):
        rs = slice(r0, r0 + half)
        prob = prob_ref[rs, :]
        selw = _slot_select(ids_ref[rs, :], lrank_ref[rs, :], lbv_ref[0],
                            [prob[:, kq:kq + 1] for kq in range(TOP_K)])
        y = jnp.dot(selw, strips, preferred_element_type=F32)
        outs.append(_rms(x2_ref[rs, :] + y, g_ref[...]))

    @pl.when(i < n_pt)
    def _():
        for j, part in enumerate(outs):
            op_ref[j * half:(j + 1) * half, :] = part

    @pl.when(i >= n_pt)
    def _():
        for j, part in enumerate(outs):
            os_ref[j * half:(j + 1) * half, :] = part


def _combine(tile_tot, c8, lbase, gbase, ys, x2, ids, lrank, prob, g_final, n_pt):
    n_t = c8.shape[0]
    last = n_t - 1
    cur = pl.BlockSpec((1, 1, N_EXPERTS), lambda i, *_: (i, 0, 0), memory_space=pltpu.SMEM)
    nxt = pl.BlockSpec((1, 1, N_EXPERTS), lambda i, *_: (jnp.minimum(i + 1, last), 0, 0), memory_space=pltpu.SMEM)
    row = lambda w: pl.BlockSpec((TOK_TILE, w), lambda i, *_: (i, 0))
    grid_spec = pltpu.PrefetchScalarGridSpec(
        num_scalar_prefetch=1,
        grid=(n_t,),
        in_specs=[cur, cur, cur, nxt, nxt, nxt,
                  pl.BlockSpec(memory_space=pl.ANY),
                  row(D_MODEL), row(TOP_K), row(TOP_K), row(TOP_K),
                  pl.BlockSpec((1, 1, N_EXPERTS), lambda i, *_: (i, 0, 0)),
                  pl.BlockSpec((1, D_MODEL), lambda i, *_: (0, 0))],
        out_specs=[pl.BlockSpec((TOK_TILE, D_MODEL), lambda i, *_: (jnp.minimum(i, n_pt - 1), 0)),
                   pl.BlockSpec((TOK_TILE, D_MODEL), lambda i, *_: (jnp.maximum(i - n_pt, 0), 0))],
        scratch_shapes=[pltpu.VMEM((2, SLOT_ROWS, D_MODEL), F32), pltpu.SemaphoreType.DMA((2,))],
    )
    return pl.pallas_call(
        functools.partial(_combine_kernel, n_pt=n_pt, n_t=n_t),
        grid_spec=grid_spec,
        out_shape=[jax.ShapeDtypeStruct((n_pt * TOK_TILE, D_MODEL), F32),
                   jax.ShapeDtypeStruct(((n_t - n_pt) * TOK_TILE, D_MODEL), F32)],
        compiler_params=_cparams(("arbitrary",)),
        name="combine",
    )(tile_tot, c8, lbase, gbase, c8, lbase, gbase, ys, x2, ids, lrank, prob, lbase.astype(F32), g_final)


def kernel(x_prompt, x_sample, mem_prompt, state_shift, state_wkv, state_conv, cache_mem_k, cache_mem_v,
           g_mix, w_in, mu_shift, w0, w_decay_up, a0, w_iclr_up, w_glora_up, k_k, k_a, r_k, gn_g, gn_b,
           dw_w, dw_b, cln_g, cln_b, w_out, g_xattn, g_mem, w_q, w_mk, w_mv, w_o,
           g_moe, w_router, b_router, w_moe_gate, b_moe_gate, w_moe_up, b_moe_up, w_moe_down, b_moe_down,
           g_final):
    bp, tp, _ = x_prompt.shape
    bs, ts, _ = x_sample.shape
    n_p, n_s = bp * tp, bs * ts
    n_all = n_p + n_s
    assert tp % WKV_BLOCK == 0 and tp % TOK_TILE == 0 and n_s == TOK_TILE
    assert ts <= SAMPLE_PAD and bs % SAMPLE_SEQS == 0 and bs % ATTN_SEQS == 0
    row2 = lambda a: a.reshape(1, -1)

    x_p = x_prompt.reshape(n_p, D_MODEL)
    x_s = x_sample.reshape(n_s, D_MODEL)
    p_rw, u = _inproj(x_p, x_s, row2(g_mix[0]), w_in[0].astype(BF16))

    head_idx = jnp.arange(RW_WIDTH) // RW_HEAD_DIM
    eseg = (head_idx[:, None] == head_idx[None, :]).astype(BF16)
    wts = (row2(mu_shift[0]), row2(w0[0]), w_decay_up[0].astype(BF16), row2(a0[0]),
           w_iclr_up[0].astype(BF16), w_glora_up[0].astype(BF16), row2(k_k[0]), row2(k_a[0]),
           row2(r_k[0]), row2(gn_g[0]), row2(gn_b[0]), eseg)
    conv_w = (dw_w[0], row2(dw_b[0]), row2(cln_g[0]), row2(cln_b[0]))
    n_tp = tp // WKV_BLOCK
    rw_p, wkv_p, cv_p = _mixer(p_rw, u, 0, jnp.zeros((bp, 1, RW_PROJ), F32),
                               jnp.zeros((1, bp, RW_HEADS, RW_HEAD_DIM, RW_HEAD_DIM), F32),
                               jnp.zeros((1, bp, CONV_K - 1, CONV_WIDTH), F32), wts, conv_w,
                               batch=bp, n_seq=1, tt_seq=WKV_BLOCK, chunk=WKV_CHUNK, t_valid=tp, n_t=n_tp,
                               out_rows=n_p)
    pad_seq = lambda a: jnp.pad(a.reshape(bs, ts, -1), ((0, 0), (0, SAMPLE_PAD - ts), (0, 0))).reshape(
        bs * SAMPLE_PAD, -1)
    unpad_seq = lambda a: a.reshape(bs, SAMPLE_PAD, -1)[:, :ts].reshape(n_s, -1)
    p_rw_s = p_rw[n_p:].reshape(bs, ts, RW_PROJ)
    u_s = u[n_p:]
    rw_s_pad, wkv_s, cv_s_pad = _mixer(pad_seq(p_rw_s), pad_seq(u_s), 0, state_shift[0].reshape(bs, 1, RW_PROJ),
                                       state_wkv, state_conv, wts, conv_w,
                                       batch=bs, n_seq=SAMPLE_SEQS, tt_seq=SAMPLE_PAD, chunk=SAMPLE_PAD,
                                       t_valid=ts, n_t=1, out_rows=bs * SAMPLE_PAD)
    rw_s, cv_s = unpad_seq(rw_s_pad), unpad_seq(cv_s_pad)

    x1, q = _mid1(x_p, x_s, rw_p, rw_s, cv_p, cv_s, w_out[0].astype(BF16), row2(g_xattn[0]), w_q[0].astype(BF16))

    mk_p, mv_p = _memkv(mem_prompt.reshape(bp * N_MEM, D_MODEL), row2(g_mem[0]),
                        w_mk[0].astype(BF16), w_mv[0].astype(BF16))
    mk_p = mk_p.reshape(bp, N_MEM, D_MODEL)
    mv_p = mv_p.reshape(bp, N_MEM, D_MODEL)
    o_p = _attn_rows(q, mk_p, mv_p, n_p, tp // TOK_TILE)
    o_s = unpad_seq(_attn_seq(pad_seq(q[n_p:]).reshape(bs, SAMPLE_PAD, D_MODEL), cache_mem_k, cache_mem_v))

    tri = (jnp.arange(TOK_TILE)[None, :] < jnp.arange(TOK_TILE)[:, None]).astype(BF16)
    x2, h3, ids, prob, lrank, cnt3, base3, tot = _mid2(x1, o_p, o_s, w_o[0].astype(BF16), row2(g_moe[0]),
                                                       w_router[0], row2(b_router[0]), tri)

    n_t = n_all // TOK_TILE
    cnt = cnt3.astype(I32)
    c8 = ((cnt + STRIP - 1) // STRIP) * STRIP
    lbase = jnp.cumsum(c8, axis=-1) - c8
    tile_tot = jnp.sum(c8, axis=(1, 2))
    used = tot[0].astype(I32)
    padded = ((used + FFN_TILE - 1) // FFN_TILE) * FFN_TILE
    ends = jnp.cumsum(padded)
    offs = ends - padded
    gbase = offs[None, None, :] + base3.astype(I32)
    tail_n = padded - used
    n_tiles = -(-(n_all * TOP_K + n_t * N_EXPERTS * (STRIP - 1)) // FFN_TILE) + N_EXPERTS
    n_used = (ends[-1] // FFN_TILE).reshape(1)
    tile_ids = jnp.minimum(jnp.arange(n_tiles, dtype=I32), n_used - 1)
    tile_expert = jnp.sum((ends[None, :] // FFN_TILE) <= tile_ids[:, None], axis=1).astype(I32)
    tile_expert = jnp.minimum(tile_expert, N_EXPERTS - 1)

    xs = _dispatch(tile_tot, tail_n, offs + used, n_used, c8, lbase, gbase, h3, ids, lrank, n_tiles)
    bias4 = lambda b: b.reshape(1, N_EXPERTS, 1, D_MODEL)
    ys = _ffn(tile_expert, padded // FFN_TILE, offs + used, n_used, xs, w_moe_gate, bias4(b_moe_gate),
              w_moe_up, bias4(b_moe_up), w_moe_down, bias4(b_moe_down))

    y_p, y_s = _combine(tile_tot, c8, lbase, gbase, ys, x2, ids, lrank, prob, row2(g_final), n_p // TOK_TILE)

    last_rows = lambda a, rows: jnp.stack([a[(b + 1) * tp - rows:(b + 1) * tp] for b in range(bp)])
    new_conv_s = jnp.concatenate([state_conv[0], u_s.reshape(bs, ts, CONV_WIDTH)], axis=1)[:, -(CONV_K - 1):]
    kv_shape = (1, bp, N_MEM, X_HEADS, X_HEAD_DIM)
    return (y_p.reshape(bp, tp, D_MODEL), y_s.reshape(bs, ts, D_MODEL),
            last_rows(p_rw, 1).reshape(1, bp, RW_PROJ), wkv_p[None],
            last_rows(u, CONV_K - 1)[None],
            mk_p.reshape(kv_shape), mv_p.reshape(kv_shape),
            p_rw_s[:, -1][None], wkv_s[None], new_conv_s[None])
```
